```python
import math
import jax, jax.numpy as jnp
from jax import lax
import numpy as np

D_MODEL = 1024
BATCH = 8
SEQ = 4096
DEPTH = 1

HEAD_DIM = 64
N_HEADS_A = 8
N_IDX_HEADS = 8
IDX_DIM = 32
TOPK_TOKENS = 256
N_HEADS_B = 8
N_KV_GROUPS_B = 2
HEADS_PER_GROUP = N_HEADS_B // N_KV_GROUPS_B
CMP_BLOCK = 32
CMP_STRIDE = 16
CMP_HIDDEN = 128
SLC_BLOCK = 64
N_SLC_BLOCKS = 16
WINDOW = 512
FORCED_SCORE = 1.0e4
N_BUCKETS = 32
MAX_DISTANCE = 128
D_FF = 4 * D_MODEL
Q_BLOCK_A = 128
Q_BLOCK_B = 64
EPS = 1e-6

W_QA = N_HEADS_A * HEAD_DIM
W_KA = HEAD_DIM
W_VA = HEAD_DIM
W_QI = N_IDX_HEADS * IDX_DIM
W_KI = IDX_DIM
W_WI = N_IDX_HEADS
W_QB = N_HEADS_B * HEAD_DIM
W_KVB = 6 * N_KV_GROUPS_B * HEAD_DIM
W_GB = 3 * N_HEADS_B
W_GATE = 2 * D_MODEL
D_IN = W_QA + W_KA + W_VA + W_QI + W_KI + W_WI + W_QB + W_KVB + W_GB + W_GATE

kernel_name = "hybrid_dsa_nsa_gated_block"


def rmsnorm(x, g):
    xf = x.astype(jnp.float32)
    y = xf * lax.rsqrt(jnp.mean(xf * xf, axis=-1, keepdims=True) + EPS)
    return (y * g.astype(jnp.float32)).astype(x.dtype)


def t5_bucket(dist):
    n = jnp.maximum(dist, 0)
    max_exact = N_BUCKETS // 2
    nf = jnp.maximum(n, 1).astype(jnp.float32)
    large = max_exact + (jnp.log(nf / max_exact) / math.log(MAX_DISTANCE / max_exact)
                         * (N_BUCKETS - max_exact)).astype(jnp.int32)
    large = jnp.minimum(large, N_BUCKETS - 1)
    return jnp.where(n < max_exact, n, large)


def masked_softmax(logits, mask):
    lf = jnp.where(mask, logits.astype(jnp.float32), -jnp.inf)
    m = jnp.max(lf, axis=-1, keepdims=True)
    m = jnp.where(jnp.isfinite(m), m, 0.0)
    e = jnp.where(mask, jnp.exp(lf - m), 0.0)
    return e / jnp.maximum(jnp.sum(e, axis=-1, keepdims=True), 1e-30)


def dsa_mixer(q_a, k_a, v_a, q_i, k_i, w_i, table_a):
    B, S = q_a.shape[:2]
    k_top = min(TOPK_TOKENS, S // 4)
    n_blk = S // Q_BLOCK_A
    scale = HEAD_DIM ** -0.5
    w_scale = (N_IDX_HEADS ** -0.5) * (IDX_DIM ** -0.5)
    key_pos = jnp.arange(S)
    gather = jax.vmap(lambda kb, ib: kb[ib])

    def block(i):
        q0 = i * Q_BLOCK_A
        t = q0 + jnp.arange(Q_BLOCK_A)
        qi = lax.dynamic_slice_in_dim(q_i, q0, Q_BLOCK_A, axis=1)
        wi = lax.dynamic_slice_in_dim(w_i, q0, Q_BLOCK_A, axis=1).astype(jnp.float32) * w_scale
        s_h = jnp.einsum('bqhd,bsd->bqhs', qi, k_i).astype(jnp.float32)
        score = jnp.einsum('bqh,bqhs->bqs', wi, jax.nn.relu(s_h))
        causal = key_pos[None, :] <= t[:, None]
        score = jnp.where(causal[None], score, -jnp.inf)
        _, idx = lax.top_k(score, k_top)
        k_sel = gather(k_a, idx)
        v_sel = gather(v_a, idx)
        qa = lax.dynamic_slice_in_dim(q_a, q0, Q_BLOCK_A, axis=1)
        dist = t[None, :, None] - idx
        bias = table_a[t5_bucket(dist)].transpose(0, 1, 3, 2)
        logits = jnp.einsum('bqhd,bqkd->bqhk', qa, k_sel) * scale + bias
        p = masked_softmax(logits, (dist >= 0)[:, :, None, :])
        o = jnp.einsum('bqhk,bqkd->bqhd', p.astype(v_sel.dtype), v_sel)
        return o.reshape(B, Q_BLOCK_A, N_HEADS_A * HEAD_DIM)

    out = lax.map(block, jnp.arange(n_blk))
    return out.transpose(1, 0, 2, 3).reshape(B, S, N_HEADS_A * HEAD_DIM)


def compress_blocks(kv, pe, w1, w2):
    B, S, G, D = kv.shape
    n_c = (S - CMP_BLOCK) // CMP_STRIDE + 1
    idx = jnp.arange(n_c)[:, None] * CMP_STRIDE + jnp.arange(CMP_BLOCK)[None, :]
    blk = kv[:, idx] + pe[None, None, :, None, :]
    blk = blk.transpose(0, 1, 3, 2, 4).reshape(B, n_c, G, CMP_BLOCK * D)
    return jax.nn.gelu(blk @ w1) @ w2


def nsa_mixer(q, kc, vc, ks, vs, kw, vw, gates, pe_k, w1_k, w2_k, pe_v, w1_v, w2_v, table_b):
    B, S = q.shape[:2]
    G, R, D = N_KV_GROUPS_B, HEADS_PER_GROUP, HEAD_DIM
    QB = Q_BLOCK_B
    scale = D ** -0.5
    q = q.reshape(B, S, G, R, D)
    gates = jax.nn.sigmoid(gates.astype(jnp.float32)).reshape(B, S, 3, G, R)
    kc_cmp = compress_blocks(kc, pe_k, w1_k, w2_k)
    vc_cmp = compress_blocks(vc, pe_v, w1_v, w2_v)
    n_c = kc_cmp.shape[1]
    n_s = S // SLC_BLOCK
    n_sel = min(N_SLC_BLOCKS, n_s)
    cmp_start = jnp.arange(n_c) * CMP_STRIDE
    cmp_end = cmp_start + CMP_BLOCK - 1
    slc_start = jnp.arange(n_s) * SLC_BLOCK
    overlap = jnp.clip(jnp.minimum(cmp_start[:, None] + CMP_BLOCK, slc_start[None, :] + SLC_BLOCK)
                       - jnp.maximum(cmp_start[:, None], slc_start[None, :]), 0, None
                       ).astype(jnp.float32) / CMP_BLOCK
    ks_blk = ks.reshape(B, n_s, SLC_BLOCK, G, D).transpose(0, 3, 1, 2, 4)
    vs_blk = vs.reshape(B, n_s, SLC_BLOCK, G, D).transpose(0, 3, 1, 2, 4)
    gather = jax.vmap(jax.vmap(lambda kb, ib: kb[ib]))
    pad = ((0, 0), (WINDOW, 0), (0, 0), (0, 0))
    kw_p = jnp.pad(kw, pad)
    vw_p = jnp.pad(vw, pad)
    tg = table_b.reshape(N_BUCKETS, G, R).transpose(1, 0, 2)
    g_idx = jnp.arange(G)[None, :, None, None]
    blk_id = jnp.arange(n_s)
    KW = WINDOW + QB

    def block(i):
        q0 = i * QB
        t = q0 + jnp.arange(QB)
        qb = lax.dynamic_slice_in_dim(q, q0, QB, axis=1)
        gb = lax.dynamic_slice_in_dim(gates, q0, QB, axis=1)
        dist_c = t[:, None] - cmp_end[None, :]
        bias_c = table_b[t5_bucket(dist_c)].reshape(QB, n_c, G, R).transpose(0, 2, 3, 1)
        logit_c = jnp.einsum('bqgrd,bcgd->bqgrc', qb, kc_cmp) * scale + bias_c
        p_c = masked_softmax(logit_c, (dist_c >= 0)[:, None, None, :])
        o_c = jnp.einsum('bqgrc,bcgd->bqgrd', p_c.astype(vc_cmp.dtype), vc_cmp)
        imp = jnp.einsum('bqgrc,cn->bqgn', p_c, overlap)
        cur = t // SLC_BLOCK
        valid = blk_id[None, :] <= cur[:, None]
        forced = ((blk_id[None, :] == 0) | (blk_id[None, :] == cur[:, None])
                  | (blk_id[None, :] == cur[:, None] - 1))
        score = jnp.where(valid[None, :, None, :],
                          jnp.where(forced[None, :, None, :], FORCED_SCORE, imp), -jnp.inf)
        _, sel = lax.top_k(score, n_sel)
        sel_t = sel.transpose(0, 2, 1, 3)
        k_sel = gather(ks_blk, sel_t).reshape(B, G, QB, n_sel * SLC_BLOCK, D)
        v_sel = gather(vs_blk, sel_t).reshape(B, G, QB, n_sel * SLC_BLOCK, D)
        pos_s = (sel_t[..., None] * SLC_BLOCK + jnp.arange(SLC_BLOCK)).reshape(B, G, QB, n_sel * SLC_BLOCK)
        dist_s = t[None, None, :, None] - pos_s
        bias_s = tg[g_idx, t5_bucket(dist_s)].transpose(0, 1, 2, 4, 3)
        logit_s = jnp.einsum('bqgrd,bgqkd->bgqrk', qb, k_sel) * scale + bias_s
        p_s = masked_softmax(logit_s, (dist_s >= 0)[:, :, :, None, :])
        o_s = jnp.einsum('bgqrk,bgqkd->bqgrd', p_s.astype(v_sel.dtype), v_sel)
        kwb = lax.dynamic_slice_in_dim(kw_p, q0, KW, axis=1)
        vwb = lax.dynamic_slice_in_dim(vw_p, q0, KW, axis=1)
        pos_w = q0 - WINDOW + jnp.arange(KW)
        dist_w = t[:, None] - pos_w[None, :]
        mask_w = (dist_w >= 0) & (dist_w < WINDOW) & (pos_w[None, :] >= 0)
        bias_w = table_b[t5_bucket(dist_w)].reshape(QB, KW, G, R).transpose(0, 2, 3, 1)
        logit_w = jnp.einsum('bqgrd,bkgd->bqgrk', qb, kwb) * scale + bias_w
        p_w = masked_softmax(logit_w, mask_w[:, None, None, :])
        o_w = jnp.einsum('bqgrk,bkgd->bqgrd', p_w.astype(vwb.dtype), vwb)
        o = (gb[:, :, 0, :, :, None] * o_c + gb[:, :, 1, :, :, None] * o_s
             + gb[:, :, 2, :, :, None] * o_w)
        return o.astype(q.dtype).reshape(B, QB, G * R * D)

    out = lax.map(block, jnp.arange(S // QB))
    return out.transpose(1, 0, 2, 3).reshape(B, S, N_HEADS_B * HEAD_DIM)


def setup_inputs(seed: int = 0) -> dict:
    key = jax.random.key(seed)
    ks = jax.random.split(key, 20)
    f32 = jnp.float32
    nrm = lambda k, shape, s: jax.random.normal(k, shape, f32) * s
    L, D = CMP_BLOCK, HEAD_DIM
    return {
        "x": nrm(ks[0], (BATCH, SEQ, D_MODEL), 1.0),
        "norm_mix": 1.0 + nrm(ks[1], (DEPTH, D_MODEL), 0.02),
        "w_in": nrm(ks[2], (DEPTH, D_MODEL, D_IN), D_MODEL ** -0.5),
        "cmp_pe_k": nrm(ks[3], (DEPTH, L, D), 0.1),
        "cmp_w1_k": nrm(ks[4], (DEPTH, L * D, CMP_HIDDEN), (L * D) ** -0.5),
        "cmp_w2_k": nrm(ks[5], (DEPTH, CMP_HIDDEN, D), CMP_HIDDEN ** -0.5),
        "cmp_pe_v": nrm(ks[6], (DEPTH, L, D), 0.1),
        "cmp_w1_v": nrm(ks[7], (DEPTH, L * D, CMP_HIDDEN), (L * D) ** -0.5),
        "cmp_w2_v": nrm(ks[8], (DEPTH, CMP_HIDDEN, D), CMP_HIDDEN ** -0.5),
        "rel_bias": nrm(ks[9], (N_BUCKETS, N_HEADS_A + N_HEADS_B), 0.5),
        "w_branch_a": nrm(ks[10], (DEPTH, W_QA, D_MODEL), W_QA ** -0.5),
        "w_branch_b": nrm(ks[11], (DEPTH, W_QB, D_MODEL), W_QB ** -0.5),
        "w_out": nrm(ks[12], (DEPTH, D_MODEL, D_MODEL), D_MODEL ** -0.5),
        "norm_mlp": 1.0 + nrm(ks[13], (DEPTH, D_MODEL), 0.02),
        "w_mlp_in": nrm(ks[14], (DEPTH, D_MODEL, D_FF), D_MODEL ** -0.5),
        "w_mlp_out": nrm(ks[15], (DEPTH, D_FF, D_MODEL), D_FF ** -0.5),
        "norm_final": 1.0 + nrm(ks[16], (D_MODEL,), 0.02),
    }


def reference(x, norm_mix, w_in, cmp_pe_k, cmp_w1_k, cmp_w2_k, cmp_pe_v, cmp_w1_v, cmp_w2_v,
              rel_bias, w_branch_a, w_branch_b, w_out, norm_mlp, w_mlp_in, w_mlp_out, norm_final):
    B, S, _ = x.shape
    G, D = N_KV_GROUPS_B, HEAD_DIM
    widths = [W_QA, W_KA, W_VA, W_QI, W_KI, W_WI, W_QB, W_KVB, W_GB, W_GATE]
    split_pts = [int(v) for v in np.cumsum(widths)[:-1]]
    table_a = rel_bias[:, :N_HEADS_A]
    table_b = rel_bias[:, N_HEADS_A:]
    for l in range(DEPTH):
        h = rmsnorm(x, norm_mix[l])
        proj = h @ w_in[l]
        q_a, k_a, v_a, q_i, k_i, w_i, q_b, kv_b, g_b, g_br = jnp.split(proj, split_pts, axis=-1)
        o_a = dsa_mixer(q_a.reshape(B, S, N_HEADS_A, D), k_a, v_a,
                        q_i.reshape(B, S, N_IDX_HEADS, IDX_DIM), k_i, w_i, table_a)
        kv_b = kv_b.reshape(B, S, 6, G, D)
        o_b = nsa_mixer(q_b.reshape(B, S, N_HEADS_B, D), kv_b[:, :, 0], kv_b[:, :, 1],
                        kv_b[:, :, 2], kv_b[:, :, 3], kv_b[:, :, 4], kv_b[:, :, 5], g_b,
                        cmp_pe_k[l], cmp_w1_k[l], cmp_w2_k[l], cmp_pe_v[l], cmp_w1_v[l],
                        cmp_w2_v[l], table_b)
        gate_a, gate_b = jnp.split(g_br, 2, axis=-1)
        mix = (jax.nn.sigmoid(gate_a) * (o_a @ w_branch_a[l])
               + jax.nn.sigmoid(gate_b) * (o_b @ w_branch_b[l]))
        x = x + mix @ w_out[l]
        h = rmsnorm(x, norm_mlp[l])
        x = x + jnp.square(jax.nn.relu(h @ w_mlp_in[l])) @ w_mlp_out[l]
    return rmsnorm(x, norm_final)
```

```python
import functools
import math

import numpy as np
import jax
import jax.numpy as jnp
from jax import lax
from jax.experimental import pallas as pl
from jax.experimental.pallas import tpu as pltpu

f32 = jnp.float32
bf16 = jnp.bfloat16

HEAD_DIM = 64
N_HEADS_A = 8
N_IDX_HEADS = 8
IDX_DIM = 32
TOPK_TOKENS = 256
N_HEADS_B = 8
N_KV_GROUPS_B = 2
HEADS_PER_GROUP = N_HEADS_B // N_KV_GROUPS_B
CMP_BLOCK = 32
CMP_STRIDE = 16
CMP_HIDDEN = 128
SLC_BLOCK = 64
N_SLC_BLOCKS = 16
WINDOW = 512
FORCED_SCORE = 1.0e4
N_BUCKETS = 32
MAX_DISTANCE = 128
EPS = 1e-6

W_QA = N_HEADS_A * HEAD_DIM
W_QI = N_IDX_HEADS * IDX_DIM
W_QB = N_HEADS_B * HEAD_DIM
W_KVB = 6 * N_KV_GROUPS_B * HEAD_DIM
W_GB = 3 * N_HEADS_B

LANES = 128
TQ = 128
TK = 128
LOG2E = 1.4426950408889634
NEG_MASK = -2.0e30
M_INIT = -1.0e30
NEG_SCORE = -3.0e38
BIG_SCORE = 3.0e38
MAX_SEARCH_ITERS = 400
TIE_CHECK_START = 26
VMEM_LIMIT = 56 * 1024 * 1024


def _cparams(*sem):
    return pltpu.CompilerParams(dimension_semantics=sem, vmem_limit_bytes=VMEM_LIMIT)


def _rms(x, g):
    return x * lax.rsqrt(jnp.mean(x * x, axis=-1, keepdims=True) + EPS) * g


def _inproj_kernel(x_ref, g_ref, w_ref, qa_ref, kva_ref, qi_ref, kw_ref, qb_ref, kvb_ref, gate_ref):
    h = _rms(x_ref[...], g_ref[...]).astype(bf16)
    q_scale = HEAD_DIM ** -0.5 * LOG2E
    off = 0
    for ref, scale in ((qa_ref, q_scale), (kva_ref, None), (qi_ref, None), (kw_ref, None),
                       (qb_ref, q_scale), (kvb_ref, None), (gate_ref, None)):
        width = ref.shape[-1]
        for c0 in range(0, width, 512):
            c1 = min(c0 + 512, width)
            y = jnp.dot(h, w_ref[:, off + c0:off + c1], preferred_element_type=f32)
            if scale is not None:
                y = y * scale
            ref[:, c0:c1] = y.astype(ref.dtype)
        off += width


def _inproj(x2, g, w_perm, tm=256):
    rows, d = x2.shape
    widths = (W_QA, 2 * HEAD_DIM, W_QI, LANES, W_QB, W_KVB, 2 * d)
    dtypes = (bf16, bf16, f32, f32, bf16, bf16, f32)
    return pl.pallas_call(
        _inproj_kernel,
        grid=(rows // tm,),
        in_specs=[pl.BlockSpec((tm, d), lambda i: (i, 0)),
                  pl.BlockSpec((1, d), lambda i: (0, 0)),
                  pl.BlockSpec(w_perm.shape, lambda i: (0, 0))],
        out_specs=[pl.BlockSpec((tm, w), lambda i: (i, 0)) for w in widths],
        out_shape=[jax.ShapeDtypeStruct((rows, w), dt) for w, dt in zip(widths, dtypes)],
        compiler_params=_cparams("parallel"),
    )(x2, g, w_perm)


def _count_ge(sc_ref, nkb, p):
    def body(kb, acc):
        return acc + jnp.where(sc_ref[kb] >= p, 1.0, 0.0)
    acc = lax.fori_loop(0, nkb, body, jnp.zeros((TQ, LANES), f32))
    return jnp.sum(acc, axis=1, keepdims=True)


def _select_topk(sc_ref, nkb, rmin, rmax, nvalid, k):
    kf = float(k)
    active = nvalid > kf
    lo0 = jnp.where(active, rmin, NEG_SCORE)
    cl0 = jnp.where(active, nvalid, kf)
    hi0 = jnp.full_like(lo0, BIG_SCORE)
    zero = jnp.zeros_like(lo0)

    def pending(cl, tie):
        return (cl != kf) & (tie == 0.0)

    def cond(st):
        it, lo, hi, cl, ch, tie = st
        return (jnp.max(jnp.where(pending(cl, tie), 1.0, 0.0)) > 0.0) & (it < MAX_SEARCH_ITERS)

    def body(st):
        it, lo, hi, cl, ch, tie = st
        pend = pending(cl, tie)
        hie = jnp.minimum(hi, rmax)
        p = lo + (hie - lo) * 0.5
        p = jnp.where(p > lo, p, hie)
        cnt = _count_ge(sc_ref, nkb, p)
        ge = cnt >= kf
        up_lo = pend & ge
        up_hi = pend & jnp.logical_not(ge)
        lo = jnp.where(up_lo, p, lo)
        cl = jnp.where(up_lo, cnt, cl)
        hi = jnp.where(up_hi, p, hi)
        ch = jnp.where(up_hi, cnt, ch)

        def tie_check(args):
            lo, tie = args
            def scan(kb, c):
                dmin, dmax = c
                s = sc_ref[kb]
                dmin = jnp.minimum(dmin, jnp.where(s >= lo, s, BIG_SCORE))
                dmax = jnp.maximum(dmax, jnp.where(s < hi, s, NEG_SCORE))
                return dmin, dmax
            dmin, dmax = lax.fori_loop(
                0, nkb, scan,
                (jnp.full((TQ, LANES), BIG_SCORE, f32), jnp.full((TQ, LANES), NEG_SCORE, f32)))
            dmin = jnp.min(dmin, axis=1, keepdims=True)
            dmax = jnp.max(dmax, axis=1, keepdims=True)
            new_tie = pending(cl, tie) & (dmin == dmax)
            return jnp.where(new_tie, dmin, lo), jnp.where(new_tie, 1.0, tie)

        lo, tie = lax.cond(it >= TIE_CHECK_START, tie_check, lambda a: a, (lo, tie))
        return it + 1, lo, hi, cl, ch, tie

    _, lo, hi, cl, ch, tie = lax.while_loop(cond, body, (jnp.int32(0), lo0, hi0, cl0, zero, zero))
    any_tie = jnp.max(tie) > 0.0
    need = kf - ch

    @pl.when(any_tie)
    def _():
        r = lax.broadcasted_iota(jnp.int32, (LANES, LANES), 0)
        c = lax.broadcasted_iota(jnp.int32, (LANES, LANES), 1)
        triu = jnp.where(r <= c, 1.0, 0.0).astype(bf16)

        def body(kb, run):
            s = sc_ref[kb]
            eq = s == lo
            eqf = jnp.where(eq, 1.0, 0.0)
            cum = run + jnp.dot(eqf.astype(bf16), triu, preferred_element_type=f32)
            sel_tie = jnp.where((s > lo) | (eq & (cum <= need)), 1.0, -1.0)
            sel_all = jnp.where(s >= lo, 1.0, -1.0)
            sc_ref[kb] = jnp.where(tie > 0.0, sel_tie, sel_all)
            return run + jnp.sum(eqf, axis=1, keepdims=True)

        lax.fori_loop(0, nkb, body, zero)

    return jnp.where(any_tie, 0.0, lo)


def _rank_select(score, k):
    n, width = score.shape
    sub = 8
    tiles = [score[v * sub:(v + 1) * sub] for v in range(n // sub)]
    ridx = lax.broadcasted_iota(jnp.int32, (sub, width), 0)
    ranks = [jnp.zeros((sub, width), f32) for _ in tiles]
    for i in range(n):
        si = jnp.broadcast_to(tiles[i // sub][i % sub:i % sub + 1], (sub, width))
        for v, x in enumerate(tiles):
            ge = jnp.where(si >= x, 1.0, 0.0)
            gt = jnp.where(si > x, 1.0, 0.0)
            if v * sub > i:
                inc = ge
            elif (v + 1) * sub <= i:
                inc = gt
            else:
                inc = jnp.where(ridx > i % sub, ge, gt)
            ranks[v] = ranks[v] + inc
    flags = [jnp.where(r < float(k), 1.0, 0.0) for r in ranks]
    flags.append(jnp.zeros((LANES - n, width), f32))
    return jnp.concatenate(flags, axis=0)


def _attn_init(m_ref, l_ref, acc_ref):
    m_ref[...] = jnp.full(m_ref.shape, M_INIT, f32)
    l_ref[...] = jnp.zeros(l_ref.shape, f32)
    acc_ref[...] = jnp.zeros(acc_ref.shape, f32)


def _attn_step(qs, kT, v, mask, bias, m_ref, l_ref, acc_ref):
    for j, q in enumerate(qs):
        s = jnp.dot(q, kT, preferred_element_type=f32)
        if bias is not None:
            s = s + bias[j]
        if mask is not None:
            s = jnp.where(mask, s, NEG_MASK)
        m_old = m_ref[j]
        m_new = jnp.maximum(m_old, jnp.max(s, axis=1, keepdims=True))
        alpha = jnp.exp2(m_old - m_new)
        p = jnp.exp2(s - m_new)
        l_ref[j] = alpha * l_ref[j] + jnp.sum(p, axis=1, keepdims=True)
        acc_ref[j] = alpha * acc_ref[j] + jnp.dot(p.astype(bf16), v, preferred_element_type=f32)
        m_ref[j] = m_new


def _attn_out(j, l_ref, acc_ref):
    return acc_ref[j] / jnp.maximum(l_ref[j], 1e-30)


def _dsa_kernel(qi_ref, kw_ref, kiT_ref, qa_ref, kaT_ref, va_ref, bnear_ref, o_ref,
                sc_ref, m_ref, l_ref, acc_ref, *, k_top):
    i = pl.program_id(1)
    w_scale = (N_IDX_HEADS ** -0.5) * (IDX_DIM ** -0.5)
    w = kw_ref[:, IDX_DIM:IDX_DIM + N_IDX_HEADS] * w_scale
    qi = qi_ref[...]
    qih = [qi[:, h * IDX_DIM:(h + 1) * IDX_DIM].astype(bf16) for h in range(N_IDX_HEADS)]
    wb = [w[:, h:h + 1] for h in range(N_IDX_HEADS)]
    row = lax.broadcasted_iota(jnp.int32, (TQ, TK), 0)
    col = lax.broadcasted_iota(jnp.int32, (TQ, TK), 1)
    causal = col <= row

    def score_blk(kb):
        kblk = kiT_ref[kb]
        acc = None
        for h in range(N_IDX_HEADS):
            sh = jnp.dot(qih[h], kblk, preferred_element_type=f32)
            term = wb[h] * jnp.maximum(sh, 0.0)
            acc = term if acc is None else acc + term
        return acc

    def far_score(kb, carry):
        mn, mx = carry
        a = score_blk(kb)
        sc_ref[kb] = a
        return jnp.minimum(mn, a), jnp.maximum(mx, a)

    mn, mx = lax.fori_loop(0, i, far_score,
                           (jnp.full((TQ, TK), BIG_SCORE, f32), jnp.full((TQ, TK), NEG_SCORE, f32)))
    a = score_blk(i)
    sc_ref[i] = jnp.where(causal, a, NEG_SCORE)
    mn = jnp.minimum(mn, jnp.where(causal, a, BIG_SCORE))
    mx = jnp.maximum(mx, jnp.where(causal, a, NEG_SCORE))
    rmin = jnp.min(mn, axis=1, keepdims=True)
    rmax = jnp.max(mx, axis=1, keepdims=True)
    nvalid = (i * TQ + row[:, :1] + 1).astype(f32)
    lo = _select_topk(sc_ref, i + 1, rmin, rmax, nvalid, k_top)

    _attn_init(m_ref, l_ref, acc_ref)
    qs = [qa_ref[:, h * HEAD_DIM:(h + 1) * HEAD_DIM] for h in range(N_HEADS_A)]

    def far_attn(kb, _):
        _attn_step(qs, kaT_ref[kb], va_ref[kb], sc_ref[kb] >= lo, None, m_ref, l_ref, acc_ref)
        return 0

    lax.fori_loop(0, jnp.maximum(i - 1, 0), far_attn, 0)

    @pl.when(i >= 1)
    def _():
        bias = [bnear_ref[h, :, 0:TK] for h in range(N_HEADS_A)]
        _attn_step(qs, kaT_ref[i - 1], va_ref[i - 1], sc_ref[i - 1] >= lo, bias, m_ref, l_ref, acc_ref)

    bias = [bnear_ref[h, :, TK:2 * TK] for h in range(N_HEADS_A)]
    _attn_step(qs, kaT_ref[i], va_ref[i], (sc_ref[i] >= lo) & causal, bias, m_ref, l_ref, acc_ref)
    for h in range(N_HEADS_A):
        o_ref[:, h * HEAD_DIM:(h + 1) * HEAD_DIM] = _attn_out(h, l_ref, acc_ref).astype(o_ref.dtype)


def _dsa(qi, kw, kiT, qa, kaT, va, bnear, B, S, k_top):
    nq = S // TQ
    nkb = S // TK
    row_spec = lambda w: pl.BlockSpec((TQ, w), lambda b, i: (b * nq + i, 0))
    return pl.pallas_call(
        functools.partial(_dsa_kernel, k_top=k_top),
        grid=(B, nq),
        in_specs=[row_spec(W_QI), row_spec(LANES),
                  pl.BlockSpec((None, nkb, IDX_DIM, TK), lambda b, i: (b, 0, 0, 0)),
                  row_spec(W_QA),
                  pl.BlockSpec((None, nkb, HEAD_DIM, TK), lambda b, i: (b, 0, 0, 0)),
                  pl.BlockSpec((None, nkb, TK, HEAD_DIM), lambda b, i: (b, 0, 0, 0)),
                  pl.BlockSpec((N_HEADS_A, TQ, 2 * TK), lambda b, i: (0, 0, 0))],
        out_specs=row_spec(W_QA),
        out_shape=jax.ShapeDtypeStruct((B * S, W_QA), bf16),
        scratch_shapes=[pltpu.VMEM((nkb, TQ, TK), f32),
                        pltpu.VMEM((N_HEADS_A, TQ, 1), f32),
                        pltpu.VMEM((N_HEADS_A, TQ, 1), f32),
                        pltpu.VMEM((N_HEADS_A, TQ, HEAD_DIM), f32)],
        compiler_params=_cparams("parallel", "parallel"),
    )(qi, kw, kiT, qa, kaT, va, bnear)


def _compress_kernel(x_ref, pe_ref, w1_ref, w1ab_ref, w2_ref, o_ref):
    nc = x_ref.shape[0]
    ab = jnp.dot(x_ref[...], w1ab_ref[...], preferred_element_type=f32)
    const = jnp.dot(pe_ref[...], w1_ref[...], preferred_element_type=f32)[0:1]
    first = ab[:, :CMP_HIDDEN]
    second = pltpu.roll(ab[:, CMP_HIDDEN:], nc - 1, 0)
    hid = jax.nn.gelu(first + second + const)
    o_ref[...] = jnp.dot(hid.astype(bf16), w2_ref[...], preferred_element_type=f32)


def _compress(xc, pe8, w1, w1ab, w2):
    B, _, G, nc, width = xc.shape
    return pl.pallas_call(
        _compress_kernel,
        grid=(B, 2, G),
        in_specs=[pl.BlockSpec((None, None, None, nc, width), lambda b, k, g: (b, k, g, 0, 0)),
                  pl.BlockSpec((None, 8, 2 * width), lambda b, k, g: (k, 0, 0)),
                  pl.BlockSpec((None, 2 * width, CMP_HIDDEN), lambda b, k, g: (k, 0, 0)),
                  pl.BlockSpec((None, width, 2 * CMP_HIDDEN), lambda b, k, g: (k, 0, 0)),
                  pl.BlockSpec((None, CMP_HIDDEN, HEAD_DIM), lambda b, k, g: (k, 0, 0))],
        out_specs=pl.BlockSpec((None, None, None, nc, HEAD_DIM), lambda b, k, g: (b, k, g, 0, 0)),
        out_shape=jax.ShapeDtypeStruct((B, 2, G, nc, HEAD_DIM), f32),
        compiler_params=_cparams("parallel", "parallel", "parallel"),
    )(xc, pe8, w1, w1ab, w2)


def _nsa_kernel(qb_ref, kw_ref, kcT_ref, vcc_ref, ksT_ref, vs_ref, kwT_ref, vw_ref,
                bnear_ref, bcmp_ref, ovT_ref, e3_ref, o_ref,
                m_ref, l_ref, acc_ref, *, n_sel):
    i = pl.program_id(1)
    nc = kcT_ref.shape[-1]
    R = HEADS_PER_GROUP
    row = lax.broadcasted_iota(jnp.int32, (TQ, TK), 0)
    col = lax.broadcasted_iota(jnp.int32, (TQ, TK), 1)
    causal = col <= row
    t_c = i * TQ + lax.broadcasted_iota(jnp.int32, (TQ, nc), 0)
    c_c = lax.broadcasted_iota(jnp.int32, (TQ, nc), 1)
    mask_c = c_c * CMP_STRIDE + (CMP_BLOCK - 1) <= t_c
    cmp_shift = (i * (TQ // CMP_STRIDE) + nc - 9) % nc
    n_s = ovT_ref.shape[0]
    blk = lax.broadcasted_iota(jnp.int32, (n_s, TQ), 0)
    cur = (i * TQ + lax.broadcasted_iota(jnp.int32, (n_s, TQ), 1)) // SLC_BLOCK
    valid = blk <= cur
    forced = (blk == 0) | (blk == cur) | (blk == cur - 1)
    gates = jax.nn.sigmoid(kw_ref[:, IDX_DIM + N_IDX_HEADS:IDX_DIM + N_IDX_HEADS + W_GB])
    win_blocks = WINDOW // TK

    for g in range(N_KV_GROUPS_B):
        heads = [g * R + r for r in range(R)]
        qs = [qb_ref[:, h * HEAD_DIM:(h + 1) * HEAD_DIM] for h in heads]

        psum = jnp.zeros((TQ, nc), f32)
        o_c = []
        for r, h in enumerate(heads):
            bias = pltpu.roll(bcmp_ref[h], cmp_shift, 1)
            s = jnp.dot(qs[r], kcT_ref[g], preferred_element_type=f32) + bias
            s = jnp.where(mask_c, s, NEG_MASK)
            m = jnp.max(s, axis=1, keepdims=True)
            e = jnp.where(mask_c, jnp.exp2(s - m), 0.0)
            p = e / jnp.maximum(jnp.sum(e, axis=1, keepdims=True), 1e-30)
            psum = psum + p
            o_c.append(jnp.dot(p.astype(bf16), vcc_ref[g], preferred_element_type=f32))

        p_hi = psum.astype(bf16)
        p_lo = (psum - p_hi.astype(f32)).astype(bf16)
        nt = (((1,), (1,)), ((), ()))
        imp = (lax.dot_general(ovT_ref[...], p_hi, nt, preferred_element_type=f32)
               + lax.dot_general(ovT_ref[...], p_lo, nt, preferred_element_type=f32))
        score = jnp.where(valid, jnp.where(forced, FORCED_SCORE, imp), NEG_SCORE)
        selm = _rank_select(score, n_sel).T.astype(bf16)

        _attn_init(m_ref, l_ref, acc_ref)

        def tok_mask(kb):
            return jnp.dot(selm, e3_ref[kb], preferred_element_type=f32) > 0.5

        def far_sel(kb, _):
            _attn_step(qs, ksT_ref[g, kb], vs_ref[g, kb], tok_mask(kb), None, m_ref, l_ref, acc_ref)
            return 0

        lax.fori_loop(0, jnp.maximum(i - 1, 0), far_sel, 0)
        bias_prev = [bnear_ref[h, :, 0:TK] for h in heads]
        bias_diag = [bnear_ref[h, :, TK:2 * TK] for h in heads]

        @pl.when(i >= 1)
        def _():
            _attn_step(qs, ksT_ref[g, i - 1], vs_ref[g, i - 1], tok_mask(i - 1), bias_prev,
                       m_ref, l_ref, acc_ref)

        _attn_step(qs, ksT_ref[g, i], vs_ref[g, i], tok_mask(i) & causal, bias_diag, m_ref, l_ref, acc_ref)
        o_s = [_attn_out(r, l_ref, acc_ref) for r in range(R)]

        _attn_init(m_ref, l_ref, acc_ref)
        for d in range(win_blocks, 1, -1):
            @pl.when(i >= d)
            def _(d=d):
                mask = (col > row) if d == win_blocks else None
                _attn_step(qs, kwT_ref[g, i - d], vw_ref[g, i - d], mask, None, m_ref, l_ref, acc_ref)

        @pl.when(i >= 1)
        def _():
            _attn_step(qs, kwT_ref[g, i - 1], vw_ref[g, i - 1], None, bias_prev, m_ref, l_ref, acc_ref)

        _attn_step(qs, kwT_ref[g, i], vw_ref[g, i], causal, bias_diag, m_ref, l_ref, acc_ref)

        for r, h in enumerate(heads):
            o_w = _attn_out(r, l_ref, acc_ref)
            o = (gates[:, h:h + 1] * o_c[r]
                 + gates[:, N_HEADS_B + h:N_HEADS_B + h + 1] * o_s[r]
                 + gates[:, 2 * N_HEADS_B + h:2 * N_HEADS_B + h + 1] * o_w)
            o_ref[:, h * HEAD_DIM:(h + 1) * HEAD_DIM] = o.astype(o_ref.dtype)


def _nsa(qb, kw, kcT, vcc, ksT, vs, kwT, vw, bnear, bcmp, ovT, e3, B, S, n_sel):
    nq = S // TQ
    nkb = S // TK
    nc = kcT.shape[-1]
    G = N_KV_GROUPS_B
    row_spec = lambda w: pl.BlockSpec((TQ, w), lambda b, i: (b * nq + i, 0))
    kT_spec = pl.BlockSpec((None, G, nkb, HEAD_DIM, TK), lambda b, i: (b, 0, 0, 0, 0))
    v_spec = pl.BlockSpec((None, G, nkb, TK, HEAD_DIM), lambda b, i: (b, 0, 0, 0, 0))
    const = lambda shape: pl.BlockSpec(shape, lambda b, i: (0,) * len(shape))
    return pl.pallas_call(
        functools.partial(_nsa_kernel, n_sel=n_sel),
        grid=(B, nq),
        in_specs=[row_spec(W_QB), row_spec(LANES),
                  pl.BlockSpec((None, G, HEAD_DIM, nc), lambda b, i: (b, 0, 0, 0)),
                  pl.BlockSpec((None, G, nc, HEAD_DIM), lambda b, i: (b, 0, 0, 0)),
                  kT_spec, v_spec, kT_spec, v_spec,
                  const(bnear.shape), const(bcmp.shape), const(ovT.shape), const(e3.shape)],
        out_specs=row_spec(W_QB),
        out_shape=jax.ShapeDtypeStruct((B * S, W_QB), bf16),
        scratch_shapes=[pltpu.VMEM((HEADS_PER_GROUP, TQ, 1), f32),
                        pltpu.VMEM((HEADS_PER_GROUP, TQ, 1), f32),
                        pltpu.VMEM((HEADS_PER_GROUP, TQ, HEAD_DIM), f32)],
        compiler_params=_cparams("parallel", "parallel"),
    )(qb, kw, kcT, vcc, ksT, vs, kwT, vw, bnear, bcmp, ovT, e3)


def _mix_kernel(oa_ref, ob_ref, gate_ref, x_ref, wa_ref, wb_ref, wo_ref, x1_ref):
    d = x_ref.shape[-1]
    ga = jax.nn.sigmoid(gate_ref[:, :d])
    gb = jax.nn.sigmoid(gate_ref[:, d:])
    mix = (ga * jnp.dot(oa_ref[...], wa_ref[...], preferred_element_type=f32)
           + gb * jnp.dot(ob_ref[...], wb_ref[...], preferred_element_type=f32))
    x1_ref[...] = x_ref[...] + jnp.dot(mix.astype(bf16), wo_ref[...], preferred_element_type=f32)


def _mix(oa, ob, gate, x2, wa, wb, wo, tm=512):
    rows, d = x2.shape
    rs = lambda w: pl.BlockSpec((tm, w), lambda i: (i, 0))
    cs = lambda a: pl.BlockSpec(a.shape, lambda i: (0, 0))
    return pl.pallas_call(
        _mix_kernel,
        grid=(rows // tm,),
        in_specs=[rs(W_QA), rs(W_QB), rs(2 * d), rs(d), cs(wa), cs(wb), cs(wo)],
        out_specs=rs(d),
        out_shape=jax.ShapeDtypeStruct((rows, d), f32),
        compiler_params=_cparams("parallel"),
    )(oa, ob, gate, x2, wa, wb, wo)


def _mlp_kernel(x1_ref, g_ref, w1_ref, w2_ref, gf_ref, o_ref, *, chunk):
    x1 = x1_ref[...]
    h = _rms(x1, g_ref[...]).astype(bf16)
    acc = jnp.zeros(x1.shape, f32)
    for c0 in range(0, w1_ref.shape[1], chunk):
        u = jnp.dot(h, w1_ref[:, c0:c0 + chunk], preferred_element_type=f32)
        u = jnp.square(jnp.maximum(u, 0.0)).astype(bf16)
        acc = acc + jnp.dot(u, w2_ref[c0:c0 + chunk, :], preferred_element_type=f32)
    o_ref[...] = _rms(x1 + acc, gf_ref[...])


def _mlp(x1, g, w1, w2, gf, tm=256, chunk=512):
    rows, d = x1.shape
    rs = pl.BlockSpec((tm, d), lambda i: (i, 0))
    cs = lambda a: pl.BlockSpec(a.shape, lambda i: (0, 0))
    ws = lambda a: pl.BlockSpec(a.shape, lambda i: (0, 0), pipeline_mode=pl.Buffered(1))
    return pl.pallas_call(
        functools.partial(_mlp_kernel, chunk=chunk),
        grid=(rows // tm,),
        in_specs=[rs, cs(g), ws(w1), ws(w2), cs(gf)],
        out_specs=rs,
        out_shape=jax.ShapeDtypeStruct((rows, d), f32),
        compiler_params=_cparams("parallel"),
    )(x1, g, w1, w2, gf)


def _t5_bucket(dist):
    n = jnp.maximum(dist, 0)
    max_exact = N_BUCKETS // 2
    nf = jnp.maximum(n, 1).astype(f32)
    large = max_exact + (jnp.log(nf / max_exact) / math.log(MAX_DISTANCE / max_exact)
                         * (N_BUCKETS - max_exact)).astype(jnp.int32)
    large = jnp.minimum(large, N_BUCKETS - 1)
    return jnp.where(n < max_exact, n, large)


def _bias_tables(rel_bias, nc):
    tab = rel_bias.astype(f32) * LOG2E
    by_dist = tab[_t5_bucket(jnp.arange(MAX_DISTANCE + 1))]
    shifted = (by_dist - by_dist[MAX_DISTANCE]).T
    r = np.arange(TQ)[:, None]
    dist = r + TK - np.arange(2 * TK)[None, :]
    near = shifted[:, np.clip(dist, 0, MAX_DISTANCE)]
    near = jnp.where(jnp.asarray(dist >= 0)[None], near, 0.0)
    j = np.arange(nc)[None, :]
    dist_c = r - (CMP_BLOCK - 1) - CMP_STRIDE * (j - 9)
    cmp = shifted[N_HEADS_A:, np.clip(dist_c, 0, MAX_DISTANCE)]
    cmp = jnp.where(jnp.asarray((dist_c >= 0) & (j < 16))[None], cmp, 0.0)
    return near[:N_HEADS_A], near[N_HEADS_A:], cmp


def _static_tables(S, nc):
    n_s = S // SLC_BLOCK
    cmp_start = np.arange(nc)[None, :] * CMP_STRIDE
    slc_start = np.arange(n_s)[:, None] * SLC_BLOCK
    ovT = np.clip(np.minimum(cmp_start + CMP_BLOCK, slc_start + SLC_BLOCK)
                  - np.maximum(cmp_start, slc_start), 0, None).astype(np.float32) / CMP_BLOCK
    nkb = S // TK
    tok_blk = (np.arange(nkb)[:, None, None] * TK + np.arange(TK)[None, None, :]) // SLC_BLOCK
    e3 = (np.arange(LANES)[None, :, None] == tok_blk).astype(np.float32)
    return jnp.asarray(ovT, bf16), jnp.asarray(e3, bf16)


def kernel(x, norm_mix, w_in, cmp_pe_k, cmp_w1_k, cmp_w2_k, cmp_pe_v, cmp_w1_v, cmp_w2_v, rel_bias,
           w_branch_a, w_branch_b, w_out, norm_mlp, w_mlp_in, w_mlp_out, norm_final):
    B, S, D = x.shape
    assert norm_mix.shape[0] == 1 and S % TQ == 0 and (S // CMP_STRIDE) % LANES == 0
    G = N_KV_GROUPS_B
    rows = B * S
    nkb = S // TK
    nc = S // CMP_STRIDE
    k_top = min(TOPK_TOKENS, S // 4)
    n_sel = min(N_SLC_BLOCKS, S // SLC_BLOCK)
    x2 = x.reshape(rows, D)

    w = w_in[0]
    o_ka = W_QA
    o_qi = o_ka + 2 * HEAD_DIM
    o_ki = o_qi + W_QI
    o_wi = o_ki + IDX_DIM
    o_qb = o_wi + N_IDX_HEADS
    o_kvb = o_qb + W_QB
    o_gb = o_kvb + W_KVB
    o_gate = o_gb + W_GB
    pad = jnp.zeros((D, LANES - IDX_DIM - N_IDX_HEADS - W_GB), w.dtype)
    w_perm = jnp.concatenate(
        [w[:, :o_qi], w[:, o_qi:o_ki], w[:, o_ki:o_qb], w[:, o_gb:o_gate], pad,
         w[:, o_qb:o_kvb], w[:, o_kvb:o_gb], w[:, o_gate:]], axis=1).astype(bf16)
    qa, kva, qi, kw, qb, kvb, gate = _inproj(x2, norm_mix[0][None], w_perm)

    def keyT(k):
        return k.reshape(B, nkb, TK, k.shape[-1]).transpose(0, 1, 3, 2)
    kiT = keyT(kw[:, :IDX_DIM].astype(bf16).reshape(B, S, IDX_DIM))
    kaT = keyT(kva[:, :HEAD_DIM].reshape(B, S, HEAD_DIM))
    va = kva[:, HEAD_DIM:].reshape(B, nkb, TK, HEAD_DIM)
    kv6 = kvb.reshape(B, S, 6, G, HEAD_DIM)
    def keyT_g(k):
        return k.reshape(B, nkb, TK, G, HEAD_DIM).transpose(0, 3, 1, 4, 2)
    def val_g(v):
        return v.reshape(B, nkb, TK, G, HEAD_DIM).transpose(0, 3, 1, 2, 4)
    ksT, vs = keyT_g(kv6[:, :, 2]), val_g(kv6[:, :, 3])
    kwT, vw = keyT_g(kv6[:, :, 4]), val_g(kv6[:, :, 5])

    bnear_a, bnear_b, bcmp = _bias_tables(rel_bias, nc)
    ovT, e3 = _static_tables(S, nc)

    o_a = _dsa(qi, kw, kiT, qa, kaT, va, bnear_a, B, S, k_top)

    chunk_w = CMP_STRIDE * HEAD_DIM
    xc = kv6[:, :, 0:2].transpose(0, 2, 3, 1, 4).reshape(B, 2, G, nc, chunk_w)
    pe = jnp.stack([cmp_pe_k[0], cmp_pe_v[0]]).reshape(2, 1, 2 * chunk_w)
    pe8 = jnp.broadcast_to(pe, (2, 8, 2 * chunk_w)).astype(bf16)
    w1 = jnp.stack([cmp_w1_k[0], cmp_w1_v[0]]).astype(bf16)
    w1ab = jnp.concatenate([w1[:, :chunk_w], w1[:, chunk_w:]], axis=2)
    w2 = jnp.stack([cmp_w2_k[0], cmp_w2_v[0]]).astype(bf16)
    cmp = _compress(xc, pe8, w1, w1ab, w2)
    kcT = cmp[:, 0].transpose(0, 1, 3, 2).astype(bf16)
    vcc = cmp[:, 1].astype(bf16)

    o_b = _nsa(qb, kw, kcT, vcc, ksT, vs, kwT, vw, bnear_b, bcmp, ovT, e3, B, S, n_sel)

    x1 = _mix(o_a, o_b, gate, x2, w_branch_a[0].astype(bf16), w_branch_b[0].astype(bf16),
              w_out[0].astype(bf16))
    out = _mlp(x1, norm_mlp[0][None], w_mlp_in[0].astype(bf16), w_mlp_out[0].astype(bf16),
               norm_final[None])
    return out.reshape(B, S, D)
```

```python
import functools
import math

import numpy as np
import jax
import jax.numpy as jnp
from jax import lax
from jax.experimental import pallas as pl
from jax.experimental.pallas import tpu as pltpu

f32 = jnp.float32
bf16 = jnp.bfloat16

HEAD_DIM = 64
N_HEADS_A = 8
N_IDX_HEADS = 8
IDX_DIM = 32
TOPK_TOKENS = 256
N_HEADS_B = 8
N_KV_GROUPS_B = 2
HEADS_PER_GROUP = N_HEADS_B // N_KV_GROUPS_B
CMP_BLOCK = 32
CMP_STRIDE = 16
CMP_HIDDEN = 128
SLC_BLOCK = 64
N_SLC_BLOCKS = 16
WINDOW = 512
FORCED_SCORE = 1.0e4
N_BUCKETS = 32
MAX_DISTANCE = 128
EPS = 1e-6

W_QA = N_HEADS_A * HEAD_DIM
W_QI = N_IDX_HEADS * IDX_DIM
W_QB = N_HEADS_B * HEAD_DIM
W_KVB = 6 * N_KV_GROUPS_B * HEAD_DIM
W_GB = 3 * N_HEADS_B

LANES = 128
TQ = 128
TK = 128
LOG2E = 1.4426950408889634
NEG_MASK = -2.0e30
M_INIT = -1.0e30
NEG_SCORE = -3.0e38
BIG_SCORE = 3.0e38
MAX_SEARCH_ITERS = 400
TIE_CHECK_START = 26
VMEM_LIMIT = 56 * 1024 * 1024


def _cparams(*sem):
    return pltpu.CompilerParams(dimension_semantics=sem, vmem_limit_bytes=VMEM_LIMIT)


def _rms(x, g):
    return x * lax.rsqrt(jnp.mean(x * x, axis=-1, keepdims=True) + EPS) * g


def _inproj_kernel(x_ref, g_ref, w_ref, qa_ref, kva_ref, qi_ref, kw_ref, qb_ref, kvb_ref, gate_ref):
    h = _rms(x_ref[...], g_ref[...]).astype(bf16)
    q_scale = HEAD_DIM ** -0.5 * LOG2E
    off = 0
    for ref, scale in ((qa_ref, q_scale), (kva_ref, None), (qi_ref, None), (kw_ref, None),
                       (qb_ref, q_scale), (kvb_ref, None), (gate_ref, None)):
        width = ref.shape[-1]
        for c0 in range(0, width, 512):
            c1 = min(c0 + 512, width)
            y = jnp.dot(h, w_ref[:, off + c0:off + c1], preferred_element_type=f32)
            if scale is not None:
                y = y * scale
            ref[:, c0:c1] = y.astype(ref.dtype)
        off += width


def _inproj(x2, g, w_perm, tm=256):
    rows, d = x2.shape
    widths = (W_QA, 2 * HEAD_DIM, W_QI, LANES, W_QB, W_KVB, 2 * d)
    dtypes = (bf16, bf16, f32, f32, bf16, bf16, f32)
    return pl.pallas_call(
        _inproj_kernel,
        grid=(rows // tm,),
        in_specs=[pl.BlockSpec((tm, d), lambda i: (i, 0)),
                  pl.BlockSpec((1, d), lambda i: (0, 0)),
                  pl.BlockSpec(w_perm.shape, lambda i: (0, 0))],
        out_specs=[pl.BlockSpec((tm, w), lambda i: (i, 0)) for w in widths],
        out_shape=[jax.ShapeDtypeStruct((rows, w), dt) for w, dt in zip(widths, dtypes)],
        compiler_params=_cparams("parallel"),
    )(x2, g, w_perm)


def _count_ge(sc_ref, nkb, p):
    def body(kb, acc):
        return acc + jnp.where(sc_ref[kb] >= p, 1.0, 0.0)
    acc = lax.fori_loop(0, nkb, body, jnp.zeros((TQ, LANES), f32))
    return jnp.sum(acc, axis=1, keepdims=True)


def _select_topk(sc_ref, nkb, rmin, rmax, nvalid, k):
    kf = float(k)
    active = nvalid > kf
    lo0 = jnp.where(active, rmin, NEG_SCORE)
    cl0 = jnp.where(active, nvalid, kf)
    hi0 = jnp.full_like(lo0, BIG_SCORE)
    zero = jnp.zeros_like(lo0)

    def pending(cl, tie):
        return (cl != kf) & (tie == 0.0)

    def cond(st):
        it, lo, hi, cl, ch, tie = st
        return (jnp.max(jnp.where(pending(cl, tie), 1.0, 0.0)) > 0.0) & (it < MAX_SEARCH_ITERS)

    def body(st):
        it, lo, hi, cl, ch, tie = st
        pend = pending(cl, tie)
        hie = jnp.minimum(hi, rmax)
        p = lo + (hie - lo) * 0.5
        p = jnp.where(p > lo, p, hie)
        cnt = _count_ge(sc_ref, nkb, p)
        ge = cnt >= kf
        up_lo = pend & ge
        up_hi = pend & jnp.logical_not(ge)
        lo = jnp.where(up_lo, p, lo)
        cl = jnp.where(up_lo, cnt, cl)
        hi = jnp.where(up_hi, p, hi)
        ch = jnp.where(up_hi, cnt, ch)

        def tie_check(args):
            lo, tie = args
            def scan(kb, c):
                dmin, dmax = c
                s = sc_ref[kb]
                dmin = jnp.minimum(dmin, jnp.where(s >= lo, s, BIG_SCORE))
                dmax = jnp.maximum(dmax, jnp.where(s < hi, s, NEG_SCORE))
                return dmin, dmax
            dmin, dmax = lax.fori_loop(
                0, nkb, scan,
                (jnp.full((TQ, LANES), BIG_SCORE, f32), jnp.full((TQ, LANES), NEG_SCORE, f32)))
            dmin = jnp.min(dmin, axis=1, keepdims=True)
            dmax = jnp.max(dmax, axis=1, keepdims=True)
            new_tie = pending(cl, tie) & (dmin == dmax)
            return jnp.where(new_tie, dmin, lo), jnp.where(new_tie, 1.0, tie)

        lo, tie = lax.cond(it >= TIE_CHECK_START, tie_check, lambda a: a, (lo, tie))
        return it + 1, lo, hi, cl, ch, tie

    _, lo, hi, cl, ch, tie = lax.while_loop(cond, body, (jnp.int32(0), lo0, hi0, cl0, zero, zero))
    any_tie = jnp.max(tie) > 0.0
    need = kf - ch

    @pl.when(any_tie)
    def _():
        r = lax.broadcasted_iota(jnp.int32, (LANES, LANES), 0)
        c = lax.broadcasted_iota(jnp.int32, (LANES, LANES), 1)
        triu = jnp.where(r <= c, 1.0, 0.0).astype(bf16)

        def body(kb, run):
            s = sc_ref[kb]
            eq = s == lo
            eqf = jnp.where(eq, 1.0, 0.0)
            cum = run + jnp.dot(eqf.astype(bf16), triu, preferred_element_type=f32)
            sel_tie = jnp.where((s > lo) | (eq & (cum <= need)), 1.0, -1.0)
            sel_all = jnp.where(s >= lo, 1.0, -1.0)
            sc_ref[kb] = jnp.where(tie > 0.0, sel_tie, sel_all)
            return run + jnp.sum(eqf, axis=1, keepdims=True)

        lax.fori_loop(0, nkb, body, zero)

    return jnp.where(any_tie, 0.0, lo)


def _rank_select(score, k):
    n, width = score.shape
    sub = 8
    tiles = [score[v * sub:(v + 1) * sub] for v in range(n // sub)]
    ridx = lax.broadcasted_iota(jnp.int32, (sub, width), 0)
    ranks = [jnp.zeros((sub, width), f32) for _ in tiles]
    for i in range(n):
        si = jnp.broadcast_to(tiles[i // sub][i % sub:i % sub + 1], (sub, width))
        for v, x in enumerate(tiles):
            ge = jnp.where(si >= x, 1.0, 0.0)
            gt = jnp.where(si > x, 1.0, 0.0)
            if v * sub > i:
                inc = ge
            elif (v + 1) * sub <= i:
                inc = gt
            else:
                inc = jnp.where(ridx > i % sub, ge, gt)
            ranks[v] = ranks[v] + inc
    flags = [jnp.where(r < float(k), 1.0, 0.0) for r in ranks]
    flags.append(jnp.zeros((LANES - n, width), f32))
    return jnp.concatenate(flags, axis=0)


def _attn_init(m_ref, l_ref, acc_ref):
    m_ref[...] = jnp.full(m_ref.shape, M_INIT, f32)
    l_ref[...] = jnp.zeros(l_ref.shape, f32)
    acc_ref[...] = jnp.zeros(acc_ref.shape, f32)


def _attn_step(qs, kT, v, mask, bias, m_ref, l_ref, acc_ref):
    for j, q in enumerate(qs):
        s = jnp.dot(q, kT, preferred_element_type=f32)
        if bias is not None:
            s = s + bias[j]
        if mask is not None:
            s = jnp.where(mask, s, NEG_MASK)
        m_old = m_ref[j]
        m_new = jnp.maximum(m_old, jnp.max(s, axis=1, keepdims=True))
        alpha = jnp.exp2(m_old - m_new)
        p = jnp.exp2(s - m_new)
        l_ref[j] = alpha * l_ref[j] + jnp.sum(p, axis=1, keepdims=True)
        acc_ref[j] = alpha * acc_ref[j] + jnp.dot(p.astype(bf16), v, preferred_element_type=f32)
        m_ref[j] = m_new


def _attn_out(j, l_ref, acc_ref):
    return acc_ref[j] / jnp.maximum(l_ref[j], 1e-30)


def _dsa_kernel(qi_ref, kw_ref, kiT_ref, qa_ref, kaT_ref, va_ref, bnear_ref, o_ref,
                sc_ref, m_ref, l_ref, acc_ref, *, k_top):
    i = pl.program_id(1)
    w_scale = (N_IDX_HEADS ** -0.5) * (IDX_DIM ** -0.5)
    w = kw_ref[:, IDX_DIM:IDX_DIM + N_IDX_HEADS] * w_scale
    qi = qi_ref[...]
    qih = [qi[:, h * IDX_DIM:(h + 1) * IDX_DIM].astype(bf16) for h in range(N_IDX_HEADS)]
    wb = [w[:, h:h + 1] for h in range(N_IDX_HEADS)]
    row = lax.broadcasted_iota(jnp.int32, (TQ, TK), 0)
    col = lax.broadcasted_iota(jnp.int32, (TQ, TK), 1)
    causal = col <= row

    def score_blk(kb):
        kblk = kiT_ref[kb]
        acc = None
        for h in range(N_IDX_HEADS):
            sh = jnp.dot(qih[h], kblk, preferred_element_type=f32)
            term = wb[h] * jnp.maximum(sh, 0.0)
            acc = term if acc is None else acc + term
        return acc

    def far_score(kb, carry):
        mn, mx = carry
        a = score_blk(kb)
        sc_ref[kb] = a
        return jnp.minimum(mn, a), jnp.maximum(mx, a)

    mn, mx = lax.fori_loop(0, i, far_score,
                           (jnp.full((TQ, TK), BIG_SCORE, f32), jnp.full((TQ, TK), NEG_SCORE, f32)))
    a = score_blk(i)
    sc_ref[i] = jnp.where(causal, a, NEG_SCORE)
    mn = jnp.minimum(mn, jnp.where(causal, a, BIG_SCORE))
    mx = jnp.maximum(mx, jnp.where(causal, a, NEG_SCORE))
    rmin = jnp.min(mn, axis=1, keepdims=True)
    rmax = jnp.max(mx, axis=1, keepdims=True)
    nvalid = (i * TQ + row[:, :1] + 1).astype(f32)
    lo = _select_topk(sc_ref, i + 1, rmin, rmax, nvalid, k_top)

    _attn_init(m_ref, l_ref, acc_ref)
    qs = [qa_ref[:, h * HEAD_DIM:(h + 1) * HEAD_DIM] for h in range(N_HEADS_A)]

    def far_attn(kb, _):
        _attn_step(qs, kaT_ref[kb], va_ref[kb], sc_ref[kb] >= lo, None, m_ref, l_ref, acc_ref)
        return 0

    lax.fori_loop(0, jnp.maximum(i - 1, 0), far_attn, 0)

    @pl.when(i >= 1)
    def _():
        bias = [bnear_ref[h, :, 0:TK] for h in range(N_HEADS_A)]
        _attn_step(qs, kaT_ref[i - 1], va_ref[i - 1], sc_ref[i - 1] >= lo, bias, m_ref, l_ref, acc_ref)

    bias = [bnear_ref[h, :, TK:2 * TK] for h in range(N_HEADS_A)]
    _attn_step(qs, kaT_ref[i], va_ref[i], (sc_ref[i] >= lo) & causal, bias, m_ref, l_ref, acc_ref)
    for h in range(N_HEADS_A):
        o_ref[:, h * HEAD_DIM:(h + 1) * HEAD_DIM] = _attn_out(h, l_ref, acc_ref).astype(o_ref.dtype)


def _dsa(qi, kw, kiT, qa, kaT, va, bnear, B, S, k_top):
    nq = S // TQ
    nkb = S // TK
    row_spec = lambda w: pl.BlockSpec((TQ, w), lambda b, i: (b * nq + i, 0))
    return pl.pallas_call(
        functools.partial(_dsa_kernel, k_top=k_top),
        grid=(B, nq),
        in_specs=[row_spec(W_QI), row_spec(LANES),
                  pl.BlockSpec((None, nkb, IDX_DIM, TK), lambda b, i: (b, 0, 0, 0)),
                  row_spec(W_QA),
                  pl.BlockSpec((None, nkb, HEAD_DIM, TK), lambda b, i: (b, 0, 0, 0)),
                  pl.BlockSpec((None, nkb, TK, HEAD_DIM), lambda b, i: (b, 0, 0, 0)),
                  pl.BlockSpec((N_HEADS_A, TQ, 2 * TK), lambda b, i: (0, 0, 0))],
        out_specs=row_spec(W_QA),
        out_shape=jax.ShapeDtypeStruct((B * S, W_QA), bf16),
        scratch_shapes=[pltpu.VMEM((nkb, TQ, TK), f32),
                        pltpu.VMEM((N_HEADS_A, TQ, 1), f32),
                        pltpu.VMEM((N_HEADS_A, TQ, 1), f32),
                        pltpu.VMEM((N_HEADS_A, TQ, HEAD_DIM), f32)],
        compiler_params=_cparams("parallel", "parallel"),
    )(qi, kw, kiT, qa, kaT, va, bnear)


def _compress_kernel(x_ref, pe_ref, w1_ref, w1ab_ref, w2_ref, o_ref):
    nc = x_ref.shape[0]
    ab = jnp.dot(x_ref[...], w1ab_ref[...], preferred_element_type=f32)
    const = jnp.dot(pe_ref[...], w1_ref[...], preferred_element_type=f32)[0:1]
    first = ab[:, :CMP_HIDDEN]
    second = pltpu.roll(ab[:, CMP_HIDDEN:], nc - 1, 0)
    hid = jax.nn.gelu(first + second + const)
    o_ref[...] = jnp.dot(hid.astype(bf16), w2_ref[...], preferred_element_type=f32)


def _compress(xc, pe8, w1, w1ab, w2):
    B, _, G, nc, width = xc.shape
    return pl.pallas_call(
        _compress_kernel,
        grid=(B, 2, G),
        in_specs=[pl.BlockSpec((None, None, None, nc, width), lambda b, k, g: (b, k, g, 0, 0)),
                  pl.BlockSpec((None, 8, 2 * width), lambda b, k, g: (k, 0, 0)),
                  pl.BlockSpec((None, 2 * width, CMP_HIDDEN), lambda b, k, g: (k, 0, 0)),
                  pl.BlockSpec((None, width, 2 * CMP_HIDDEN), lambda b, k, g: (k, 0, 0)),
                  pl.BlockSpec((None, CMP_HIDDEN, HEAD_DIM), lambda b, k, g: (k, 0, 0))],
        out_specs=pl.BlockSpec((None, None, None, nc, HEAD_DIM), lambda b, k, g: (b, k, g, 0, 0)),
        out_shape=jax.ShapeDtypeStruct((B, 2, G, nc, HEAD_DIM), f32),
        compiler_params=_cparams("parallel", "parallel", "parallel"),
    )(xc, pe8, w1, w1ab, w2)


def _nsa_kernel(qb_ref, kw_ref, kcT_ref, vcc_ref, ksT_ref, vs_ref, kwT_ref, vw_ref,
                bnear_ref, bcmp_ref, ovT_ref, e3_ref, o_ref,
                m_ref, l_ref, acc_ref, *, n_sel):
    i = pl.program_id(1)
    nc = kcT_ref.shape[-1]
    R = HEADS_PER_GROUP
    row = lax.broadcasted_iota(jnp.int32, (TQ, TK), 0)
    col = lax.broadcasted_iota(jnp.int32, (TQ, TK), 1)
    causal = col <= row
    t_c = i * TQ + lax.broadcasted_iota(jnp.int32, (TQ, nc), 0)
    c_c = lax.broadcasted_iota(jnp.int32, (TQ, nc), 1)
    mask_c = c_c * CMP_STRIDE + (CMP_BLOCK - 1) <= t_c
    cmp_shift = (i * (TQ // CMP_STRIDE) + nc - 9) % nc
    n_s = ovT_ref.shape[0]
    blk = lax.broadcasted_iota(jnp.int32, (n_s, TQ), 0)
    cur = (i * TQ + lax.broadcasted_iota(jnp.int32, (n_s, TQ), 1)) // SLC_BLOCK
    valid = blk <= cur
    forced = (blk == 0) | (blk == cur) | (blk == cur - 1)
    gates = jax.nn.sigmoid(kw_ref[:, IDX_DIM + N_IDX_HEADS:IDX_DIM + N_IDX_HEADS + W_GB])
    win_blocks = WINDOW // TK

    for g in range(N_KV_GROUPS_B):
        heads = [g * R + r for r in range(R)]
        qs = [qb_ref[:, h * HEAD_DIM:(h + 1) * HEAD_DIM] for h in heads]

        psum = jnp.zeros((TQ, nc), f32)
        o_c = []
        for r, h in enumerate(heads):
            bias = pltpu.roll(bcmp_ref[h], cmp_shift, 1)
            s = jnp.dot(qs[r], kcT_ref[g], preferred_element_type=f32) + bias
            s = jnp.where(mask_c, s, NEG_MASK)
            m = jnp.max(s, axis=1, keepdims=True)
            e = jnp.where(mask_c, jnp.exp2(s - m), 0.0)
            p = e / jnp.maximum(jnp.sum(e, axis=1, keepdims=True), 1e-30)
            psum = psum + p
            o_c.append(jnp.dot(p.astype(bf16), vcc_ref[g], preferred_element_type=f32))

        p_hi = psum.astype(bf16)
        p_lo = (psum - p_hi.astype(f32)).astype(bf16)
        nt = (((1,), (1,)), ((), ()))
        imp = (lax.dot_general(ovT_ref[...], p_hi, nt, preferred_element_type=f32)
               + lax.dot_general(ovT_ref[...], p_lo, nt, preferred_element_type=f32))
        score = jnp.where(valid, jnp.where(forced, FORCED_SCORE, imp), NEG_SCORE)
        selm = _rank_select(score, n_sel).T.astype(bf16)

        _attn_init(m_ref, l_ref, acc_ref)

        def tok_mask(kb):
            return jnp.dot(selm, e3_ref[kb], preferred_element_type=f32) > 0.5

        def far_sel(kb, _):
            _attn_step(qs, ksT_ref[g, kb], vs_ref[g, kb], tok_mask(kb), None, m_ref, l_ref, acc_ref)
            return 0

        lax.fori_loop(0, jnp.maximum(i - 1, 0), far_sel, 0)
        bias_prev = [bnear_ref[h, :, 0:TK] for h in heads]
        bias_diag = [bnear_ref[h, :, TK:2 * TK] for h in heads]

        @pl.when(i >= 1)
        def _():
            _attn_step(qs, ksT_ref[g, i - 1], vs_ref[g, i - 1], tok_mask(i - 1), bias_prev,
                       m_ref, l_ref, acc_ref)

        _attn_step(qs, ksT_ref[g, i], vs_ref[g, i], tok_mask(i) & causal, bias_diag, m_ref, l_ref, acc_ref)
        o_s = [_attn_out(r, l_ref, acc_ref) for r in range(R)]

        _attn_init(m_ref, l_ref, acc_ref)
        for d in range(win_blocks, 1, -1):
            @pl.when(i >= d)
            def _(d=d):
                mask = (col > row) if d == win_blocks else None
                _attn_step(qs, kwT_ref[g, i - d], vw_ref[g, i - d], mask, None, m_ref, l_ref, acc_ref)

        @pl.when(i >= 1)
        def _():
            _attn_step(qs, kwT_ref[g, i - 1], vw_ref[g, i - 1], None, bias_prev, m_ref, l_ref, acc_ref)

        _attn_step(qs, kwT_ref[g, i], vw_ref[g, i], causal, bias_diag, m_ref, l_ref, acc_ref)

        for r, h in enumerate(heads):
            o_w = _attn_out(r, l_ref, acc_ref)
            o = (gates[:, h:h + 1] * o_c[r]
                 + gates[:, N_HEADS_B + h:N_HEADS_B + h + 1] * o_s[r]
                 + gates[:, 2 * N_HEADS_B + h:2 * N_HEADS_B + h + 1] * o_w)
            o_ref[:, h * HEAD_DIM:(h + 1) * HEAD_DIM] = o.astype(o_ref.dtype)


def _nsa(qb, kw, kcT, vcc, ksT, vs, kwT, vw, bnear, bcmp, ovT, e3, B, S, n_sel):
    nq = S // TQ
    nkb = S // TK
    nc = kcT.shape[-1]
    G = N_KV_GROUPS_B
    row_spec = lambda w: pl.BlockSpec((TQ, w), lambda b, i: (b * nq + i, 0))
    kT_spec = pl.BlockSpec((None, G, nkb, HEAD_DIM, TK), lambda b, i: (b, 0, 0, 0, 0))
    v_spec = pl.BlockSpec((None, G, nkb, TK, HEAD_DIM), lambda b, i: (b, 0, 0, 0, 0))
    const = lambda shape: pl.BlockSpec(shape, lambda b, i: (0,) * len(shape))
    return pl.pallas_call(
        functools.partial(_nsa_kernel, n_sel=n_sel),
        grid=(B, nq),
        in_specs=[row_spec(W_QB), row_spec(LANES),
                  pl.BlockSpec((None, G, HEAD_DIM, nc), lambda b, i: (b, 0, 0, 0)),
                  pl.BlockSpec((None, G, nc, HEAD_DIM), lambda b, i: (b, 0, 0, 0)),
                  kT_spec, v_spec, kT_spec, v_spec,
                  const(bnear.shape), const(bcmp.shape), const(ovT.shape), const(e3.shape)],
        out_specs=row_spec(W_QB),
        out_shape=jax.ShapeDtypeStruct((B * S, W_QB), bf16),
        scratch_shapes=[pltpu.VMEM((HEADS_PER_GROUP, TQ, 1), f32),
                        pltpu.VMEM((HEADS_PER_GROUP, TQ, 1), f32),
                        pltpu.VMEM((HEADS_PER_GROUP, TQ, HEAD_DIM), f32)],
        compiler_params=_cparams("parallel", "parallel"),
    )(qb, kw, kcT, vcc, ksT, vs, kwT, vw, bnear, bcmp, ovT, e3)


def _mix_kernel(oa_ref, ob_ref, gate_ref, x_ref, wa_ref, wb_ref, wo_ref, x1_ref):
    d = x_ref.shape[-1]
    ga = jax.nn.sigmoid(gate_ref[:, :d])
    gb = jax.nn.sigmoid(gate_ref[:, d:])
    mix = (ga * jnp.dot(oa_ref[...], wa_ref[...], preferred_element_type=f32)
           + gb * jnp.dot(ob_ref[...], wb_ref[...], preferred_element_type=f32))
    x1_ref[...] = x_ref[...] + jnp.dot(mix.astype(bf16), wo_ref[...], preferred_element_type=f32)


def _mix(oa, ob, gate, x2, wa, wb, wo, tm=512):
    rows, d = x2.shape
    rs = lambda w: pl.BlockSpec((tm, w), lambda i: (i, 0))
    cs = lambda a: pl.BlockSpec(a.shape, lambda i: (0, 0))
    return pl.pallas_call(
        _mix_kernel,
        grid=(rows // tm,),
        in_specs=[rs(W_QA), rs(W_QB), rs(2 * d), rs(d), cs(wa), cs(wb), cs(wo)],
        out_specs=rs(d),
        out_shape=jax.ShapeDtypeStruct((rows, d), f32),
        compiler_params=_cparams("parallel"),
    )(oa, ob, gate, x2, wa, wb, wo)


def _mlp_kernel(x1_ref, g_ref, w1_ref, w2_ref, gf_ref, o_ref, *, chunk):
    x1 = x1_ref[...]
    h = _rms(x1, g_ref[...]).astype(bf16)
    acc = jnp.zeros(x1.shape, f32)
    for c0 in range(0, w1_ref.shape[1], chunk):
        u = jnp.dot(h, w1_ref[:, c0:c0 + chunk], preferred_element_type=f32)
        u = jnp.square(jnp.maximum(u, 0.0)).astype(bf16)
        acc = acc + jnp.dot(u, w2_ref[c0:c0 + chunk, :], preferred_element_type=f32)
    o_ref[...] = _rms(x1 + acc, gf_ref[...])


def _mlp(x1, g, w1, w2, gf, tm=256, chunk=512):
    rows, d = x1.shape
    rs = pl.BlockSpec((tm, d), lambda i: (i, 0))
    cs = lambda a: pl.BlockSpec(a.shape, lambda i: (0, 0))
    ws = lambda a: pl.BlockSpec(a.shape, lambda i: (0, 0), pipeline_mode=pl.Buffered(1))
    return pl.pallas_call(
        functools.partial(_mlp_kernel, chunk=chunk),
        grid=(rows // tm,),
        in_specs=[rs, cs(g), ws(w1), ws(w2), cs(gf)],
        out_specs=rs,
        out_shape=jax.ShapeDtypeStruct((rows, d), f32),
        compiler_params=_cparams("parallel"),
    )(x1, g, w1, w2, gf)


def _t5_bucket(dist):
    n = jnp.maximum(dist, 0)
    max_exact = N_BUCKETS // 2
    nf = jnp.maximum(n, 1).astype(f32)
    large = max_exact + (jnp.log(nf / max_exact) / math.log(MAX_DISTANCE / max_exact)
                         * (N_BUCKETS - max_exact)).astype(jnp.int32)
    large = jnp.minimum(large, N_BUCKETS - 1)
    return jnp.where(n < max_exact, n, large)


def _bias_tables(rel_bias, nc):
    tab = rel_bias.astype(f32) * LOG2E
    bucket = _t5_bucket(jnp.arange(MAX_DISTANCE + 1))
    by_dist = jnp.dot((bucket[:, None] == jnp.arange(N_BUCKETS)).astype(f32), tab,
                      precision=lax.Precision.HIGHEST)
    shifted = (by_dist - by_dist[MAX_DISTANCE]).T

    def lookup(dist, heads):
        onehot = (jnp.asarray(dist)[..., None] == jnp.arange(MAX_DISTANCE)).astype(f32)
        return jnp.einsum('rcd,hd->hrc', onehot, heads[:, :MAX_DISTANCE], precision=lax.Precision.HIGHEST)

    r = np.arange(TQ)[:, None]
    near = lookup(r + TK - np.arange(2 * TK)[None, :], shifted)
    j = np.arange(nc)[None, :]
    dist_c = np.where(j < 16, r - (CMP_BLOCK - 1) - CMP_STRIDE * (j - 9), -1)
    cmp = lookup(dist_c, shifted[N_HEADS_A:])
    return near[:N_HEADS_A], near[N_HEADS_A:], cmp


def _static_tables(S, nc):
    n_s = S // SLC_BLOCK
    cmp_start = np.arange(nc)[None, :] * CMP_STRIDE
    slc_start = np.arange(n_s)[:, None] * SLC_BLOCK
    ovT = np.clip(np.minimum(cmp_start + CMP_BLOCK, slc_start + SLC_BLOCK)
                  - np.maximum(cmp_start, slc_start), 0, None).astype(np.float32) / CMP_BLOCK
    nkb = S // TK
    tok_blk = (np.arange(nkb)[:, None, None] * TK + np.arange(TK)[None, None, :]) // SLC_BLOCK
    e3 = (np.arange(LANES)[None, :, None] == tok_blk).astype(np.float32)
    return jnp.asarray(ovT, bf16), jnp.asarray(e3, bf16)


def kernel(x, norm_mix, w_in, cmp_pe_k, cmp_w1_k, cmp_w2_k, cmp_pe_v, cmp_w1_v, cmp_w2_v, rel_bias,
           w_branch_a, w_branch_b, w_out, norm_mlp, w_mlp_in, w_mlp_out, norm_final):
    B, S, D = x.shape
    assert norm_mix.shape[0] == 1 and S % TQ == 0 and (S // CMP_STRIDE) % LANES == 0
    G = N_KV_GROUPS_B
    rows = B * S
    nkb = S // TK
    nc = S // CMP_STRIDE
    k_top = min(TOPK_TOKENS, S // 4)
    n_sel = min(N_SLC_BLOCKS, S // SLC_BLOCK)
    x2 = x.reshape(rows, D)

    w = w_in[0]
    o_ka = W_QA
    o_qi = o_ka + 2 * HEAD_DIM
    o_ki = o_qi + W_QI
    o_wi = o_ki + IDX_DIM
    o_qb = o_wi + N_IDX_HEADS
    o_kvb = o_qb + W_QB
    o_gb = o_kvb + W_KVB
    o_gate = o_gb + W_GB
    pad = jnp.zeros((D, LANES - IDX_DIM - N_IDX_HEADS - W_GB), w.dtype)
    w_perm = jnp.concatenate(
        [w[:, :o_qi], w[:, o_qi:o_ki], w[:, o_ki:o_qb], w[:, o_gb:o_gate], pad,
         w[:, o_qb:o_kvb], w[:, o_kvb:o_gb], w[:, o_gate:]], axis=1).astype(bf16)
    qa, kva, qi, kw, qb, kvb, gate = _inproj(x2, norm_mix[0][None], w_perm)

    def keyT(k):
        return k.reshape(B, nkb, TK, k.shape[-1]).transpose(0, 1, 3, 2)
    kiT = keyT(kw[:, :IDX_DIM].astype(bf16).reshape(B, S, IDX_DIM))
    kaT = keyT(kva[:, :HEAD_DIM].reshape(B, S, HEAD_DIM))
    va = kva[:, HEAD_DIM:].reshape(B, nkb, TK, HEAD_DIM)
    kv6 = kvb.reshape(B, S, 6, G, HEAD_DIM)
    def keyT_g(k):
        return k.reshape(B, nkb, TK, G, HEAD_DIM).transpose(0, 3, 1, 4, 2)
    def val_g(v):
        return v.reshape(B, nkb, TK, G, HEAD_DIM).transpose(0, 3, 1, 2, 4)
    ksT, vs = keyT_g(kv6[:, :, 2]), val_g(kv6[:, :, 3])
    kwT, vw = keyT_g(kv6[:, :, 4]), val_g(kv6[:, :, 5])

    bnear_a, bnear_b, bcmp = _bias_tables(rel_bias, nc)
    ovT, e3 = _static_tables(S, nc)

    o_a = _dsa(qi, kw, kiT, qa, kaT, va, bnear_a, B, S, k_top)

    chunk_w = CMP_STRIDE * HEAD_DIM
    xc = kv6[:, :, 0:2].transpose(0, 2, 3, 1, 4).reshape(B, 2, G, nc, chunk_w)
    pe = jnp.stack([cmp_pe_k[0], cmp_pe_v[0]]).reshape(2, 1, 2 * chunk_w)
    pe8 = jnp.broadcast_to(pe, (2, 8, 2 * chunk_w)).astype(bf16)
    w1 = jnp.stack([cmp_w1_k[0], cmp_w1_v[0]]).astype(bf16)
    w1ab = jnp.concatenate([w1[:, :chunk_w], w1[:, chunk_w:]], axis=2)
    w2 = jnp.stack([cmp_w2_k[0], cmp_w2_v[0]]).astype(bf16)
    cmp = _compress(xc, pe8, w1, w1ab, w2)
    kcT = cmp[:, 0].transpose(0, 1, 3, 2).astype(bf16)
    vcc = cmp[:, 1].astype(bf16)

    o_b = _nsa(qb, kw, kcT, vcc, ksT, vs, kwT, vw, bnear_b, bcmp, ovT, e3, B, S, n_sel)

    x1 = _mix(o_a, o_b, gate, x2, w_branch_a[0].astype(bf16), w_branch_b[0].astype(bf16),
              w_out[0].astype(bf16))
    out = _mlp(x1, norm_mlp[0][None], w_mlp_in[0].astype(bf16), w_mlp_out[0].astype(bf16),
               norm_final[None])
    return out.reshape(B, S, D)
```

```python
import functools
import math

import numpy as np
import jax
import jax.numpy as jnp
from jax import lax
from jax.experimental import pallas as pl
from jax.experimental.pallas import tpu as pltpu

f32 = jnp.float32
bf16 = jnp.bfloat16

HEAD_DIM = 64
N_HEADS_A = 8
N_IDX_HEADS = 8
IDX_DIM = 32
TOPK_TOKENS = 256
N_HEADS_B = 8
N_KV_GROUPS_B = 2
HEADS_PER_GROUP = N_HEADS_B // N_KV_GROUPS_B
CMP_BLOCK = 32
CMP_STRIDE = 16
CMP_HIDDEN = 128
SLC_BLOCK = 64
N_SLC_BLOCKS = 16
WINDOW = 512
FORCED_SCORE = 1.0e4
N_BUCKETS = 32
MAX_DISTANCE = 128
EPS = 1e-6

W_QA = N_HEADS_A * HEAD_DIM
W_QI = N_IDX_HEADS * IDX_DIM
W_QB = N_HEADS_B * HEAD_DIM
W_KVB = 6 * N_KV_GROUPS_B * HEAD_DIM
W_GB = 3 * N_HEADS_B
W_MISC = N_IDX_HEADS + W_GB
W_VT = 5 * HEAD_DIM

LANES = 128
SUBLANES = 8
TQ = 256
TK = 256
QK_LOOKAHEAD = 2
CMP_WIN = 32
LOG2E = 1.4426950408889634
NEG_MASK = -2.0e30
M_INIT = -1.0e30
NEG_SCORE = -3.0e38
BIG_SCORE = 3.0e38
MAX_SEARCH_ITERS = 400
TIE_CHECK_START = 22
COUNT_ROWS = 32
VMEM_LIMIT = 56 * 1024 * 1024

NT_DIMS = (((1,), (1,)), ((), ()))


def _cparams(*sem):
    return pltpu.CompilerParams(dimension_semantics=sem, vmem_limit_bytes=VMEM_LIMIT)


def _rms(x, g):
    return x * lax.rsqrt(jnp.mean(x * x, axis=-1, keepdims=True) + EPS) * g


def _inproj_kernel(x_ref, g_ref, w_ref, wT_ref, keys_ref, kvb_ref, gate_ref,
                   qaT_ref, qiT_ref, miscT_ref, qbT_ref, vT_ref):
    h = _rms(x_ref[...], g_ref[...]).astype(bf16)
    q_scale = HEAD_DIM ** -0.5 * LOG2E
    off = 0
    for ref in (keys_ref, kvb_ref, gate_ref):
        width = ref.shape[-1]
        for c0 in range(0, width, 512):
            c1 = min(c0 + 512, width)
            y = jnp.dot(h, w_ref[:, off + c0:off + c1], preferred_element_type=f32)
            ref[:, c0:c1] = y.astype(ref.dtype)
        off += width
    off = 0
    for ref, scale in ((qaT_ref, q_scale), (qiT_ref, None), (miscT_ref, None), (qbT_ref, q_scale),
                       (vT_ref, None)):
        n = ref.shape[0]
        y = lax.dot_general(wT_ref[off:off + n, :], h, NT_DIMS, preferred_element_type=f32)
        if scale is not None:
            y = y * scale
        ref[...] = y.astype(ref.dtype)
        off += n


def _inproj(x2, g, w_rm, w_t, tm=256):
    rows, d = x2.shape
    rm = ((LANES, bf16), (4 * LANES, bf16), (2 * d, f32))
    tr = ((W_QA, bf16), (W_QI, bf16), (W_MISC, f32), (W_QB, bf16), (W_VT, bf16))
    return pl.pallas_call(
        _inproj_kernel,
        grid=(rows // tm,),
        in_specs=[pl.BlockSpec((tm, d), lambda i: (i, 0)),
                  pl.BlockSpec((1, d), lambda i: (0, 0)),
                  pl.BlockSpec(w_rm.shape, lambda i: (0, 0)),
                  pl.BlockSpec(w_t.shape, lambda i: (0, 0))],
        out_specs=([pl.BlockSpec((tm, w), lambda i: (i, 0)) for w, _ in rm]
                   + [pl.BlockSpec((n, tm), lambda i: (0, i)) for n, _ in tr]),
        out_shape=([jax.ShapeDtypeStruct((rows, w), dt) for w, dt in rm]
                   + [jax.ShapeDtypeStruct((n, rows), dt) for n, dt in tr]),
        compiler_params=_cparams("parallel"),
    )(x2, g, w_rm, w_t)


def _count_ge(sc_ref, nkb, p):
    def body(kb, acc):
        hit = jnp.where(sc_ref[kb] >= p, 1.0, 0.0)
        return acc + jnp.sum(hit.reshape(TK // COUNT_ROWS, COUNT_ROWS, TQ), axis=0)
    acc = lax.fori_loop(0, nkb, body, jnp.zeros((COUNT_ROWS, TQ), f32))
    return jnp.sum(acc, axis=0, keepdims=True)


def _select_topk(sc_ref, nkb, rmin, rmax, nvalid, k):
    kf = float(k)
    active = nvalid > kf
    lo0 = jnp.where(active, rmin, NEG_SCORE)
    cl0 = jnp.where(active, nvalid, kf)
    hi0 = jnp.full_like(lo0, BIG_SCORE)
    zero = jnp.zeros_like(lo0)

    def pending(cl, tie):
        return (cl != kf) & (tie == 0.0)

    def cond(st):
        it, lo, hi, cl, ch, tie = st
        return (jnp.max(jnp.where(pending(cl, tie), 1.0, 0.0)) > 0.0) & (it < MAX_SEARCH_ITERS)

    def body(st):
        it, lo, hi, cl, ch, tie = st
        pend = pending(cl, tie)
        hie = jnp.minimum(hi, rmax)
        p = lo + (hie - lo) * 0.5
        p = jnp.where(p > lo, p, hie)
        cnt = _count_ge(sc_ref, nkb, p)
        ge = cnt >= kf
        up_lo = pend & ge
        up_hi = pend & jnp.logical_not(ge)
        lo = jnp.where(up_lo, p, lo)
        cl = jnp.where(up_lo, cnt, cl)
        hi = jnp.where(up_hi, p, hi)
        ch = jnp.where(up_hi, cnt, ch)

        def tie_check(args):
            lo, tie = args
            def scan(kb, c):
                dmin, dmax = c
                s = sc_ref[kb]
                dmin = jnp.minimum(dmin, jnp.min(jnp.where(s >= lo, s, BIG_SCORE), axis=0, keepdims=True))
                dmax = jnp.maximum(dmax, jnp.max(jnp.where(s < hi, s, NEG_SCORE), axis=0, keepdims=True))
                return dmin, dmax
            dmin, dmax = lax.fori_loop(
                0, nkb, scan, (jnp.full((1, TQ), BIG_SCORE, f32), jnp.full((1, TQ), NEG_SCORE, f32)))
            new_tie = pending(cl, tie) & (dmin == dmax)
            return jnp.where(new_tie, dmin, lo), jnp.where(new_tie, 1.0, tie)

        lo, tie = lax.cond(it >= TIE_CHECK_START, tie_check, lambda a: a, (lo, tie))
        return it + 1, lo, hi, cl, ch, tie

    _, lo, hi, cl, ch, tie = lax.while_loop(cond, body, (jnp.int32(0), lo0, hi0, cl0, zero, zero))
    any_tie = jnp.max(tie) > 0.0
    need = kf - ch

    @pl.when(any_tie)
    def _():
        r = lax.broadcasted_iota(jnp.int32, (TK, TK), 0)
        c = lax.broadcasted_iota(jnp.int32, (TK, TK), 1)
        tril = jnp.where(c <= r, 1.0, 0.0).astype(bf16)

        def body(kb, run):
            s = sc_ref[kb]
            eq = s == lo
            eqf = jnp.where(eq, 1.0, 0.0)
            cum = run + jnp.dot(tril, eqf.astype(bf16), preferred_element_type=f32)
            sel_tie = jnp.where((s > lo) | (eq & (cum <= need)), 1.0, -1.0)
            sel_all = jnp.where(s >= lo, 1.0, -1.0)
            sc_ref[kb] = jnp.where(tie > 0.0, sel_tie, sel_all)
            return run + jnp.sum(eqf, axis=0, keepdims=True)

        lax.fori_loop(0, nkb, body, zero)

    return jnp.where(any_tie, 0.0, lo)


def _rank_select(score, k):
    n, width = score.shape
    tiles = [score[v * SUBLANES:(v + 1) * SUBLANES] for v in range(n // SUBLANES)]
    ridx = lax.broadcasted_iota(jnp.int32, (SUBLANES, width), 0)
    ranks = [jnp.zeros((SUBLANES, width), f32) for _ in tiles]
    for i in range(n):
        si = jnp.broadcast_to(tiles[i // SUBLANES][i % SUBLANES:i % SUBLANES + 1], (SUBLANES, width))
        for v, x in enumerate(tiles):
            if v * SUBLANES > i:
                inc = jnp.where(si >= x, 1.0, 0.0)
            elif (v + 1) * SUBLANES <= i:
                inc = jnp.where(si > x, 1.0, 0.0)
            else:
                inc = jnp.where(ridx > i % SUBLANES, jnp.where(si >= x, 1.0, 0.0), jnp.where(si > x, 1.0, 0.0))
            ranks[v] = ranks[v] + inc
    flags = [jnp.where(r < float(k), 1.0, 0.0) for r in ranks]
    flags.append(jnp.zeros((LANES - n, width), f32))
    return jnp.concatenate(flags, axis=0)


def _attn_init(m_ref, l_ref, acc_ref):
    m_ref[...] = jnp.full(m_ref.shape, M_INIT, f32)
    l_ref[...] = jnp.zeros(l_ref.shape, f32)
    acc_ref[...] = jnp.zeros(acc_ref.shape, f32)


def _attn_step(qpads, keys, vT, madd, bias, m_ref, l_ref, acc_ref):
    def logits(j):
        s = jnp.dot(keys, qpads[j], preferred_element_type=f32)
        if bias is not None:
            s = s + bias[j]
        if madd is not None:
            s = s + madd
        return s

    def values(j, alpha, p):
        acc_ref[j] = alpha * acc_ref[j] + jnp.dot(vT, p, preferred_element_type=f32)

    n = len(qpads)
    pending = {j: logits(j) for j in range(min(QK_LOOKAHEAD, n))}
    deferred = None
    for j in range(n):
        if j + QK_LOOKAHEAD < n:
            pending[j + QK_LOOKAHEAD] = logits(j + QK_LOOKAHEAD)
        s = pending.pop(j)
        m_old = m_ref[j]
        m_new = jnp.maximum(m_old, jnp.max(s, axis=0, keepdims=True))
        alpha = jnp.exp2(m_old - m_new)
        p = jnp.exp2(s - m_new)
        l_ref[j] = alpha * l_ref[j] + jnp.sum(p, axis=0, keepdims=True)
        m_ref[j] = m_new
        if deferred is not None:
            values(*deferred)
        deferred = (j, alpha, p.astype(bf16))
    values(*deferred)


def _attn_out(j, l_ref, acc_ref):
    return acc_ref[j] * (1.0 / jnp.maximum(l_ref[j], 1e-30))


def _mask_add(mask):
    return jnp.where(mask, 0.0, NEG_MASK)


def _pad_rows(x, start):
    parts = []
    if start:
        parts.append(jnp.zeros((start, x.shape[1]), x.dtype))
    parts.append(x)
    if LANES - start - x.shape[0]:
        parts.append(jnp.zeros((LANES - start - x.shape[0], x.shape[1]), x.dtype))
    return jnp.concatenate(parts, axis=0)


def _dsa_kernel(qiT_ref, miscT_ref, keys_ref, qaT_ref, vT_ref, bias_ref, o_ref,
                sc_ref, qi_ref, qa_ref, oT_ref, m_ref, l_ref, acc_ref, *, k_top):
    j = pl.program_id(1)
    w_scale = (N_IDX_HEADS ** -0.5) * (IDX_DIM ** -0.5)
    w = miscT_ref[0:N_IDX_HEADS, :] * w_scale
    for h in range(N_IDX_HEADS):
        qi_ref[h] = _pad_rows(qiT_ref[h * IDX_DIM:(h + 1) * IDX_DIM, :], HEAD_DIM)
    for h in range(N_HEADS_A):
        qa_ref[h] = _pad_rows(qaT_ref[h * HEAD_DIM:(h + 1) * HEAD_DIM, :], 0)
    krow = lax.broadcasted_iota(jnp.int32, (TK, TQ), 0)
    qcol = lax.broadcasted_iota(jnp.int32, (TK, TQ), 1)
    causal = krow <= qcol

    def keys_blk(kb):
        return keys_ref[pl.ds(pl.multiple_of(kb * TK, TK), TK), :]

    def score_blk(kb):
        keys = keys_blk(kb)
        acc = None
        for h in range(N_IDX_HEADS):
            sh = jnp.dot(keys, qi_ref[h], preferred_element_type=f32)
            term = w[h:h + 1, :] * jnp.maximum(sh, 0.0)
            acc = term if acc is None else acc + term
        return acc

    def far_score(kb, carry):
        mn, mx = carry
        a = score_blk(kb)
        sc_ref[kb] = a
        return (jnp.minimum(mn, jnp.min(a, axis=0, keepdims=True)),
                jnp.maximum(mx, jnp.max(a, axis=0, keepdims=True)))

    mn, mx = lax.fori_loop(0, j, far_score,
                           (jnp.full((1, TQ), BIG_SCORE, f32), jnp.full((1, TQ), NEG_SCORE, f32)))
    a = score_blk(j)
    sc_ref[j] = jnp.where(causal, a, NEG_SCORE)
    rmin = jnp.minimum(mn, jnp.min(jnp.where(causal, a, BIG_SCORE), axis=0, keepdims=True))
    rmax = jnp.maximum(mx, jnp.max(jnp.where(causal, a, NEG_SCORE), axis=0, keepdims=True))
    nvalid = (j * TQ + qcol[0:1, :] + 1).astype(f32)
    lo = _select_topk(sc_ref, j + 1, rmin, rmax, nvalid, k_top)

    _attn_init(m_ref, l_ref, acc_ref)
    qpads = [qa_ref[h] for h in range(N_HEADS_A)]

    def far_attn(kb, _):
        _attn_step(qpads, keys_blk(kb), vT_ref[kb], _mask_add(sc_ref[kb] >= lo), None, m_ref, l_ref, acc_ref)
        return 0

    lax.fori_loop(0, jnp.maximum(j - 1, 0), far_attn, 0)

    @pl.when(j >= 1)
    def _():
        bias = [bias_ref[h, 0:TK, :] for h in range(N_HEADS_A)]
        _attn_step(qpads, keys_blk(j - 1), vT_ref[j - 1], _mask_add(sc_ref[j - 1] >= lo), bias,
                   m_ref, l_ref, acc_ref)

    bias = [bias_ref[h, TK:2 * TK, :] for h in range(N_HEADS_A)]
    _attn_step(qpads, keys_blk(j), vT_ref[j], _mask_add((sc_ref[j] >= lo) & causal), bias, m_ref, l_ref, acc_ref)
    for h in range(N_HEADS_A):
        oT_ref[h * HEAD_DIM:(h + 1) * HEAD_DIM, :] = _attn_out(h, l_ref, acc_ref)
    o_ref[...] = oT_ref[...].T.astype(o_ref.dtype)


def _attn_scratch(n_heads):
    return [pltpu.VMEM((n_heads, 1, TQ), f32), pltpu.VMEM((n_heads, 1, TQ), f32),
            pltpu.VMEM((n_heads, HEAD_DIM, TQ), f32)]


def _dsa(qiT, miscT, keys, qaT, vT_blocks, biasT, B, S, k_top):
    nq = S // TQ
    nkb = S // TK
    col_spec = lambda n: pl.BlockSpec((n, TQ), lambda b, j: (0, b * nq + j))
    return pl.pallas_call(
        functools.partial(_dsa_kernel, k_top=k_top),
        grid=(B, nq),
        in_specs=[col_spec(W_QI), col_spec(W_MISC),
                  pl.BlockSpec((S, LANES), lambda b, j: (b, 0)),
                  col_spec(W_QA),
                  pl.BlockSpec((None, nkb, HEAD_DIM, TK), lambda b, j: (b, 0, 4, 0)),
                  pl.BlockSpec(biasT.shape, lambda b, j: (0, 0, 0))],
        out_specs=pl.BlockSpec((TQ, W_QA), lambda b, j: (b * nq + j, 0)),
        out_shape=jax.ShapeDtypeStruct((B * S, W_QA), bf16),
        scratch_shapes=[pltpu.VMEM((nkb, TK, TQ), f32),
                        pltpu.VMEM((N_IDX_HEADS, LANES, TQ), bf16),
                        pltpu.VMEM((N_HEADS_A, LANES, TQ), bf16),
                        pltpu.VMEM((W_QA, TQ), f32)] + _attn_scratch(N_HEADS_A),
        compiler_params=_cparams("parallel", "parallel"),
    )(qiT, miscT, keys, qaT, vT_blocks, biasT)


def _compress_kernel(x_ref, pe_ref, w1_ref, w1ab_ref, w2bd_ref, w2bdT_ref, ck_ref, cvT_ref):
    nc = x_ref.shape[2]

    def hidden(kind):
        const = jnp.dot(pe_ref[kind], w1_ref[kind], preferred_element_type=f32)[0:1]
        hs = []
        for g in range(N_KV_GROUPS_B):
            ab = jnp.dot(x_ref[kind, g], w1ab_ref[kind], preferred_element_type=f32)
            first = ab[:, :CMP_HIDDEN]
            second = pltpu.roll(ab[:, CMP_HIDDEN:], nc - 1, 0)
            hs.append(jax.nn.gelu(first + second + const).astype(bf16))
        return jnp.concatenate(hs, axis=1)

    ck_ref[...] = jnp.dot(hidden(0), w2bd_ref[...], preferred_element_type=f32).astype(ck_ref.dtype)
    cvT_ref[...] = lax.dot_general(w2bdT_ref[...], hidden(1), NT_DIMS,
                                   preferred_element_type=f32).astype(cvT_ref.dtype)


def _compress(xc, pe8, w1, w1ab, w2bd_k, w2bdT_v):
    B, _, G, nc, width = xc.shape
    cs = lambda a: pl.BlockSpec(a.shape, lambda b: (0,) * a.ndim)
    return pl.pallas_call(
        _compress_kernel,
        grid=(B,),
        in_specs=[pl.BlockSpec((None, 2, G, nc, width), lambda b: (b, 0, 0, 0, 0)),
                  cs(pe8), cs(w1), cs(w1ab), cs(w2bd_k), cs(w2bdT_v)],
        out_specs=[pl.BlockSpec((None, nc, LANES), lambda b: (b, 0, 0)),
                   pl.BlockSpec((None, LANES, nc), lambda b: (b, 0, 0))],
        out_shape=[jax.ShapeDtypeStruct((B, nc, LANES), bf16), jax.ShapeDtypeStruct((B, LANES, nc), bf16)],
        compiler_params=_cparams("parallel"),
    )(xc, pe8, w1, w1ab, w2bd_k, w2bdT_v)


def _nsa_kernel(qbT_ref, miscT_ref, ck_ref, cvT_ref, kvb_ref, vT_ref, bias_ref, bcmp_ref, ovT_ref, e3_ref,
                o_ref, qb_ref, s_ref, oT_ref, m_ref, l_ref, acc_ref, *, n_sel):
    j = pl.program_id(1)
    nc = ck_ref.shape[0]
    n_s = ovT_ref.shape[0]
    R = HEADS_PER_GROUP
    for h in range(N_HEADS_B):
        qb_ref[h] = _pad_rows(qbT_ref[h * HEAD_DIM:(h + 1) * HEAD_DIM, :], (h // R) * HEAD_DIM)
    krow = lax.broadcasted_iota(jnp.int32, (TK, TQ), 0)
    qcol = lax.broadcasted_iota(jnp.int32, (TK, TQ), 1)
    causal = krow <= qcol
    t_c = j * TQ + lax.broadcasted_iota(jnp.int32, (nc, TQ), 1)
    c_c = lax.broadcasted_iota(jnp.int32, (nc, TQ), 0)
    mask_c = c_c * CMP_STRIDE + (CMP_BLOCK - 1) <= t_c
    blk = lax.broadcasted_iota(jnp.int32, (n_s, TQ), 0)
    cur = (j * TQ + lax.broadcasted_iota(jnp.int32, (n_s, TQ), 1)) // SLC_BLOCK
    valid = blk <= cur
    forced = (blk == 0) | (blk == cur) | (blk == cur - 1)
    gates = jax.nn.sigmoid(miscT_ref[N_IDX_HEADS:N_IDX_HEADS + W_GB, :])
    win_blocks = WINDOW // TK
    cmp_per_q = TQ // CMP_STRIDE
    half = CMP_WIN // 2

    def ks_blk(kb):
        return kvb_ref[pl.ds(pl.multiple_of(kb * TK, TK), TK), 0:LANES]

    def kw_blk(kb):
        return kvb_ref[pl.ds(pl.multiple_of(kb * TK, TK), TK), LANES:2 * LANES]

    for g in range(N_KV_GROUPS_B):
        heads = [g * R + r for r in range(R)]
        qpads = [qb_ref[h] for h in heads]
        vs_rows = slice(g * HEAD_DIM, (g + 1) * HEAD_DIM)
        vw_rows = slice((N_KV_GROUPS_B + g) * HEAD_DIM, (N_KV_GROUPS_B + g + 1) * HEAD_DIM)

        psum = jnp.zeros((nc, TQ), f32)
        o_c = []
        for r, h in enumerate(heads):
            s_ref[...] = jnp.dot(ck_ref[...], qpads[r], preferred_element_type=f32)

            @pl.when(j == 0)
            def _(h=h):
                s_ref[0:half, :] += bcmp_ref[h, half:CMP_WIN, :]

            @pl.when(j > 0)
            def _(h=h):
                rows = pl.ds(pl.multiple_of(j * cmp_per_q - half, half), CMP_WIN)
                s_ref[rows, :] += bcmp_ref[h]

            s = jnp.where(mask_c, s_ref[...], NEG_MASK)
            m = jnp.max(s, axis=0, keepdims=True)
            e = jnp.where(mask_c, jnp.exp2(s - m), 0.0)
            p = e * (1.0 / jnp.maximum(jnp.sum(e, axis=0, keepdims=True), 1e-30))
            psum = psum + p
            o_c.append(jnp.dot(cvT_ref[vs_rows, :], p.astype(bf16), preferred_element_type=f32))

        p_hi = psum.astype(bf16)
        p_lo = (psum - p_hi.astype(f32)).astype(bf16)
        imp = (jnp.dot(ovT_ref[...], p_hi, preferred_element_type=f32)
               + jnp.dot(ovT_ref[...], p_lo, preferred_element_type=f32))
        score = jnp.where(valid, jnp.where(forced, FORCED_SCORE, imp), NEG_SCORE)
        selT = _rank_select(score, n_sel).astype(bf16)

        _attn_init(m_ref, l_ref, acc_ref)

        def tok_mask(kb):
            return jnp.dot(e3_ref[kb], selT, preferred_element_type=f32) > 0.5

        def far_sel(kb, _):
            _attn_step(qpads, ks_blk(kb), vT_ref[kb, vs_rows, :], _mask_add(tok_mask(kb)), None,
                       m_ref, l_ref, acc_ref)
            return 0

        lax.fori_loop(0, jnp.maximum(j - 1, 0), far_sel, 0)

        @pl.when(j >= 1)
        def _():
            bias = [bias_ref[h, 0:TK, :] for h in heads]
            _attn_step(qpads, ks_blk(j - 1), vT_ref[j - 1, vs_rows, :], _mask_add(tok_mask(j - 1)), bias,
                       m_ref, l_ref, acc_ref)

        bias_diag = [bias_ref[h, TK:2 * TK, :] for h in heads]
        _attn_step(qpads, ks_blk(j), vT_ref[j, vs_rows, :], _mask_add(tok_mask(j) & causal), bias_diag,
                   m_ref, l_ref, acc_ref)
        o_s = [_attn_out(r, l_ref, acc_ref) for r in range(R)]

        _attn_init(m_ref, l_ref, acc_ref)
        for d in range(win_blocks, 1, -1):
            @pl.when(j >= d)
            def _(d=d):
                madd = _mask_add(krow > qcol) if d == win_blocks else None
                _attn_step(qpads, kw_blk(j - d), vT_ref[j - d, vw_rows, :], madd, None, m_ref, l_ref, acc_ref)

        @pl.when(j >= 1)
        def _():
            bias = [bias_ref[h, 0:TK, :] for h in heads]
            madd = _mask_add(krow > qcol) if win_blocks == 1 else None
            _attn_step(qpads, kw_blk(j - 1), vT_ref[j - 1, vw_rows, :], madd, bias, m_ref, l_ref, acc_ref)

        _attn_step(qpads, kw_blk(j), vT_ref[j, vw_rows, :], _mask_add(causal), bias_diag, m_ref, l_ref, acc_ref)

        for r, h in enumerate(heads):
            o_w = _attn_out(r, l_ref, acc_ref)
            oT_ref[h * HEAD_DIM:(h + 1) * HEAD_DIM, :] = (
                gates[h:h + 1, :] * o_c[r]
                + gates[N_HEADS_B + h:N_HEADS_B + h + 1, :] * o_s[r]
                + gates[2 * N_HEADS_B + h:2 * N_HEADS_B + h + 1, :] * o_w)
    o_ref[...] = oT_ref[...].T.astype(o_ref.dtype)


def _nsa(qbT, miscT, ck, cvT, kvb, vT_blocks, biasT, bcmpT, ovT, e3T, B, S, n_sel):
    nq = S // TQ
    nkb = S // TK
    nc = ck.shape[1]
    col_spec = lambda n: pl.BlockSpec((n, TQ), lambda b, j: (0, b * nq + j))
    const = lambda a: pl.BlockSpec(a.shape, lambda b, j: (0,) * a.ndim)
    return pl.pallas_call(
        functools.partial(_nsa_kernel, n_sel=n_sel),
        grid=(B, nq),
        in_specs=[col_spec(W_QB), col_spec(W_MISC),
                  pl.BlockSpec((None, nc, LANES), lambda b, j: (b, 0, 0)),
                  pl.BlockSpec((None, LANES, nc), lambda b, j: (b, 0, 0)),
                  pl.BlockSpec((S, 2 * LANES), lambda b, j: (b, 1)),
                  pl.BlockSpec((None, nkb, 4 * HEAD_DIM, TK), lambda b, j: (b, 0, 0, 0)),
                  const(biasT), const(bcmpT), const(ovT), const(e3T)],
        out_specs=pl.BlockSpec((TQ, W_QB), lambda b, j: (b * nq + j, 0)),
        out_shape=jax.ShapeDtypeStruct((B * S, W_QB), bf16),
        scratch_shapes=[pltpu.VMEM((N_HEADS_B, LANES, TQ), bf16),
                        pltpu.VMEM((nc, TQ), f32),
                        pltpu.VMEM((W_QB, TQ), f32)] + _attn_scratch(HEADS_PER_GROUP),
        compiler_params=_cparams("parallel", "parallel"),
    )(qbT, miscT, ck, cvT, kvb, vT_blocks, biasT, bcmpT, ovT, e3T)


def _mix_kernel(oa_ref, ob_ref, gate_ref, x_ref, wa_ref, wb_ref, wo_ref, x1_ref):
    d = x_ref.shape[-1]
    ga = jax.nn.sigmoid(gate_ref[:, :d])
    gb = jax.nn.sigmoid(gate_ref[:, d:])
    mix = (ga * jnp.dot(oa_ref[...], wa_ref[...], preferred_element_type=f32)
           + gb * jnp.dot(ob_ref[...], wb_ref[...], preferred_element_type=f32))
    x1_ref[...] = x_ref[...] + jnp.dot(mix.astype(bf16), wo_ref[...], preferred_element_type=f32)


def _mix(oa, ob, gate, x2, wa, wb, wo, tm=512):
    rows, d = x2.shape
    rs = lambda w: pl.BlockSpec((tm, w), lambda i: (i, 0))
    cs = lambda a: pl.BlockSpec(a.shape, lambda i: (0, 0))
    return pl.pallas_call(
        _mix_kernel,
        grid=(rows // tm,),
        in_specs=[rs(W_QA), rs(W_QB), rs(2 * d), rs(d), cs(wa), cs(wb), cs(wo)],
        out_specs=rs(d),
        out_shape=jax.ShapeDtypeStruct((rows, d), f32),
        compiler_params=_cparams("parallel"),
    )(oa, ob, gate, x2, wa, wb, wo)


def _mlp_kernel(x1_ref, g_ref, w1_ref, w2_ref, gf_ref, o_ref, *, chunk):
    x1 = x1_ref[...]
    h = _rms(x1, g_ref[...]).astype(bf16)
    acc = jnp.zeros(x1.shape, f32)
    for c0 in range(0, w1_ref.shape[1], chunk):
        u = jnp.dot(h, w1_ref[:, c0:c0 + chunk], preferred_element_type=f32)
        u = jnp.square(jnp.maximum(u, 0.0)).astype(bf16)
        acc = acc + jnp.dot(u, w2_ref[c0:c0 + chunk, :], preferred_element_type=f32)
    o_ref[...] = _rms(x1 + acc, gf_ref[...])


def _mlp(x1, g, w1, w2, gf, tm=256, chunk=512):
    rows, d = x1.shape
    rs = pl.BlockSpec((tm, d), lambda i: (i, 0))
    cs = lambda a: pl.BlockSpec(a.shape, lambda i: (0, 0))
    ws = lambda a: pl.BlockSpec(a.shape, lambda i: (0, 0), pipeline_mode=pl.Buffered(1))
    return pl.pallas_call(
        functools.partial(_mlp_kernel, chunk=chunk),
        grid=(rows // tm,),
        in_specs=[rs, cs(g), ws(w1), ws(w2), cs(gf)],
        out_specs=rs,
        out_shape=jax.ShapeDtypeStruct((rows, d), f32),
        compiler_params=_cparams("parallel"),
    )(x1, g, w1, w2, gf)


def _t5_bucket(dist):
    n = jnp.maximum(dist, 0)
    max_exact = N_BUCKETS // 2
    nf = jnp.maximum(n, 1).astype(f32)
    large = max_exact + (jnp.log(nf / max_exact) / math.log(MAX_DISTANCE / max_exact)
                         * (N_BUCKETS - max_exact)).astype(jnp.int32)
    large = jnp.minimum(large, N_BUCKETS - 1)
    return jnp.where(n < max_exact, n, large)


def _bias_tables(rel_bias):
    tab = rel_bias.astype(f32) * LOG2E
    bucket = _t5_bucket(jnp.arange(MAX_DISTANCE + 1))
    by_dist = jnp.dot((bucket[:, None] == jnp.arange(N_BUCKETS)).astype(f32), tab,
                      precision=lax.Precision.HIGHEST)
    shifted = (by_dist - by_dist[MAX_DISTANCE]).T

    n_heads = shifted.shape[0]
    period = 4 * TK

    def toeplitz(offset, n_rows):
        start = (-offset) % period
        vec = jnp.concatenate([jnp.zeros((n_heads, start), f32), shifted[:, :MAX_DISTANCE],
                               jnp.zeros((n_heads, period - start - MAX_DISTANCE), f32)], axis=1)
        flat = jnp.tile(vec, (1, n_rows))[:, :n_rows * (period - 1)]
        return flat.reshape(n_heads, n_rows, period - 1)[:, :, :TQ]

    near = toeplitz(TK, 2 * TK)
    cmp = toeplitz(CMP_STRIDE * (CMP_WIN // 2) - (CMP_BLOCK - 1), CMP_STRIDE * CMP_WIN)[:, ::CMP_STRIDE]
    return near[:N_HEADS_A], near[N_HEADS_A:], cmp[N_HEADS_A:]


def _static_tables(S, nc):
    n_s = S // SLC_BLOCK
    cmp_start = np.arange(nc)[None, :] * CMP_STRIDE
    slc_start = np.arange(n_s)[:, None] * SLC_BLOCK
    ovT = np.clip(np.minimum(cmp_start + CMP_BLOCK, slc_start + SLC_BLOCK)
                  - np.maximum(cmp_start, slc_start), 0, None).astype(np.float32) / CMP_BLOCK
    nkb = S // TK
    tok_blk = (np.arange(nkb)[:, None, None] * TK + np.arange(TK)[None, :, None]) // SLC_BLOCK
    e3T = (np.arange(LANES)[None, None, :] == tok_blk).astype(np.float32)
    return jnp.asarray(ovT, bf16), jnp.asarray(e3T, bf16)


def kernel(x, norm_mix, w_in, cmp_pe_k, cmp_w1_k, cmp_w2_k, cmp_pe_v, cmp_w1_v, cmp_w2_v, rel_bias,
           w_branch_a, w_branch_b, w_out, norm_mlp, w_mlp_in, w_mlp_out, norm_final):
    B, S, D = x.shape
    assert norm_mix.shape[0] == 1 and S % TQ == 0 and (S // CMP_STRIDE) % LANES == 0
    assert WINDOW % TK == 0 and (S // SLC_BLOCK) % SUBLANES == 0 and S // SLC_BLOCK <= LANES
    G = N_KV_GROUPS_B
    rows = B * S
    nkb = S // TK
    nc = S // CMP_STRIDE
    k_top = min(TOPK_TOKENS, S // 4)
    n_sel = min(N_SLC_BLOCKS, S // SLC_BLOCK)
    x2 = x.reshape(rows, D)

    w = w_in[0]
    o_ka = W_QA
    o_va = o_ka + HEAD_DIM
    o_qi = o_va + HEAD_DIM
    o_ki = o_qi + W_QI
    o_wi = o_ki + IDX_DIM
    o_qb = o_wi + N_IDX_HEADS
    o_kvb = o_qb + W_QB
    o_gb = o_kvb + W_KVB
    o_gate = o_gb + W_GB
    kv = lambda kind: w[:, o_kvb + kind * G * HEAD_DIM:o_kvb + (kind + 1) * G * HEAD_DIM]
    pad = jnp.zeros((D, LANES - HEAD_DIM - IDX_DIM), w.dtype)
    w_rm = jnp.concatenate([w[:, o_ka:o_va], w[:, o_ki:o_wi], pad, kv(0), kv(1), kv(2), kv(4),
                            w[:, o_gate:]], axis=1).astype(bf16)
    w_t = jnp.concatenate([w[:, :o_ka], w[:, o_qi:o_ki], w[:, o_wi:o_qb], w[:, o_gb:o_gate],
                           w[:, o_qb:o_kvb], kv(3), kv(5), w[:, o_va:o_qi]], axis=1).T.astype(bf16)
    keys_a, kvb, gate, qaT, qiT, miscT, qbT, vT = _inproj(x2, norm_mix[0][None], w_rm, w_t)

    vT_blocks = vT.reshape(W_VT, B, nkb, TK).transpose(1, 2, 0, 3)
    bias_a, bias_b, bcmp = _bias_tables(rel_bias)
    ovT, e3T = _static_tables(S, nc)

    o_a = _dsa(qiT, miscT, keys_a, qaT, vT_blocks, bias_a, B, S, k_top)

    chunk_w = CMP_STRIDE * HEAD_DIM
    xc = (kvb[:, :2 * G * HEAD_DIM].reshape(B, S, 2, G, HEAD_DIM)
          .transpose(0, 2, 3, 1, 4).reshape(B, 2, G, nc, chunk_w))
    pe = jnp.stack([cmp_pe_k[0], cmp_pe_v[0]]).reshape(2, 1, 2 * chunk_w)
    pe8 = jnp.broadcast_to(pe, (2, SUBLANES, 2 * chunk_w)).astype(bf16)
    w1 = jnp.stack([cmp_w1_k[0], cmp_w1_v[0]]).astype(bf16)
    w1ab = jnp.concatenate([w1[:, :chunk_w], w1[:, chunk_w:]], axis=2)
    zeros = jnp.zeros((CMP_HIDDEN, HEAD_DIM), f32)
    bd = lambda w2: jnp.block([[w2, zeros], [zeros, w2]])
    ck, cvT = _compress(xc, pe8, w1, w1ab, bd(cmp_w2_k[0]).astype(bf16), bd(cmp_w2_v[0]).T.astype(bf16))

    o_b = _nsa(qbT, miscT, ck, cvT, kvb, vT_blocks, bias_b, bcmp, ovT, e3T, B, S, n_sel)

    x1 = _mix(o_a, o_b, gate, x2, w_branch_a[0].astype(bf16), w_branch_b[0].astype(bf16),
              w_out[0].astype(bf16))
    out = _mlp(x1, norm_mlp[0][None], w_mlp_in[0].astype(bf16), w_mlp_out[0].astype(bf16),
               norm_final[None])
    return out.reshape(B, S, D)
```

```python
import collections
import functools
import math

import numpy as np
import jax
import jax.numpy as jnp
from jax import lax
from jax.experimental import pallas as pl
from jax.experimental.pallas import tpu as pltpu

f32 = jnp.float32
bf16 = jnp.bfloat16

HEAD_DIM = 64
N_HEADS_A = 8
N_IDX_HEADS = 8
IDX_DIM = 32
TOPK_TOKENS = 256
N_HEADS_B = 8
N_KV_GROUPS_B = 2
HEADS_PER_GROUP = N_HEADS_B // N_KV_GROUPS_B
CMP_BLOCK = 32
CMP_STRIDE = 16
CMP_HIDDEN = 128
SLC_BLOCK = 64
N_SLC_BLOCKS = 16
WINDOW = 512
FORCED_SCORE = 1.0e4
N_BUCKETS = 32
MAX_DISTANCE = 128
EPS = 1e-6

W_QA = N_HEADS_A * HEAD_DIM
W_QI = N_IDX_HEADS * IDX_DIM
W_QB = N_HEADS_B * HEAD_DIM
W_KVB = 6 * N_KV_GROUPS_B * HEAD_DIM
W_GB = 3 * N_HEADS_B
W_MISC = N_IDX_HEADS + W_GB
W_VT = 5 * HEAD_DIM

LANES = 128
SUBLANES = 8
BF16_ROWS = 16
TQ = 256
TK = 256
QK_LOOKAHEAD = 2
ACC_ROWS = HEAD_DIM + BF16_ROWS
CMP_WIN = 32
LOG2E = 1.4426950408889634
NEG_MASK = -2.0e30
M_INIT = -1.0e30
NEG_SCORE = -3.0e38
BIG_SCORE = 3.0e38
MAX_SEARCH_ITERS = 400
TIE_CHECK_START = 22
COUNT_ROWS = 32
VMEM_LIMIT = 56 * 1024 * 1024

NT_DIMS = (((1,), (1,)), ((), ()))


def _cparams(*sem):
    return pltpu.CompilerParams(dimension_semantics=sem, vmem_limit_bytes=VMEM_LIMIT)


def _rms(x, g):
    return x * lax.rsqrt(jnp.mean(x * x, axis=-1, keepdims=True) + EPS) * g


def _inproj_kernel(x_ref, g_ref, w_ref, wT_ref, keys_ref, kcv_ref, ksw_ref, gate_ref,
                   qaT_ref, qiT_ref, miscT_ref, qbT_ref, vT_ref):
    h = _rms(x_ref[...], g_ref[...]).astype(bf16)
    q_scale = HEAD_DIM ** -0.5 * LOG2E
    off = 0
    for ref in (keys_ref, kcv_ref, ksw_ref, gate_ref):
        width = ref.shape[-1]
        for c0 in range(0, width, 512):
            c1 = min(c0 + 512, width)
            y = jnp.dot(h, w_ref[:, off + c0:off + c1], preferred_element_type=f32)
            ref[:, c0:c1] = y.astype(ref.dtype)
        off += width
    off = 0
    for ref, scale in ((qaT_ref, q_scale), (qiT_ref, None), (miscT_ref, None), (qbT_ref, q_scale),
                       (vT_ref, None)):
        n = ref.shape[0]
        y = lax.dot_general(wT_ref[off:off + n, :], h, NT_DIMS, preferred_element_type=f32)
        if scale is not None:
            y = y * scale
        ref[...] = y.astype(ref.dtype)
        off += n


def _inproj(x2, g, w_rm, w_t):
    rows, d = x2.shape
    tm = TK
    rm = ((LANES, bf16), (2 * LANES, f32), (2 * LANES, bf16), (2 * d, f32))
    tr = ((W_QA, bf16), (W_QI, bf16), (W_MISC, f32), (W_QB, bf16))
    return pl.pallas_call(
        _inproj_kernel,
        grid=(rows // tm,),
        in_specs=[pl.BlockSpec((tm, d), lambda i: (i, 0)),
                  pl.BlockSpec((1, d), lambda i: (0, 0)),
                  pl.BlockSpec(w_rm.shape, lambda i: (0, 0)),
                  pl.BlockSpec(w_t.shape, lambda i: (0, 0))],
        out_specs=([pl.BlockSpec((tm, w), lambda i: (i, 0)) for w, _ in rm]
                   + [pl.BlockSpec((n, tm), lambda i: (0, i)) for n, _ in tr]
                   + [pl.BlockSpec((None, W_VT, tm), lambda i: (i, 0, 0))]),
        out_shape=([jax.ShapeDtypeStruct((rows, w), dt) for w, dt in rm]
                   + [jax.ShapeDtypeStruct((n, rows), dt) for n, dt in tr]
                   + [jax.ShapeDtypeStruct((rows // tm, W_VT, tm), bf16)]),
        compiler_params=_cparams("parallel"),
    )(x2, g, w_rm, w_t)


def _count_ge(sc_ref, nkb, p):
    def body(kb, acc):
        hit = jnp.where(sc_ref[kb] >= p, 1.0, 0.0)
        return acc + jnp.sum(hit.reshape(TK // COUNT_ROWS, COUNT_ROWS, TQ), axis=0)
    acc = lax.fori_loop(0, nkb, body, jnp.zeros((COUNT_ROWS, TQ), f32))
    return jnp.sum(acc, axis=0, keepdims=True)


def _select_topk(sc_ref, nkb, rmin, rmax, nvalid, k):
    kf = float(k)
    active = nvalid > kf
    lo0 = jnp.where(active, rmin, NEG_SCORE)
    cl0 = jnp.where(active, nvalid, kf)
    hi0 = jnp.full_like(lo0, BIG_SCORE)
    zero = jnp.zeros_like(lo0)

    def pending(cl, tie):
        return (cl != kf) & (tie == 0.0)

    def cond(st):
        it, lo, hi, cl, ch, tie = st
        return (jnp.max(jnp.where(pending(cl, tie), 1.0, 0.0)) > 0.0) & (it < MAX_SEARCH_ITERS)

    def body(st):
        it, lo, hi, cl, ch, tie = st
        pend = pending(cl, tie)
        hie = jnp.minimum(hi, rmax)
        p = lo + (hie - lo) * 0.5
        p = jnp.where(p > lo, p, hie)
        cnt = _count_ge(sc_ref, nkb, p)
        ge = cnt >= kf
        up_lo = pend & ge
        up_hi = pend & jnp.logical_not(ge)
        lo = jnp.where(up_lo, p, lo)
        cl = jnp.where(up_lo, cnt, cl)
        hi = jnp.where(up_hi, p, hi)
        ch = jnp.where(up_hi, cnt, ch)

        def tie_check(args):
            lo, tie = args
            def scan(kb, c):
                dmin, dmax = c
                s = sc_ref[kb]
                dmin = jnp.minimum(dmin, jnp.min(jnp.where(s >= lo, s, BIG_SCORE), axis=0, keepdims=True))
                dmax = jnp.maximum(dmax, jnp.max(jnp.where(s < hi, s, NEG_SCORE), axis=0, keepdims=True))
                return dmin, dmax
            dmin, dmax = lax.fori_loop(
                0, nkb, scan, (jnp.full((1, TQ), BIG_SCORE, f32), jnp.full((1, TQ), NEG_SCORE, f32)))
            new_tie = pending(cl, tie) & (dmin == dmax)
            return jnp.where(new_tie, dmin, lo), jnp.where(new_tie, 1.0, tie)

        lo, tie = lax.cond(it >= TIE_CHECK_START, tie_check, lambda a: a, (lo, tie))
        return it + 1, lo, hi, cl, ch, tie

    _, lo, hi, cl, ch, tie = lax.while_loop(cond, body, (jnp.int32(0), lo0, hi0, cl0, zero, zero))
    any_tie = jnp.max(tie) > 0.0
    need = kf - ch

    @pl.when(any_tie)
    def _():
        r = lax.broadcasted_iota(jnp.int32, (TK, TK), 0)
        c = lax.broadcasted_iota(jnp.int32, (TK, TK), 1)
        tril = jnp.where(c <= r, 1.0, 0.0).astype(bf16)

        def body(kb, run):
            s = sc_ref[kb]
            eq = s == lo
            eqf = jnp.where(eq, 1.0, 0.0)
            cum = run + jnp.dot(tril, eqf.astype(bf16), preferred_element_type=f32)
            sel_tie = jnp.where((s > lo) | (eq & (cum <= need)), 1.0, -1.0)
            sel_all = jnp.where(s >= lo, 1.0, -1.0)
            sc_ref[kb] = jnp.where(tie > 0.0, sel_tie, sel_all)
            return run + jnp.sum(eqf, axis=0, keepdims=True)

        lax.fori_loop(0, nkb, body, zero)

    return jnp.where(any_tie, 0.0, lo)


def _rank_select(score, k):
    n, width = score.shape
    tiles = [score[v * SUBLANES:(v + 1) * SUBLANES] for v in range(n // SUBLANES)]
    ridx = lax.broadcasted_iota(jnp.int32, (SUBLANES, width), 0)
    ranks = [jnp.zeros((SUBLANES, width), f32) for _ in tiles]
    for i in range(n):
        si = jnp.broadcast_to(tiles[i // SUBLANES][i % SUBLANES:i % SUBLANES + 1], (SUBLANES, width))
        for v, x in enumerate(tiles):
            if v * SUBLANES > i:
                inc = jnp.where(si >= x, 1.0, 0.0)
            elif (v + 1) * SUBLANES <= i:
                inc = jnp.where(si > x, 1.0, 0.0)
            else:
                inc = jnp.where(ridx > i % SUBLANES, jnp.where(si >= x, 1.0, 0.0), jnp.where(si > x, 1.0, 0.0))
            ranks[v] = ranks[v] + inc
    flags = [jnp.where(r < float(k), 1.0, 0.0) for r in ranks]
    flags.append(jnp.zeros((LANES - n, width), f32))
    return jnp.concatenate(flags, axis=0)


Stream = collections.namedtuple("Stream", "q keys vT madd bias slot")


def _attn_init(m_ref, acc_ref):
    m_ref[...] = jnp.full(m_ref.shape, M_INIT, f32)
    acc_ref[...] = jnp.zeros(acc_ref.shape, f32)


def _attn_streams(streams, m_ref, acc_ref):
    def logits(i):
        st = streams[i]
        s = jnp.dot(st.keys, st.q, preferred_element_type=f32)
        if st.bias is not None:
            s = s + st.bias
        if st.madd is not None:
            s = s + st.madd
        return s

    def values(i, alpha, p):
        st = streams[i]
        acc_ref[st.slot] = alpha * acc_ref[st.slot] + jnp.dot(st.vT, p, preferred_element_type=f32)

    n = len(streams)
    pending = {i: logits(i) for i in range(min(QK_LOOKAHEAD, n))}
    deferred = None
    for i in range(n):
        if i + QK_LOOKAHEAD < n:
            pending[i + QK_LOOKAHEAD] = logits(i + QK_LOOKAHEAD)
        s = pending.pop(i)
        slot = streams[i].slot
        m_old = m_ref[slot]
        m_new = jnp.maximum(m_old, jnp.max(s, axis=0, keepdims=True))
        alpha = jnp.exp2(m_old - m_new)
        p = jnp.exp2(s - m_new)
        m_ref[slot] = m_new
        if deferred is not None:
            values(*deferred)
        deferred = (i, alpha, p.astype(bf16))
    values(*deferred)


def _attn_out(slot, acc_ref):
    acc = acc_ref[slot]
    return acc[0:HEAD_DIM] * (1.0 / jnp.maximum(acc[HEAD_DIM:HEAD_DIM + 1], 1e-30))


def _with_ones(vT):
    return jnp.concatenate([vT, jnp.ones((BF16_ROWS, vT.shape[1]), vT.dtype)], axis=0)


def _mask_add(mask):
    return jnp.where(mask, 0.0, NEG_MASK)


def _pad_rows(x, start):
    parts = []
    if start:
        parts.append(jnp.zeros((start, x.shape[1]), x.dtype))
    parts.append(x)
    if LANES - start - x.shape[0]:
        parts.append(jnp.zeros((LANES - start - x.shape[0], x.shape[1]), x.dtype))
    return jnp.concatenate(parts, axis=0)


def _far_blocks(n_far, pair_fn, single_fn):
    def body(i, _):
        pair_fn(2 * i, 2 * i + 1)
        return 0
    lax.fori_loop(0, n_far // 2, body, 0)

    @pl.when(n_far % 2 == 1)
    def _():
        single_fn(n_far - 1)


def _dsa_kernel(qiT_ref, miscT_ref, keys_ref, qaT_ref, vT_ref, bias_ref, o_ref,
                sc_ref, qi_ref, qa_ref, oT_ref, m_ref, acc_ref, *, k_top):
    j = pl.program_id(1)
    w_scale = (N_IDX_HEADS ** -0.5) * (IDX_DIM ** -0.5)
    w = miscT_ref[0:N_IDX_HEADS, :] * w_scale
    for h in range(N_IDX_HEADS):
        qi_ref[h] = _pad_rows(qiT_ref[h * IDX_DIM:(h + 1) * IDX_DIM, :], HEAD_DIM)
    for h in range(N_HEADS_A):
        qa_ref[h] = _pad_rows(qaT_ref[h * HEAD_DIM:(h + 1) * HEAD_DIM, :], 0)
    krow = lax.broadcasted_iota(jnp.int32, (TK, TQ), 0)
    qcol = lax.broadcasted_iota(jnp.int32, (TK, TQ), 1)
    causal = krow <= qcol

    def keys_blk(kb):
        return keys_ref[pl.ds(pl.multiple_of(kb * TK, TK), TK), :]

    def score_blk(kb):
        keys = keys_blk(kb)
        acc = None
        for h in range(N_IDX_HEADS):
            sh = jnp.dot(keys, qi_ref[h], preferred_element_type=f32)
            term = w[h:h + 1, :] * jnp.maximum(sh, 0.0)
            acc = term if acc is None else acc + term
        return acc

    def far_score(kb, carry):
        mn, mx = carry
        a = score_blk(kb)
        sc_ref[kb] = a
        return (jnp.minimum(mn, jnp.min(a, axis=0, keepdims=True)),
                jnp.maximum(mx, jnp.max(a, axis=0, keepdims=True)))

    mn, mx = lax.fori_loop(0, j, far_score,
                           (jnp.full((1, TQ), BIG_SCORE, f32), jnp.full((1, TQ), NEG_SCORE, f32)))
    a = score_blk(j)
    sc_ref[j] = jnp.where(causal, a, NEG_SCORE)
    rmin = jnp.minimum(mn, jnp.min(jnp.where(causal, a, BIG_SCORE), axis=0, keepdims=True))
    rmax = jnp.maximum(mx, jnp.max(jnp.where(causal, a, NEG_SCORE), axis=0, keepdims=True))
    nvalid = (j * TQ + qcol[0:1, :] + 1).astype(f32)
    lo = _select_topk(sc_ref, j + 1, rmin, rmax, nvalid, k_top)

    _attn_init(m_ref, acc_ref)

    def block_streams(kb, mask, bias_rows):
        keys, vT, madd = keys_blk(kb), _with_ones(vT_ref[kb]), _mask_add(mask)
        return [Stream(qa_ref[h], keys, vT, madd,
                       None if bias_rows is None else bias_ref[h, bias_rows, :], h) for h in range(N_HEADS_A)]

    def far(kb):
        return block_streams(kb, sc_ref[kb] >= lo, None)

    _far_blocks(jnp.maximum(j - 1, 0),
                lambda k0, k1: _attn_streams(far(k0) + far(k1), m_ref, acc_ref),
                lambda k0: _attn_streams(far(k0), m_ref, acc_ref))
    diag_mask = (sc_ref[j] >= lo) & causal

    @pl.when(j >= 1)
    def _():
        _attn_streams(block_streams(j - 1, sc_ref[j - 1] >= lo, slice(0, TK))
                      + block_streams(j, diag_mask, slice(TK, 2 * TK)), m_ref, acc_ref)

    @pl.when(j == 0)
    def _():
        _attn_streams(block_streams(j, diag_mask, slice(TK, 2 * TK)), m_ref, acc_ref)

    for h in range(N_HEADS_A):
        oT_ref[h * HEAD_DIM:(h + 1) * HEAD_DIM, :] = _attn_out(h, acc_ref)
    o_ref[...] = oT_ref[...].T.astype(o_ref.dtype)


def _attn_scratch(n_slots):
    return [pltpu.VMEM((n_slots, 1, TQ), f32), pltpu.VMEM((n_slots, ACC_ROWS, TQ), f32)]


def _dsa(qiT, miscT, keys, qaT, vT_blocks, biasT, B, S, k_top):
    nq = S // TQ
    nkb = S // TK
    col_spec = lambda n: pl.BlockSpec((n, TQ), lambda b, j: (0, b * nq + j))
    return pl.pallas_call(
        functools.partial(_dsa_kernel, k_top=k_top),
        grid=(B, nq),
        in_specs=[col_spec(W_QI), col_spec(W_MISC),
                  pl.BlockSpec((S, LANES), lambda b, j: (b, 0)),
                  col_spec(W_QA),
                  pl.BlockSpec((None, nkb, HEAD_DIM, TK), lambda b, j: (b, 0, 4, 0)),
                  pl.BlockSpec(biasT.shape, lambda b, j: (0, 0, 0))],
        out_specs=pl.BlockSpec((TQ, W_QA), lambda b, j: (b * nq + j, 0)),
        out_shape=jax.ShapeDtypeStruct((B * S, W_QA), bf16),
        scratch_shapes=[pltpu.VMEM((nkb, TK, TQ), f32),
                        pltpu.VMEM((N_IDX_HEADS, LANES, TQ), bf16),
                        pltpu.VMEM((N_HEADS_A, LANES, TQ), bf16),
                        pltpu.VMEM((W_QA, TQ), f32)] + _attn_scratch(N_HEADS_A),
        compiler_params=_cparams("parallel", "parallel"),
    )(qiT, miscT, keys, qaT, vT_blocks, biasT)


def _compress_kernel(xk_ref, xv_ref, pe_ref, w1_ref, w1tok_ref, w2bd_ref, w2bdT_ref, ck_ref, cvT_ref):
    nc = ck_ref.shape[0]
    G = N_KV_GROUPS_B

    def hidden(kind):
        x_ref = (xk_ref, xv_ref)[kind]
        const = jnp.dot(pe_ref[kind], w1_ref[kind], preferred_element_type=f32)[0:1]
        ab = jnp.zeros((nc, 2 * G * CMP_HIDDEN), f32)
        for i in range(CMP_STRIDE):
            tok = x_ref[pl.ds(i, nc, stride=CMP_STRIDE), :]
            ab = ab + jnp.dot(tok.astype(bf16), w1tok_ref[kind, i], preferred_element_type=f32)
        hs = []
        for g in range(G):
            first = ab[:, g * CMP_HIDDEN:(g + 1) * CMP_HIDDEN]
            second = pltpu.roll(ab[:, (G + g) * CMP_HIDDEN:(G + g + 1) * CMP_HIDDEN], nc - 1, 0)
            hs.append(jax.nn.gelu(first + second + const).astype(bf16))
        return jnp.concatenate(hs, axis=1)

    ck_ref[...] = jnp.dot(hidden(0), w2bd_ref[...], preferred_element_type=f32).astype(ck_ref.dtype)
    cvT_ref[...] = lax.dot_general(w2bdT_ref[...], hidden(1), NT_DIMS,
                                   preferred_element_type=f32).astype(cvT_ref.dtype)


def _compress(kcv, pe8, w1, w1tok, w2bd_k, w2bdT_v, B, S):
    nc = S // CMP_STRIDE
    cs = lambda a: pl.BlockSpec(a.shape, lambda b: (0,) * a.ndim)
    return pl.pallas_call(
        _compress_kernel,
        grid=(B,),
        in_specs=[pl.BlockSpec((S, LANES), lambda b: (b, 0)),
                  pl.BlockSpec((S, LANES), lambda b: (b, 1)),
                  cs(pe8), cs(w1), cs(w1tok), cs(w2bd_k), cs(w2bdT_v)],
        out_specs=[pl.BlockSpec((None, nc, LANES), lambda b: (b, 0, 0)),
                   pl.BlockSpec((None, LANES, nc), lambda b: (b, 0, 0))],
        out_shape=[jax.ShapeDtypeStruct((B, nc, LANES), bf16), jax.ShapeDtypeStruct((B, LANES, nc), bf16)],
        compiler_params=_cparams("parallel"),
    )(kcv, kcv, pe8, w1, w1tok, w2bd_k, w2bdT_v)


def _nsa_kernel(qbT_ref, miscT_ref, ck_ref, cvT_ref, ksw_ref, vT_ref, bias_ref, bcmp_ref, ovT_ref, e3_ref,
                o_ref, qb_ref, s_ref, oT_ref, m_ref, acc_ref, *, n_sel):
    j = pl.program_id(1)
    nc = ck_ref.shape[0]
    n_s = ovT_ref.shape[0]
    R = HEADS_PER_GROUP
    G = N_KV_GROUPS_B
    H = N_HEADS_B
    for h in range(H):
        qb_ref[h] = _pad_rows(qbT_ref[h * HEAD_DIM:(h + 1) * HEAD_DIM, :], (h // R) * HEAD_DIM)
    krow = lax.broadcasted_iota(jnp.int32, (TK, TQ), 0)
    qcol = lax.broadcasted_iota(jnp.int32, (TK, TQ), 1)
    causal = krow <= qcol
    t_c = j * TQ + lax.broadcasted_iota(jnp.int32, (nc, TQ), 1)
    c_c = lax.broadcasted_iota(jnp.int32, (nc, TQ), 0)
    mask_c = c_c * CMP_STRIDE + (CMP_BLOCK - 1) <= t_c
    blk = lax.broadcasted_iota(jnp.int32, (n_s, TQ), 0)
    cur = (j * TQ + lax.broadcasted_iota(jnp.int32, (n_s, TQ), 1)) // SLC_BLOCK
    valid = blk <= cur
    forced = (blk == 0) | (blk == cur) | (blk == cur - 1)
    gates = jax.nn.sigmoid(miscT_ref[N_IDX_HEADS:N_IDX_HEADS + W_GB, :])
    win_blocks = WINDOW // TK
    cmp_per_q = TQ // CMP_STRIDE
    half = CMP_WIN // 2

    for h in range(H):
        s_ref[h] = jnp.dot(ck_ref[...], qb_ref[h], preferred_element_type=f32)

    @pl.when(j == 0)
    def _():
        for h in range(H):
            s_ref[h, 0:half, :] += bcmp_ref[h, half:CMP_WIN, :]

    @pl.when(j > 0)
    def _():
        rows = pl.ds(pl.multiple_of(j * cmp_per_q - half, half), CMP_WIN)
        for h in range(H):
            s_ref[h, rows, :] += bcmp_ref[h]

    o_c = []
    psum = [jnp.zeros((nc, TQ), f32) for _ in range(G)]
    for h in range(H):
        g = h // R
        s = jnp.where(mask_c, s_ref[h], NEG_MASK)
        m = jnp.max(s, axis=0, keepdims=True)
        e = jnp.where(mask_c, jnp.exp2(s - m), 0.0)
        p = e * (1.0 / jnp.maximum(jnp.sum(e, axis=0, keepdims=True), 1e-30))
        psum[g] = psum[g] + p
        o_c.append(jnp.dot(cvT_ref[g * HEAD_DIM:(g + 1) * HEAD_DIM, :], p.astype(bf16),
                           preferred_element_type=f32))

    selT = []
    for g in range(G):
        p_hi = psum[g].astype(bf16)
        p_lo = (psum[g] - p_hi.astype(f32)).astype(bf16)
        imp = (jnp.dot(ovT_ref[...], p_hi, preferred_element_type=f32)
               + jnp.dot(ovT_ref[...], p_lo, preferred_element_type=f32))
        score = jnp.where(valid, jnp.where(forced, FORCED_SCORE, imp), NEG_SCORE)
        selT.append(_rank_select(score, n_sel).astype(bf16))

    _attn_init(m_ref, acc_ref)

    def branch_streams(kb, lanes, v_row0, masks, bias_rows, slot0):
        keys = ksw_ref[pl.ds(pl.multiple_of(kb * TK, TK), TK), lanes]
        out = []
        for g in range(G):
            vT = _with_ones(vT_ref[kb, v_row0 + g * HEAD_DIM:v_row0 + (g + 1) * HEAD_DIM, :])
            madd = None if masks[g] is None else _mask_add(masks[g])
            for h in range(g * R, (g + 1) * R):
                out.append(Stream(qb_ref[h], keys, vT, madd,
                                  None if bias_rows is None else bias_ref[h, bias_rows, :], slot0 + h))
        return out

    def tok_masks(kb, extra=None):
        ms = [jnp.dot(e3_ref[kb], selT[g], preferred_element_type=f32) > 0.5 for g in range(G)]
        return ms if extra is None else [m & extra for m in ms]

    def sel_streams(kb, extra=None, bias_rows=None):
        return branch_streams(kb, slice(0, LANES), 0, tok_masks(kb, extra), bias_rows, 0)

    def win_streams(kb, mask=None, bias_rows=None):
        return branch_streams(kb, slice(LANES, 2 * LANES), G * HEAD_DIM, [mask] * G, bias_rows, H)

    _far_blocks(jnp.maximum(j - 1, 0),
                lambda k0, k1: _attn_streams(sel_streams(k0) + sel_streams(k1), m_ref, acc_ref),
                lambda k0: _attn_streams(sel_streams(k0), m_ref, acc_ref))

    for d in range(win_blocks, 1, -1):
        @pl.when(j >= d)
        def _(d=d):
            _attn_streams(win_streams(j - d, (krow > qcol) if d == win_blocks else None), m_ref, acc_ref)

    prev_rows, diag_rows = slice(0, TK), slice(TK, 2 * TK)

    @pl.when(j >= 1)
    def _():
        _attn_streams(sel_streams(j - 1, None, prev_rows)
                      + win_streams(j - 1, (krow > qcol) if win_blocks == 1 else None, prev_rows),
                      m_ref, acc_ref)

    _attn_streams(sel_streams(j, causal, diag_rows) + win_streams(j, causal, diag_rows), m_ref, acc_ref)

    for h in range(H):
        oT_ref[h * HEAD_DIM:(h + 1) * HEAD_DIM, :] = (
            gates[h:h + 1, :] * o_c[h]
            + gates[H + h:H + h + 1, :] * _attn_out(h, acc_ref)
            + gates[2 * H + h:2 * H + h + 1, :] * _attn_out(H + h, acc_ref))
    o_ref[...] = oT_ref[...].T.astype(o_ref.dtype)


def _nsa(qbT, miscT, ck, cvT, ksw, vT_blocks, biasT, bcmpT, ovT, e3T, B, S, n_sel):
    nq = S // TQ
    nkb = S // TK
    nc = ck.shape[1]
    col_spec = lambda n: pl.BlockSpec((n, TQ), lambda b, j: (0, b * nq + j))
    const = lambda a: pl.BlockSpec(a.shape, lambda b, j: (0,) * a.ndim)
    return pl.pallas_call(
        functools.partial(_nsa_kernel, n_sel=n_sel),
        grid=(B, nq),
        in_specs=[col_spec(W_QB), col_spec(W_MISC),
                  pl.BlockSpec((None, nc, LANES), lambda b, j: (b, 0, 0)),
                  pl.BlockSpec((None, LANES, nc), lambda b, j: (b, 0, 0)),
                  pl.BlockSpec((S, 2 * LANES), lambda b, j: (b, 0)),
                  pl.BlockSpec((None, nkb, 4 * HEAD_DIM, TK), lambda b, j: (b, 0, 0, 0)),
                  const(biasT), const(bcmpT), const(ovT), const(e3T)],
        out_specs=pl.BlockSpec((TQ, W_QB), lambda b, j: (b * nq + j, 0)),
        out_shape=jax.ShapeDtypeStruct((B * S, W_QB), bf16),
        scratch_shapes=[pltpu.VMEM((N_HEADS_B, LANES, TQ), bf16),
                        pltpu.VMEM((N_HEADS_B, nc, TQ), f32),
                        pltpu.VMEM((W_QB, TQ), f32)] + _attn_scratch(2 * N_HEADS_B),
        compiler_params=_cparams("parallel", "parallel"),
    )(qbT, miscT, ck, cvT, ksw, vT_blocks, biasT, bcmpT, ovT, e3T)


def _mix_kernel(oa_ref, ob_ref, gate_ref, x_ref, wa_ref, wb_ref, wo_ref, x1_ref):
    d = x_ref.shape[-1]
    ga = jax.nn.sigmoid(gate_ref[:, :d])
    gb = jax.nn.sigmoid(gate_ref[:, d:])
    mix = (ga * jnp.dot(oa_ref[...], wa_ref[...], preferred_element_type=f32)
           + gb * jnp.dot(ob_ref[...], wb_ref[...], preferred_element_type=f32))
    x1_ref[...] = x_ref[...] + jnp.dot(mix.astype(bf16), wo_ref[...], preferred_element_type=f32)


def _mix(oa, ob, gate, x2, wa, wb, wo, tm=512):
    rows, d = x2.shape
    rs = lambda w: pl.BlockSpec((tm, w), lambda i: (i, 0))
    cs = lambda a: pl.BlockSpec(a.shape, lambda i: (0, 0))
    return pl.pallas_call(
        _mix_kernel,
        grid=(rows // tm,),
        in_specs=[rs(W_QA), rs(W_QB), rs(2 * d), rs(d), cs(wa), cs(wb), cs(wo)],
        out_specs=rs(d),
        out_shape=jax.ShapeDtypeStruct((rows, d), f32),
        compiler_params=_cparams("parallel"),
    )(oa, ob, gate, x2, wa, wb, wo)


def _mlp_kernel(x1_ref, g_ref, w1_ref, w2_ref, gf_ref, o_ref, *, chunk):
    x1 = x1_ref[...]
    h = _rms(x1, g_ref[...]).astype(bf16)
    acc = jnp.zeros(x1.shape, f32)
    for c0 in range(0, w1_ref.shape[1], chunk):
        u = jnp.dot(h, w1_ref[:, c0:c0 + chunk], preferred_element_type=f32)
        u = jnp.square(jnp.maximum(u, 0.0)).astype(bf16)
        acc = acc + jnp.dot(u, w2_ref[c0:c0 + chunk, :], preferred_element_type=f32)
    o_ref[...] = _rms(x1 + acc, gf_ref[...])


def _mlp(x1, g, w1, w2, gf, tm=256, chunk=512):
    rows, d = x1.shape
    rs = pl.BlockSpec((tm, d), lambda i: (i, 0))
    cs = lambda a: pl.BlockSpec(a.shape, lambda i: (0, 0))
    ws = lambda a: pl.BlockSpec(a.shape, lambda i: (0, 0), pipeline_mode=pl.Buffered(1))
    return pl.pallas_call(
        functools.partial(_mlp_kernel, chunk=chunk),
        grid=(rows // tm,),
        in_specs=[rs, cs(g), ws(w1), ws(w2), cs(gf)],
        out_specs=rs,
        out_shape=jax.ShapeDtypeStruct((rows, d), f32),
        compiler_params=_cparams("parallel"),
    )(x1, g, w1, w2, gf)


def _t5_bucket_np(dist):
    n = np.maximum(dist, 0)
    max_exact = N_BUCKETS // 2
    nf = np.maximum(n, 1).astype(np.float32)
    large = max_exact + (np.log(nf / np.float32(max_exact)) / np.float32(math.log(MAX_DISTANCE / max_exact))
                         * np.float32(N_BUCKETS - max_exact)).astype(np.int32)
    large = np.minimum(large, N_BUCKETS - 1)
    return np.where(n < max_exact, n, large)


def _bias_tables(rel_bias):
    shifted = (rel_bias.astype(f32) - rel_bias[N_BUCKETS - 1].astype(f32)) * LOG2E

    def lookup(dist, tab):
        bucket = np.where((dist >= 0) & (dist < MAX_DISTANCE), _t5_bucket_np(dist), N_BUCKETS - 1)
        onehot = (jnp.asarray(bucket, jnp.int32)[..., None] == jnp.arange(N_BUCKETS)).astype(f32)
        return jnp.einsum('kqb,bh->hkq', onehot, tab, precision=lax.Precision.HIGHEST)

    q = np.arange(TQ)[None, :]
    near = lookup(q + TK - np.arange(2 * TK)[:, None], shifted)
    m = np.arange(CMP_WIN)[:, None]
    cmp = lookup(q - (CMP_BLOCK - 1) - CMP_STRIDE * (m - CMP_WIN // 2), shifted[:, N_HEADS_A:])
    return near[:N_HEADS_A], near[N_HEADS_A:], cmp


def _static_tables(S, nc):
    n_s = S // SLC_BLOCK
    cmp_start = np.arange(nc)[None, :] * CMP_STRIDE
    slc_start = np.arange(n_s)[:, None] * SLC_BLOCK
    ovT = np.clip(np.minimum(cmp_start + CMP_BLOCK, slc_start + SLC_BLOCK)
                  - np.maximum(cmp_start, slc_start), 0, None).astype(np.float32) / CMP_BLOCK
    nkb = S // TK
    tok_blk = (np.arange(nkb)[:, None, None] * TK + np.arange(TK)[None, :, None]) // SLC_BLOCK
    e3T = (np.arange(LANES)[None, None, :] == tok_blk).astype(np.float32)
    return jnp.asarray(ovT, bf16), jnp.asarray(e3T, bf16)


def _compress_weights(w1, w2):
    z = jnp.zeros((HEAD_DIM, CMP_HIDDEN), w1.dtype)
    first = w1[:CMP_STRIDE * HEAD_DIM].reshape(CMP_STRIDE, HEAD_DIM, CMP_HIDDEN)
    second = w1[CMP_STRIDE * HEAD_DIM:].reshape(CMP_STRIDE, HEAD_DIM, CMP_HIDDEN)
    zz = jnp.broadcast_to(z, first.shape)
    tok = jnp.concatenate([jnp.concatenate([first, zz, second, zz], axis=2),
                           jnp.concatenate([zz, first, zz, second], axis=2)], axis=1)
    z2 = jnp.zeros((CMP_HIDDEN, HEAD_DIM), w2.dtype)
    return tok, jnp.block([[w2, z2], [z2, w2]])


def kernel(x, norm_mix, w_in, cmp_pe_k, cmp_w1_k, cmp_w2_k, cmp_pe_v, cmp_w1_v, cmp_w2_v, rel_bias,
           w_branch_a, w_branch_b, w_out, norm_mlp, w_mlp_in, w_mlp_out, norm_final):
    B, S, D = x.shape
    assert norm_mix.shape[0] == 1 and S % TQ == 0 and (S // CMP_STRIDE) % LANES == 0
    assert WINDOW % TK == 0 and (S // SLC_BLOCK) % SUBLANES == 0 and S // SLC_BLOCK <= LANES
    G = N_KV_GROUPS_B
    rows = B * S
    nkb = S // TK
    nc = S // CMP_STRIDE
    k_top = min(TOPK_TOKENS, S // 4)
    n_sel = min(N_SLC_BLOCKS, S // SLC_BLOCK)
    x2 = x.reshape(rows, D)

    w = w_in[0]
    o_ka = W_QA
    o_va = o_ka + HEAD_DIM
    o_qi = o_va + HEAD_DIM
    o_ki = o_qi + W_QI
    o_wi = o_ki + IDX_DIM
    o_qb = o_wi + N_IDX_HEADS
    o_kvb = o_qb + W_QB
    o_gb = o_kvb + W_KVB
    o_gate = o_gb + W_GB
    kv = lambda kind: w[:, o_kvb + kind * G * HEAD_DIM:o_kvb + (kind + 1) * G * HEAD_DIM]
    pad = jnp.zeros((D, LANES - HEAD_DIM - IDX_DIM), w.dtype)
    w_rm = jnp.concatenate([w[:, o_ka:o_va], w[:, o_ki:o_wi], pad, kv(0), kv(1), kv(2), kv(4),
                            w[:, o_gate:]], axis=1).astype(bf16)
    w_t = jnp.concatenate([w[:, :o_ka], w[:, o_qi:o_ki], w[:, o_wi:o_qb], w[:, o_gb:o_gate],
                           w[:, o_qb:o_kvb], kv(3), kv(5), w[:, o_va:o_qi]], axis=1).T.astype(bf16)
    keys_a, kcv, ksw, gate, qaT, qiT, miscT, qbT, vT = _inproj(x2, norm_mix[0][None], w_rm, w_t)
    vT_blocks = vT.reshape(B, nkb, W_VT, TK)

    bias_a, bias_b, bcmp = _bias_tables(rel_bias)
    ovT, e3T = _static_tables(S, nc)

    o_a = _dsa(qiT, miscT, keys_a, qaT, vT_blocks, bias_a, B, S, k_top)

    chunk_w = CMP_STRIDE * HEAD_DIM
    pe = jnp.stack([cmp_pe_k[0], cmp_pe_v[0]]).reshape(2, 1, 2 * chunk_w)
    pe8 = jnp.broadcast_to(pe, (2, SUBLANES, 2 * chunk_w)).astype(bf16)
    w1 = jnp.stack([cmp_w1_k[0], cmp_w1_v[0]]).astype(bf16)
    tok_k, w2bd_k = _compress_weights(cmp_w1_k[0], cmp_w2_k[0])
    tok_v, w2bd_v = _compress_weights(cmp_w1_v[0], cmp_w2_v[0])
    ck, cvT = _compress(kcv, pe8, w1, jnp.stack([tok_k, tok_v]).astype(bf16),
                        w2bd_k.astype(bf16), w2bd_v.T.astype(bf16), B, S)

    o_b = _nsa(qbT, miscT, ck, cvT, ksw, vT_blocks, bias_b, bcmp, ovT, e3T, B, S, n_sel)

    x1 = _mix(o_a, o_b, gate, x2, w_branch_a[0].astype(bf16), w_branch_b[0].astype(bf16),
              w_out[0].astype(bf16))
    out = _mlp(x1, norm_mlp[0][None], w_mlp_in[0].astype(bf16), w_mlp_out[0].astype(bf16),
               norm_final[None])
    return out.reshape(B, S, D)
```

```python
import collections
import functools
import math

import numpy as np
import jax
import jax.numpy as jnp
from jax import lax
from jax.experimental import pallas as pl
from jax.experimental.pallas import tpu as pltpu

f32 = jnp.float32
bf16 = jnp.bfloat16

HEAD_DIM = 64
N_HEADS_A = 8
N_IDX_HEADS = 8
IDX_DIM = 32
TOPK_TOKENS = 256
N_HEADS_B = 8
N_KV_GROUPS_B = 2
HEADS_PER_GROUP = N_HEADS_B // N_KV_GROUPS_B
CMP_BLOCK = 32
CMP_STRIDE = 16
CMP_HIDDEN = 128
SLC_BLOCK = 64
N_SLC_BLOCKS = 16
WINDOW = 512
FORCED_SCORE = 1.0e4
N_BUCKETS = 32
MAX_DISTANCE = 128
EPS = 1e-6

W_QA = N_HEADS_A * HEAD_DIM
W_QI = N_IDX_HEADS * IDX_DIM
W_QB = N_HEADS_B * HEAD_DIM
W_KVB = 6 * N_KV_GROUPS_B * HEAD_DIM
W_GB = 3 * N_HEADS_B
W_MISC = N_IDX_HEADS + W_GB
W_VT = 5 * HEAD_DIM

LANES = 128
SUBLANES = 8
BF16_ROWS = 16
TQ = 256
TK = 256
QK_LOOKAHEAD = 3
PV_DELAY = 2
ACC_ROWS = HEAD_DIM + BF16_ROWS
CMP_WIN = 32
LOG2E = 1.4426950408889634
NEG_MASK = -2.0 ** 101
M_INIT = -2.0 ** 100
NEG_SCORE = -3.0e38
BIG_SCORE = 3.0e38
MAX_SEARCH_ITERS = 400
TIE_CHECK_START = 22
COUNT_ROWS = 32
VMEM_LIMIT = 56 * 1024 * 1024

NT_DIMS = (((1,), (1,)), ((), ()))


def _cparams(*sem):
    return pltpu.CompilerParams(dimension_semantics=sem, vmem_limit_bytes=VMEM_LIMIT)


def _rms(x, g):
    return x * lax.rsqrt(jnp.mean(x * x, axis=-1, keepdims=True) + EPS) * g


def _inproj_kernel(x_ref, g_ref, w_ref, wT_ref, keys_ref, kcv_ref, ksw_ref, gate_ref,
                   qaT_ref, qiT_ref, miscT_ref, qbT_ref, vT_ref):
    h = _rms(x_ref[...], g_ref[...]).astype(bf16)
    q_scale = HEAD_DIM ** -0.5 * LOG2E
    off = 0
    for ref in (keys_ref, kcv_ref, ksw_ref, gate_ref):
        width = ref.shape[-1]
        for c0 in range(0, width, 512):
            c1 = min(c0 + 512, width)
            y = jnp.dot(h, w_ref[:, off + c0:off + c1], preferred_element_type=f32)
            ref[:, c0:c1] = y.astype(ref.dtype)
        off += width
    off = 0
    for ref, scale in ((qaT_ref, q_scale), (qiT_ref, None), (miscT_ref, None), (qbT_ref, q_scale),
                       (vT_ref, None)):
        n = ref.shape[0]
        y = lax.dot_general(wT_ref[off:off + n, :], h, NT_DIMS, preferred_element_type=f32)
        if scale is not None:
            y = y * scale
        ref[...] = y.astype(ref.dtype)
        off += n


def _inproj(x2, g, w_rm, w_t):
    rows, d = x2.shape
    tm = TK
    rm = ((LANES, bf16), (2 * LANES, f32), (2 * LANES, bf16), (2 * d, bf16))
    tr = ((W_QA, bf16), (W_QI, bf16), (W_MISC, f32), (W_QB, bf16))
    return pl.pallas_call(
        _inproj_kernel,
        grid=(rows // tm,),
        in_specs=[pl.BlockSpec((tm, d), lambda i: (i, 0)),
                  pl.BlockSpec((1, d), lambda i: (0, 0)),
                  pl.BlockSpec(w_rm.shape, lambda i: (0, 0)),
                  pl.BlockSpec(w_t.shape, lambda i: (0, 0))],
        out_specs=([pl.BlockSpec((tm, w), lambda i: (i, 0)) for w, _ in rm]
                   + [pl.BlockSpec((n, tm), lambda i: (0, i)) for n, _ in tr]
                   + [pl.BlockSpec((None, W_VT, tm), lambda i: (i, 0, 0))]),
        out_shape=([jax.ShapeDtypeStruct((rows, w), dt) for w, dt in rm]
                   + [jax.ShapeDtypeStruct((n, rows), dt) for n, dt in tr]
                   + [jax.ShapeDtypeStruct((rows // tm, W_VT, tm), bf16)]),
        compiler_params=_cparams("parallel"),
    )(x2, g, w_rm, w_t)


def _count_ge(sc_ref, nkb, p):
    def body(kb, acc):
        hit = jnp.where(sc_ref[kb] >= p, 1.0, 0.0)
        return acc + jnp.sum(hit.reshape(TK // COUNT_ROWS, COUNT_ROWS, TQ), axis=0)
    acc = lax.fori_loop(0, nkb, body, jnp.zeros((COUNT_ROWS, TQ), f32))
    return jnp.sum(acc, axis=0, keepdims=True)


def _select_topk(sc_ref, nkb, rmin, rmax, nvalid, k):
    kf = float(k)
    active = nvalid > kf
    lo0 = jnp.where(active, rmin, NEG_SCORE)
    cl0 = jnp.where(active, nvalid, kf)
    hi0 = jnp.full_like(lo0, BIG_SCORE)
    zero = jnp.zeros_like(lo0)

    def pending(cl, tie):
        return (cl != kf) & (tie == 0.0)

    def any_pending(st):
        return jnp.max(jnp.where(pending(st[3], st[5]), 1.0, 0.0)) > 0.0

    def bisect(st):
        it, lo, hi, cl, ch, tie = st
        pend = pending(cl, tie)
        hie = jnp.minimum(hi, rmax)
        p = lo + (hie - lo) * 0.5
        p = jnp.where(p > lo, p, hie)
        cnt = _count_ge(sc_ref, nkb, p)
        ge = cnt >= kf
        up_lo = pend & ge
        up_hi = pend & jnp.logical_not(ge)
        lo = jnp.where(up_lo, p, lo)
        cl = jnp.where(up_lo, cnt, cl)
        hi = jnp.where(up_hi, p, hi)
        ch = jnp.where(up_hi, cnt, ch)
        return it + 1, lo, hi, cl, ch, tie

    def tie_check(st):
        it, lo, hi, cl, ch, tie = st
        def scan(kb, c):
            dmin, dmax = c
            s = sc_ref[kb]
            dmin = jnp.minimum(dmin, jnp.min(jnp.where(s >= lo, s, BIG_SCORE), axis=0, keepdims=True))
            dmax = jnp.maximum(dmax, jnp.max(jnp.where(s < hi, s, NEG_SCORE), axis=0, keepdims=True))
            return dmin, dmax
        dmin, dmax = lax.fori_loop(
            0, nkb, scan, (jnp.full((1, TQ), BIG_SCORE, f32), jnp.full((1, TQ), NEG_SCORE, f32)))
        new_tie = pending(cl, tie) & (dmin == dmax)
        return it, jnp.where(new_tie, dmin, lo), hi, cl, ch, jnp.where(new_tie, 1.0, tie)

    st = (jnp.int32(0), lo0, hi0, cl0, zero, zero)
    st = lax.while_loop(lambda st: any_pending(st) & (st[0] < TIE_CHECK_START), bisect, st)
    st = lax.while_loop(lambda st: any_pending(st) & (st[0] < MAX_SEARCH_ITERS),
                        lambda st: bisect(tie_check(st)), st)
    _, lo, hi, cl, ch, tie = st
    any_tie = jnp.max(tie) > 0.0
    need = kf - ch

    @pl.when(any_tie)
    def _():
        r = lax.broadcasted_iota(jnp.int32, (TK, TK), 0)
        c = lax.broadcasted_iota(jnp.int32, (TK, TK), 1)
        tril = jnp.where(c <= r, 1.0, 0.0).astype(bf16)

        def body(kb, run):
            s = sc_ref[kb]
            eq = s == lo
            eqf = jnp.where(eq, 1.0, 0.0)
            cum = run + jnp.dot(tril, eqf.astype(bf16), preferred_element_type=f32)
            sel_tie = jnp.where((s > lo) | (eq & (cum <= need)), 1.0, -1.0)
            sel_all = jnp.where(s >= lo, 1.0, -1.0)
            sc_ref[kb] = jnp.where(tie > 0.0, sel_tie, sel_all)
            return run + jnp.sum(eqf, axis=0, keepdims=True)

        lax.fori_loop(0, nkb, body, zero)

    return jnp.where(any_tie, 0.0, lo)


def _rank_select(score, k):
    n, width = score.shape
    tiles = [score[v * SUBLANES:(v + 1) * SUBLANES] for v in range(n // SUBLANES)]
    ridx = lax.broadcasted_iota(jnp.int32, (SUBLANES, width), 0)
    ranks = [jnp.zeros((SUBLANES, width), f32) for _ in tiles]
    for i in range(n):
        si = jnp.broadcast_to(tiles[i // SUBLANES][i % SUBLANES:i % SUBLANES + 1], (SUBLANES, width))
        for v, x in enumerate(tiles):
            if v * SUBLANES > i:
                inc = jnp.where(si >= x, 1.0, 0.0)
            elif (v + 1) * SUBLANES <= i:
                inc = jnp.where(si > x, 1.0, 0.0)
            else:
                inc = jnp.where(ridx > i % SUBLANES, jnp.where(si >= x, 1.0, 0.0), jnp.where(si > x, 1.0, 0.0))
            ranks[v] = ranks[v] + inc
    flags = [jnp.where(r < float(k), 1.0, 0.0) for r in ranks]
    flags.append(jnp.zeros((LANES - n, width), f32))
    return jnp.concatenate(flags, axis=0)


Stream = collections.namedtuple("Stream", "q keys vT madd bias slot")


def _attn_init(m_ref, acc_ref):
    m_ref[...] = jnp.full(m_ref.shape, M_INIT, f32)
    acc_ref[...] = jnp.zeros(acc_ref.shape, f32)


def _attn_streams(streams, m_ref, acc_ref):
    def logits(i):
        st = streams[i]
        s = jnp.dot(st.keys, st.q, preferred_element_type=f32).astype(bf16)
        if st.bias is not None:
            s = s + st.bias
        if st.madd is not None:
            s = s + st.madd
        return s

    def values(i, alpha, p):
        st = streams[i]
        acc_ref[st.slot] = alpha * acc_ref[st.slot] + jnp.dot(st.vT, p, preferred_element_type=f32)

    n = len(streams)
    pending = {i: logits(i) for i in range(min(QK_LOOKAHEAD, n))}
    deferred = []
    for i in range(n):
        if i + QK_LOOKAHEAD < n:
            pending[i + QK_LOOKAHEAD] = logits(i + QK_LOOKAHEAD)
        s = pending.pop(i)
        slot = streams[i].slot
        m_old = m_ref[slot]
        m_new = jnp.maximum(m_old, jnp.max(s, axis=0, keepdims=True).astype(f32))
        alpha = jnp.exp2(m_old - m_new)
        p = jnp.exp2(s - m_new.astype(bf16))
        m_ref[slot] = m_new
        deferred.append((i, alpha, p))
        if len(deferred) > PV_DELAY:
            values(*deferred.pop(0))
    for d in deferred:
        values(*d)


def _attn_out(slot, acc_ref):
    acc = acc_ref[slot]
    return acc[0:HEAD_DIM] * (1.0 / jnp.maximum(acc[HEAD_DIM:HEAD_DIM + 1], 1e-30))


def _with_ones(vT):
    return jnp.concatenate([vT, jnp.ones((BF16_ROWS, vT.shape[1]), vT.dtype)], axis=0)


def _mask_add(mask):
    return jnp.where(mask, 0.0, NEG_MASK).astype(bf16)


def _pad_rows(x, start):
    parts = []
    if start:
        parts.append(jnp.zeros((start, x.shape[1]), x.dtype))
    parts.append(x)
    if LANES - start - x.shape[0]:
        parts.append(jnp.zeros((LANES - start - x.shape[0], x.shape[1]), x.dtype))
    return jnp.concatenate(parts, axis=0)


def _far_blocks(n_far, pair_fn, single_fn):
    def body(i, _):
        pair_fn(2 * i, 2 * i + 1)
        return 0
    lax.fori_loop(0, n_far // 2, body, 0)

    @pl.when(n_far % 2 == 1)
    def _():
        single_fn(n_far - 1)


def _dsa_kernel(qiT_ref, miscT_ref, keys_ref, qaT_ref, vT_ref, bias_ref, o_ref,
                sc_ref, qi_ref, qa_ref, oT_ref, m_ref, acc_ref, *, k_top):
    j = pl.program_id(1)
    w_scale = (N_IDX_HEADS ** -0.5) * (IDX_DIM ** -0.5)
    w = miscT_ref[0:N_IDX_HEADS, :] * w_scale
    for h in range(N_IDX_HEADS):
        qi_ref[h] = _pad_rows(qiT_ref[h * IDX_DIM:(h + 1) * IDX_DIM, :], HEAD_DIM)
    for h in range(N_HEADS_A):
        qa_ref[h] = _pad_rows(qaT_ref[h * HEAD_DIM:(h + 1) * HEAD_DIM, :], 0)
    krow = lax.broadcasted_iota(jnp.int32, (TK, TQ), 0)
    qcol = lax.broadcasted_iota(jnp.int32, (TK, TQ), 1)
    causal = krow <= qcol

    def keys_blk(kb):
        return keys_ref[pl.ds(pl.multiple_of(kb * TK, TK), TK), :]

    def score_blk(kb):
        keys = keys_blk(kb)
        acc = None
        for h in range(N_IDX_HEADS):
            sh = jnp.dot(keys, qi_ref[h], preferred_element_type=f32)
            term = w[h:h + 1, :] * jnp.maximum(sh, 0.0)
            acc = term if acc is None else acc + term
        return acc

    def far_score(kb, carry):
        mn, mx = carry
        a = score_blk(kb)
        sc_ref[kb] = a
        return (jnp.minimum(mn, jnp.min(a, axis=0, keepdims=True)),
                jnp.maximum(mx, jnp.max(a, axis=0, keepdims=True)))

    mn, mx = lax.fori_loop(0, j, far_score,
                           (jnp.full((1, TQ), BIG_SCORE, f32), jnp.full((1, TQ), NEG_SCORE, f32)))
    a = score_blk(j)
    sc_ref[j] = jnp.where(causal, a, NEG_SCORE)
    rmin = jnp.minimum(mn, jnp.min(jnp.where(causal, a, BIG_SCORE), axis=0, keepdims=True))
    rmax = jnp.maximum(mx, jnp.max(jnp.where(causal, a, NEG_SCORE), axis=0, keepdims=True))
    nvalid = (j * TQ + qcol[0:1, :] + 1).astype(f32)
    lo = _select_topk(sc_ref, j + 1, rmin, rmax, nvalid, k_top)

    _attn_init(m_ref, acc_ref)

    def block_streams(kb, mask, bias_rows):
        keys, vT, madd = keys_blk(kb), _with_ones(vT_ref[kb]), _mask_add(mask)
        return [Stream(qa_ref[h], keys, vT, madd,
                       None if bias_rows is None else bias_ref[h, bias_rows, :], h) for h in range(N_HEADS_A)]

    def far(kb):
        return block_streams(kb, sc_ref[kb] >= lo, None)

    _far_blocks(jnp.maximum(j - 1, 0),
                lambda k0, k1: _attn_streams(far(k0) + far(k1), m_ref, acc_ref),
                lambda k0: _attn_streams(far(k0), m_ref, acc_ref))
    diag_mask = (sc_ref[j] >= lo) & causal

    @pl.when(j >= 1)
    def _():
        _attn_streams(block_streams(j - 1, sc_ref[j - 1] >= lo, slice(0, TK))
                      + block_streams(j, diag_mask, slice(TK, 2 * TK)), m_ref, acc_ref)

    @pl.when(j == 0)
    def _():
        _attn_streams(block_streams(j, diag_mask, slice(TK, 2 * TK)), m_ref, acc_ref)

    for h in range(N_HEADS_A):
        oT_ref[h * HEAD_DIM:(h + 1) * HEAD_DIM, :] = _attn_out(h, acc_ref)
    o_ref[...] = oT_ref[...].T.astype(o_ref.dtype)


def _attn_scratch(n_slots):
    return [pltpu.VMEM((n_slots, 1, TQ), f32), pltpu.VMEM((n_slots, ACC_ROWS, TQ), f32)]


def _dsa(qiT, miscT, keys, qaT, vT_blocks, biasT, B, S, k_top):
    nq = S // TQ
    nkb = S // TK
    col_spec = lambda n: pl.BlockSpec((n, TQ), lambda b, j: (0, b * nq + j))
    return pl.pallas_call(
        functools.partial(_dsa_kernel, k_top=k_top),
        grid=(B, nq),
        in_specs=[col_spec(W_QI), col_spec(W_MISC),
                  pl.BlockSpec((S, LANES), lambda b, j: (b, 0)),
                  col_spec(W_QA),
                  pl.BlockSpec((None, nkb, HEAD_DIM, TK), lambda b, j: (b, 0, 4, 0)),
                  pl.BlockSpec(biasT.shape, lambda b, j: (0, 0, 0))],
        out_specs=pl.BlockSpec((TQ, W_QA), lambda b, j: (b * nq + j, 0)),
        out_shape=jax.ShapeDtypeStruct((B * S, W_QA), bf16),
        scratch_shapes=[pltpu.VMEM((nkb, TK, TQ), f32),
                        pltpu.VMEM((N_IDX_HEADS, LANES, TQ), bf16),
                        pltpu.VMEM((N_HEADS_A, LANES, TQ), bf16),
                        pltpu.VMEM((W_QA, TQ), f32)] + _attn_scratch(N_HEADS_A),
        compiler_params=_cparams("parallel", "parallel"),
    )(qiT, miscT, keys, qaT, vT_blocks, biasT)


def _compress_kernel(xk_ref, xv_ref, pe_ref, w1_ref, w1tok_ref, w2bd_ref, w2bdT_ref, ck_ref, cvT_ref):
    nc = ck_ref.shape[0]
    G = N_KV_GROUPS_B

    def hidden(kind):
        x_ref = (xk_ref, xv_ref)[kind]
        const = jnp.dot(pe_ref[kind], w1_ref[kind], preferred_element_type=f32)[0:1]
        ab = jnp.zeros((nc, 2 * G * CMP_HIDDEN), f32)
        for i in range(CMP_STRIDE):
            tok = x_ref[pl.ds(i, nc, stride=CMP_STRIDE), :]
            ab = ab + jnp.dot(tok.astype(bf16), w1tok_ref[kind, i], preferred_element_type=f32)
        hs = []
        for g in range(G):
            first = ab[:, g * CMP_HIDDEN:(g + 1) * CMP_HIDDEN]
            second = pltpu.roll(ab[:, (G + g) * CMP_HIDDEN:(G + g + 1) * CMP_HIDDEN], nc - 1, 0)
            hs.append(jax.nn.gelu(first + second + const).astype(bf16))
        return jnp.concatenate(hs, axis=1)

    ck_ref[...] = jnp.dot(hidden(0), w2bd_ref[...], preferred_element_type=f32).astype(ck_ref.dtype)
    cvT_ref[...] = lax.dot_general(w2bdT_ref[...], hidden(1), NT_DIMS,
                                   preferred_element_type=f32).astype(cvT_ref.dtype)


def _compress(kcv, pe8, w1, w1tok, w2bd_k, w2bdT_v, B, S):
    nc = S // CMP_STRIDE
    cs = lambda a: pl.BlockSpec(a.shape, lambda b: (0,) * a.ndim)
    return pl.pallas_call(
        _compress_kernel,
        grid=(B,),
        in_specs=[pl.BlockSpec((S, LANES), lambda b: (b, 0)),
                  pl.BlockSpec((S, LANES), lambda b: (b, 1)),
                  cs(pe8), cs(w1), cs(w1tok), cs(w2bd_k), cs(w2bdT_v)],
        out_specs=[pl.BlockSpec((None, nc, LANES), lambda b: (b, 0, 0)),
                   pl.BlockSpec((None, LANES, nc), lambda b: (b, 0, 0))],
        out_shape=[jax.ShapeDtypeStruct((B, nc, LANES), bf16), jax.ShapeDtypeStruct((B, LANES, nc), bf16)],
        compiler_params=_cparams("parallel"),
    )(kcv, kcv, pe8, w1, w1tok, w2bd_k, w2bdT_v)


def _nsa_kernel(qbT_ref, miscT_ref, ck_ref, cvT_ref, ksw_ref, vT_ref, bias_ref, bcmp_ref, ovT_ref, e3_ref,
                o_ref, qb_ref, s_ref, oT_ref, m_ref, acc_ref, *, n_sel):
    j = pl.program_id(1)
    nc = ck_ref.shape[0]
    n_s = ovT_ref.shape[0]
    R = HEADS_PER_GROUP
    G = N_KV_GROUPS_B
    H = N_HEADS_B
    for h in range(H):
        qb_ref[h] = _pad_rows(qbT_ref[h * HEAD_DIM:(h + 1) * HEAD_DIM, :], (h // R) * HEAD_DIM)
    krow = lax.broadcasted_iota(jnp.int32, (TK, TQ), 0)
    qcol = lax.broadcasted_iota(jnp.int32, (TK, TQ), 1)
    causal = krow <= qcol
    t_c = j * TQ + lax.broadcasted_iota(jnp.int32, (nc, TQ), 1)
    c_c = lax.broadcasted_iota(jnp.int32, (nc, TQ), 0)
    mask_c = c_c * CMP_STRIDE + (CMP_BLOCK - 1) <= t_c
    blk = lax.broadcasted_iota(jnp.int32, (n_s, TQ), 0)
    cur = (j * TQ + lax.broadcasted_iota(jnp.int32, (n_s, TQ), 1)) // SLC_BLOCK
    valid = blk <= cur
    forced = (blk == 0) | (blk == cur) | (blk == cur - 1)
    gates = jax.nn.sigmoid(miscT_ref[N_IDX_HEADS:N_IDX_HEADS + W_GB, :])
    win_blocks = WINDOW // TK
    cmp_per_q = TQ // CMP_STRIDE
    half = CMP_WIN // 2

    for h in range(H):
        s_ref[h] = jnp.dot(ck_ref[...], qb_ref[h], preferred_element_type=f32)

    @pl.when(j == 0)
    def _():
        for h in range(H):
            s_ref[h, 0:half, :] += bcmp_ref[h, half:CMP_WIN, :]

    @pl.when(j > 0)
    def _():
        rows = pl.ds(pl.multiple_of(j * cmp_per_q - half, half), CMP_WIN)
        for h in range(H):
            s_ref[h, rows, :] += bcmp_ref[h]

    o_c = []
    psum = [jnp.zeros((nc, TQ), f32) for _ in range(G)]
    for h in range(H):
        g = h // R
        s = jnp.where(mask_c, s_ref[h], NEG_MASK)
        m = jnp.max(s, axis=0, keepdims=True)
        e = jnp.where(mask_c, jnp.exp2(s - m), 0.0)
        p = e * (1.0 / jnp.maximum(jnp.sum(e, axis=0, keepdims=True), 1e-30))
        psum[g] = psum[g] + p
        o_c.append(jnp.dot(cvT_ref[g * HEAD_DIM:(g + 1) * HEAD_DIM, :], p.astype(bf16),
                           preferred_element_type=f32))

    selT = []
    for g in range(G):
        p_hi = psum[g].astype(bf16)
        p_lo = (psum[g] - p_hi.astype(f32)).astype(bf16)
        imp = (jnp.dot(ovT_ref[...], p_hi, preferred_element_type=f32)
               + jnp.dot(ovT_ref[...], p_lo, preferred_element_type=f32))
        score = jnp.where(valid, jnp.where(forced, FORCED_SCORE, imp), NEG_SCORE)
        selT.append(_rank_select(score, n_sel).astype(bf16))

    _attn_init(m_ref, acc_ref)

    def branch_streams(kb, lanes, v_row0, masks, bias_rows, slot0):
        keys = ksw_ref[pl.ds(pl.multiple_of(kb * TK, TK), TK), lanes]
        out = []
        for g in range(G):
            vT = _with_ones(vT_ref[kb, v_row0 + g * HEAD_DIM:v_row0 + (g + 1) * HEAD_DIM, :])
            madd = None if masks[g] is None else _mask_add(masks[g])
            for h in range(g * R, (g + 1) * R):
                out.append(Stream(qb_ref[h], keys, vT, madd,
                                  None if bias_rows is None else bias_ref[h, bias_rows, :], slot0 + h))
        return out

    def tok_masks(kb, extra=None):
        ms = [jnp.dot(e3_ref[kb], selT[g], preferred_element_type=f32) > 0.5 for g in range(G)]
        return ms if extra is None else [m & extra for m in ms]

    def sel_streams(kb, extra=None, bias_rows=None):
        return branch_streams(kb, slice(0, LANES), 0, tok_masks(kb, extra), bias_rows, 0)

    def win_streams(kb, mask=None, bias_rows=None):
        return branch_streams(kb, slice(LANES, 2 * LANES), G * HEAD_DIM, [mask] * G, bias_rows, H)

    _far_blocks(jnp.maximum(j - 1, 0),
                lambda k0, k1: _attn_streams(sel_streams(k0) + sel_streams(k1), m_ref, acc_ref),
                lambda k0: _attn_streams(sel_streams(k0), m_ref, acc_ref))

    prev_rows, diag_rows = slice(0, TK), slice(TK, 2 * TK)

    def near_streams(n_before):
        out = []
        for d in range(min(n_before, win_blocks), 1, -1):
            out += win_streams(j - d, (krow > qcol) if d == win_blocks else None)
        if n_before >= 1:
            out += sel_streams(j - 1, None, prev_rows)
            out += win_streams(j - 1, (krow > qcol) if win_blocks == 1 else None, prev_rows)
        return out + sel_streams(j, causal, diag_rows) + win_streams(j, causal, diag_rows)

    for n_before in range(win_blocks + 1):
        @pl.when((j >= n_before) if n_before == win_blocks else (j == n_before))
        def _(n_before=n_before):
            _attn_streams(near_streams(n_before), m_ref, acc_ref)

    for h in range(H):
        oT_ref[h * HEAD_DIM:(h + 1) * HEAD_DIM, :] = (
            gates[h:h + 1, :] * o_c[h]
            + gates[H + h:H + h + 1, :] * _attn_out(h, acc_ref)
            + gates[2 * H + h:2 * H + h + 1, :] * _attn_out(H + h, acc_ref))
    o_ref[...] = oT_ref[...].T.astype(o_ref.dtype)


def _nsa(qbT, miscT, ck, cvT, ksw, vT_blocks, biasT, bcmpT, ovT, e3T, B, S, n_sel):
    nq = S // TQ
    nkb = S // TK
    nc = ck.shape[1]
    col_spec = lambda n: pl.BlockSpec((n, TQ), lambda b, j: (0, b * nq + j))
    const = lambda a: pl.BlockSpec(a.shape, lambda b, j: (0,) * a.ndim)
    return pl.pallas_call(
        functools.partial(_nsa_kernel, n_sel=n_sel),
        grid=(B, nq),
        in_specs=[col_spec(W_QB), col_spec(W_MISC),
                  pl.BlockSpec((None, nc, LANES), lambda b, j: (b, 0, 0)),
                  pl.BlockSpec((None, LANES, nc), lambda b, j: (b, 0, 0)),
                  pl.BlockSpec((S, 2 * LANES), lambda b, j: (b, 0)),
                  pl.BlockSpec((None, nkb, 4 * HEAD_DIM, TK), lambda b, j: (b, 0, 0, 0)),
                  const(biasT), const(bcmpT), const(ovT), const(e3T)],
        out_specs=pl.BlockSpec((TQ, W_QB), lambda b, j: (b * nq + j, 0)),
        out_shape=jax.ShapeDtypeStruct((B * S, W_QB), bf16),
        scratch_shapes=[pltpu.VMEM((N_HEADS_B, LANES, TQ), bf16),
                        pltpu.VMEM((N_HEADS_B, nc, TQ), f32),
                        pltpu.VMEM((W_QB, TQ), f32)] + _attn_scratch(2 * N_HEADS_B),
        compiler_params=_cparams("parallel", "parallel"),
    )(qbT, miscT, ck, cvT, ksw, vT_blocks, biasT, bcmpT, ovT, e3T)


def _mix_kernel(oa_ref, ob_ref, gate_ref, x_ref, wa_ref, wb_ref, wo_ref, x1_ref):
    d = x_ref.shape[-1]
    ga = jax.nn.sigmoid(gate_ref[:, :d].astype(f32))
    gb = jax.nn.sigmoid(gate_ref[:, d:].astype(f32))
    mix = (ga * jnp.dot(oa_ref[...], wa_ref[...], preferred_element_type=f32)
           + gb * jnp.dot(ob_ref[...], wb_ref[...], preferred_element_type=f32))
    x1_ref[...] = x_ref[...] + jnp.dot(mix.astype(bf16), wo_ref[...], preferred_element_type=f32)


def _mix(oa, ob, gate, x2, wa, wb, wo, tm=512):
    rows, d = x2.shape
    rs = lambda w: pl.BlockSpec((tm, w), lambda i: (i, 0))
    cs = lambda a: pl.BlockSpec(a.shape, lambda i: (0, 0))
    return pl.pallas_call(
        _mix_kernel,
        grid=(rows // tm,),
        in_specs=[rs(W_QA), rs(W_QB), rs(2 * d), rs(d), cs(wa), cs(wb), cs(wo)],
        out_specs=rs(d),
        out_shape=jax.ShapeDtypeStruct((rows, d), f32),
        compiler_params=_cparams("parallel"),
    )(oa, ob, gate, x2, wa, wb, wo)


def _mlp_kernel(x1_ref, g_ref, w1_ref, w2_ref, gf_ref, o_ref, *, chunk):
    x1 = x1_ref[...]
    h = _rms(x1, g_ref[...]).astype(bf16)
    acc = jnp.zeros(x1.shape, f32)
    for c0 in range(0, w1_ref.shape[1], chunk):
        u = jnp.dot(h, w1_ref[:, c0:c0 + chunk], preferred_element_type=f32)
        u = jnp.square(jnp.maximum(u, 0.0)).astype(bf16)
        acc = acc + jnp.dot(u, w2_ref[c0:c0 + chunk, :], preferred_element_type=f32)
    o_ref[...] = _rms(x1 + acc, gf_ref[...])


def _mlp(x1, g, w1, w2, gf, tm=256, chunk=512):
    rows, d = x1.shape
    rs = pl.BlockSpec((tm, d), lambda i: (i, 0))
    cs = lambda a: pl.BlockSpec(a.shape, lambda i: (0, 0))
    ws = lambda a: pl.BlockSpec(a.shape, lambda i: (0, 0), pipeline_mode=pl.Buffered(1))
    return pl.pallas_call(
        functools.partial(_mlp_kernel, chunk=chunk),
        grid=(rows // tm,),
        in_specs=[rs, cs(g), ws(w1), ws(w2), cs(gf)],
        out_specs=rs,
        out_shape=jax.ShapeDtypeStruct((rows, d), f32),
        compiler_params=_cparams("parallel"),
    )(x1, g, w1, w2, gf)


def _t5_bucket_np(dist):
    n = np.maximum(dist, 0)
    max_exact = N_BUCKETS // 2
    nf = np.maximum(n, 1).astype(np.float32)
    large = max_exact + (np.log(nf / np.float32(max_exact)) / np.float32(math.log(MAX_DISTANCE / max_exact))
                         * np.float32(N_BUCKETS - max_exact)).astype(np.int32)
    large = np.minimum(large, N_BUCKETS - 1)
    return np.where(n < max_exact, n, large)


def _bias_tables(rel_bias):
    shifted = (rel_bias.astype(f32) - rel_bias[N_BUCKETS - 1].astype(f32)) * LOG2E

    def lookup(dist, tab):
        bucket = np.where((dist >= 0) & (dist < MAX_DISTANCE), _t5_bucket_np(dist), N_BUCKETS - 1)
        onehot = (jnp.asarray(bucket, jnp.int32)[..., None] == jnp.arange(N_BUCKETS)).astype(f32)
        return jnp.einsum('kqb,bh->hkq', onehot, tab, precision=lax.Precision.HIGHEST)

    q = np.arange(TQ)[None, :]
    near = lookup(q + TK - np.arange(2 * TK)[:, None], shifted).astype(bf16)
    m = np.arange(CMP_WIN)[:, None]
    cmp = lookup(q - (CMP_BLOCK - 1) - CMP_STRIDE * (m - CMP_WIN // 2), shifted[:, N_HEADS_A:])
    return near[:N_HEADS_A], near[N_HEADS_A:], cmp


def _static_tables(S, nc):
    n_s = S // SLC_BLOCK
    cmp_start = np.arange(nc)[None, :] * CMP_STRIDE
    slc_start = np.arange(n_s)[:, None] * SLC_BLOCK
    ovT = np.clip(np.minimum(cmp_start + CMP_BLOCK, slc_start + SLC_BLOCK)
                  - np.maximum(cmp_start, slc_start), 0, None).astype(np.float32) / CMP_BLOCK
    nkb = S // TK
    tok_blk = (np.arange(nkb)[:, None, None] * TK + np.arange(TK)[None, :, None]) // SLC_BLOCK
    e3T = (np.arange(LANES)[None, None, :] == tok_blk).astype(np.float32)
    return jnp.asarray(ovT, bf16), jnp.asarray(e3T, bf16)


def _compress_weights(w1, w2):
    z = jnp.zeros((HEAD_DIM, CMP_HIDDEN), w1.dtype)
    first = w1[:CMP_STRIDE * HEAD_DIM].reshape(CMP_STRIDE, HEAD_DIM, CMP_HIDDEN)
    second = w1[CMP_STRIDE * HEAD_DIM:].reshape(CMP_STRIDE, HEAD_DIM, CMP_HIDDEN)
    zz = jnp.broadcast_to(z, first.shape)
    tok = jnp.concatenate([jnp.concatenate([first, zz, second, zz], axis=2),
                           jnp.concatenate([zz, first, zz, second], axis=2)], axis=1)
    z2 = jnp.zeros((CMP_HIDDEN, HEAD_DIM), w2.dtype)
    return tok, jnp.block([[w2, z2], [z2, w2]])


def kernel(x, norm_mix, w_in, cmp_pe_k, cmp_w1_k, cmp_w2_k, cmp_pe_v, cmp_w1_v, cmp_w2_v, rel_bias,
           w_branch_a, w_branch_b, w_out, norm_mlp, w_mlp_in, w_mlp_out, norm_final):
    B, S, D = x.shape
    assert norm_mix.shape[0] == 1 and S % TQ == 0 and (S // CMP_STRIDE) % LANES == 0
    assert WINDOW % TK == 0 and (S // SLC_BLOCK) % SUBLANES == 0 and S // SLC_BLOCK <= LANES
    G = N_KV_GROUPS_B
    rows = B * S
    nkb = S // TK
    nc = S // CMP_STRIDE
    k_top = min(TOPK_TOKENS, S // 4)
    n_sel = min(N_SLC_BLOCKS, S // SLC_BLOCK)
    x2 = x.reshape(rows, D)

    w = w_in[0]
    o_ka = W_QA
    o_va = o_ka + HEAD_DIM
    o_qi = o_va + HEAD_DIM
    o_ki = o_qi + W_QI
    o_wi = o_ki + IDX_DIM
    o_qb = o_wi + N_IDX_HEADS
    o_kvb = o_qb + W_QB
    o_gb = o_kvb + W_KVB
    o_gate = o_gb + W_GB
    kv = lambda kind: w[:, o_kvb + kind * G * HEAD_DIM:o_kvb + (kind + 1) * G * HEAD_DIM]
    pad = jnp.zeros((D, LANES - HEAD_DIM - IDX_DIM), w.dtype)
    w_rm = jnp.concatenate([w[:, o_ka:o_va], w[:, o_ki:o_wi], pad, kv(0), kv(1), kv(2), kv(4),
                            w[:, o_gate:]], axis=1).astype(bf16)
    w_t = jnp.concatenate([w[:, :o_ka], w[:, o_qi:o_ki], w[:, o_wi:o_qb], w[:, o_gb:o_gate],
                           w[:, o_qb:o_kvb], kv(3), kv(5), w[:, o_va:o_qi]], axis=1).T.astype(bf16)
    keys_a, kcv, ksw, gate, qaT, qiT, miscT, qbT, vT = _inproj(x2, norm_mix[0][None], w_rm, w_t)
    vT_blocks = vT.reshape(B, nkb, W_VT, TK)

    bias_a, bias_b, bcmp = _bias_tables(rel_bias)
    ovT, e3T = _static_tables(S, nc)

    o_a = _dsa(qiT, miscT, keys_a, qaT, vT_blocks, bias_a, B, S, k_top)

    chunk_w = CMP_STRIDE * HEAD_DIM
    pe = jnp.stack([cmp_pe_k[0], cmp_pe_v[0]]).reshape(2, 1, 2 * chunk_w)
    pe8 = jnp.broadcast_to(pe, (2, SUBLANES, 2 * chunk_w)).astype(bf16)
    w1 = jnp.stack([cmp_w1_k[0], cmp_w1_v[0]]).astype(bf16)
    tok_k, w2bd_k = _compress_weights(cmp_w1_k[0], cmp_w2_k[0])
    tok_v, w2bd_v = _compress_weights(cmp_w1_v[0], cmp_w2_v[0])
    ck, cvT = _compress(kcv, pe8, w1, jnp.stack([tok_k, tok_v]).astype(bf16),
                        w2bd_k.astype(bf16), w2bd_v.T.astype(bf16), B, S)

    o_b = _nsa(qbT, miscT, ck, cvT, ksw, vT_blocks, bias_b, bcmp, ovT, e3T, B, S, n_sel)

    x1 = _mix(o_a, o_b, gate, x2, w_branch_a[0].astype(bf16), w_branch_b[0].astype(bf16),
              w_out[0].astype(bf16))
    out = _mlp(x1, norm_mlp[0][None], w_mlp_in[0].astype(bf16), w_mlp_out[0].astype(bf16),
               norm_final[None])
    return out.reshape(B, S, D)
```

```python
import collections
import functools
import math

import numpy as np
import jax
import jax.numpy as jnp
from jax import lax
from jax.experimental import pallas as pl
from jax.experimental.pallas import tpu as pltpu

f32 = jnp.float32
bf16 = jnp.bfloat16

HEAD_DIM = 64
N_HEADS_A = 8
N_IDX_HEADS = 8
IDX_DIM = 32
TOPK_TOKENS = 256
N_HEADS_B = 8
N_KV_GROUPS_B = 2
HEADS_PER_GROUP = N_HEADS_B // N_KV_GROUPS_B
CMP_BLOCK = 32
CMP_STRIDE = 16
CMP_HIDDEN = 128
SLC_BLOCK = 64
N_SLC_BLOCKS = 16
WINDOW = 512
FORCED_SCORE = 1.0e4
N_BUCKETS = 32
MAX_DISTANCE = 128
EPS = 1e-6

W_QA = N_HEADS_A * HEAD_DIM
W_QI = N_IDX_HEADS * IDX_DIM
W_QB = N_HEADS_B * HEAD_DIM
W_KVB = 6 * N_KV_GROUPS_B * HEAD_DIM
W_GB = 3 * N_HEADS_B
W_MISC = N_IDX_HEADS + W_GB
W_VT = 5 * HEAD_DIM

LANES = 128
SUBLANES = 8
BF16_ROWS = 16
TQ = 256
TK = 256
QK_LOOKAHEAD = 3
PV_DELAY = 2
ACC_ROWS = HEAD_DIM + BF16_ROWS
CMP_WIN = 32
LOG2E = 1.4426950408889634
NEG_MASK = -2.0 ** 101
M_INIT = -2.0 ** 100
NEG_SCORE = -3.0e38
BIG_SCORE = 3.0e38
BIG_SCORE16 = 2.0 ** 127
COARSE_ITERS = 16
MAX_SEARCH_ITERS = 400
TIE_CHECK_START = 12
COUNT_ROWS = 32
VMEM_LIMIT = 56 * 1024 * 1024

NT_DIMS = (((1,), (1,)), ((), ()))


def _cparams(*sem):
    return pltpu.CompilerParams(dimension_semantics=sem, vmem_limit_bytes=VMEM_LIMIT)


def _rms(x, g):
    return x * lax.rsqrt(jnp.mean(x * x, axis=-1, keepdims=True) + EPS) * g


def _inproj_kernel(x_ref, g_ref, w_ref, wT_ref, keys_ref, kcv_ref, ksw_ref, gate_ref,
                   qaT_ref, qiT_ref, miscT_ref, qbT_ref, vT_ref):
    h = _rms(x_ref[...], g_ref[...]).astype(bf16)
    q_scale = HEAD_DIM ** -0.5 * LOG2E
    off = 0
    for ref in (keys_ref, kcv_ref, ksw_ref, gate_ref):
        width = ref.shape[-1]
        for c0 in range(0, width, 512):
            c1 = min(c0 + 512, width)
            y = jnp.dot(h, w_ref[:, off + c0:off + c1], preferred_element_type=f32)
            ref[:, c0:c1] = y.astype(ref.dtype)
        off += width
    off = 0
    for ref, scale in ((qaT_ref, q_scale), (qiT_ref, None), (miscT_ref, None), (qbT_ref, q_scale),
                       (vT_ref, None)):
        n = ref.shape[0]
        y = lax.dot_general(wT_ref[off:off + n, :], h, NT_DIMS, preferred_element_type=f32)
        if scale is not None:
            y = y * scale
        ref[...] = y.astype(ref.dtype)
        off += n


def _inproj(x2, g, w_rm, w_t):
    rows, d = x2.shape
    tm = TK
    rm = ((LANES, bf16), (2 * LANES, f32), (2 * LANES, bf16), (2 * d, bf16))
    tr = ((W_QA, bf16), (W_QI, bf16), (W_MISC, f32), (W_QB, bf16))
    return pl.pallas_call(
        _inproj_kernel,
        grid=(rows // tm,),
        in_specs=[pl.BlockSpec((tm, d), lambda i: (i, 0)),
                  pl.BlockSpec((1, d), lambda i: (0, 0)),
                  pl.BlockSpec(w_rm.shape, lambda i: (0, 0)),
                  pl.BlockSpec(w_t.shape, lambda i: (0, 0))],
        out_specs=([pl.BlockSpec((tm, w), lambda i: (i, 0)) for w, _ in rm]
                   + [pl.BlockSpec((n, tm), lambda i: (0, i)) for n, _ in tr]
                   + [pl.BlockSpec((None, W_VT, tm), lambda i: (i, 0, 0))]),
        out_shape=([jax.ShapeDtypeStruct((rows, w), dt) for w, dt in rm]
                   + [jax.ShapeDtypeStruct((n, rows), dt) for n, dt in tr]
                   + [jax.ShapeDtypeStruct((rows // tm, W_VT, tm), bf16)]),
        compiler_params=_cparams("parallel"),
    )(x2, g, w_rm, w_t)


def _count_ge(sc_ref, nkb, p):
    def body(kb, acc):
        hit = jnp.where(sc_ref[kb] >= p, 1.0, 0.0)
        return acc + jnp.sum(hit.reshape(TK // COUNT_ROWS, COUNT_ROWS, TQ), axis=0)
    acc = lax.fori_loop(0, nkb, body, jnp.zeros((COUNT_ROWS, TQ), f32))
    return jnp.sum(acc, axis=0, keepdims=True)


def _count_ge16(sc16_ref, nkb, p16):
    one, nil = jnp.ones((), bf16), jnp.zeros((), bf16)
    def body(kb, acc):
        hit = jnp.where(sc16_ref[kb] >= p16, one, nil).reshape(TK // COUNT_ROWS, COUNT_ROWS, TQ)
        for i in range(TK // COUNT_ROWS):
            acc = acc + hit[i]
        return acc
    acc = lax.fori_loop(0, nkb, body, jnp.zeros((COUNT_ROWS, TQ), bf16))
    return jnp.sum(acc.astype(f32), axis=0, keepdims=True)


def _coarse_bracket(sc16_ref, nkb, rmin, rmax, active, kf):
    rmax16 = rmax.astype(bf16).astype(f32)

    def pivot(lo16, hi16):
        mid = ((lo16 + jnp.minimum(hi16, rmax16)) * 0.5).astype(bf16).astype(f32)
        return mid, active & (mid > lo16) & (mid < hi16)

    def cond(st):
        it, lo16, hi16 = st
        return (jnp.max(jnp.where(pivot(lo16, hi16)[1], 1.0, 0.0)) > 0.0) & (it < COARSE_ITERS)

    def body(st):
        it, lo16, hi16 = st
        mid, ok = pivot(lo16, hi16)
        ge = _count_ge16(sc16_ref, nkb, mid.astype(bf16)) >= kf
        return (it + 1, jnp.where(ok & ge, mid, lo16), jnp.where(ok & jnp.logical_not(ge), mid, hi16))

    _, lo16, hi16 = lax.while_loop(
        cond, body, (jnp.int32(0), rmin.astype(bf16).astype(f32), jnp.full_like(rmin, BIG_SCORE16)))
    below = jnp.maximum(jnp.abs(lo16) * 2.0 ** -7, 1e-37)
    return lo16 - below, jnp.where(hi16 < BIG_SCORE16, hi16, BIG_SCORE)


def _select_topk(sc_ref, sc16_ref, nkb, rmin, rmax, nvalid, k):
    kf = float(k)
    active = nvalid > kf
    lo_c, hi_c = _coarse_bracket(sc16_ref, nkb, rmin, rmax, active, kf)
    lo0 = jnp.where(active, lo_c, NEG_SCORE)
    cl0 = jnp.where(active, nvalid, kf)
    hi0 = jnp.where(active, hi_c, BIG_SCORE)
    zero = jnp.zeros_like(lo0)

    def pending(cl, tie):
        return (cl != kf) & (tie == 0.0)

    def any_pending(st):
        return jnp.max(jnp.where(pending(st[3], st[5]), 1.0, 0.0)) > 0.0

    def bisect(st):
        it, lo, hi, cl, ch, tie = st
        pend = pending(cl, tie)
        hie = jnp.minimum(hi, rmax)
        p = lo + (hie - lo) * 0.5
        p = jnp.where(p > lo, p, hie)
        cnt = _count_ge(sc_ref, nkb, p)
        ge = cnt >= kf
        up_lo = pend & ge
        up_hi = pend & jnp.logical_not(ge)
        lo = jnp.where(up_lo, p, lo)
        cl = jnp.where(up_lo, cnt, cl)
        hi = jnp.where(up_hi, p, hi)
        ch = jnp.where(up_hi, cnt, ch)
        return it + 1, lo, hi, cl, ch, tie

    def tie_check(st):
        it, lo, hi, cl, ch, tie = st
        def scan(kb, c):
            dmin, dmax = c
            s = sc_ref[kb]
            dmin = jnp.minimum(dmin, jnp.min(jnp.where(s >= lo, s, BIG_SCORE), axis=0, keepdims=True))
            dmax = jnp.maximum(dmax, jnp.max(jnp.where(s < hi, s, NEG_SCORE), axis=0, keepdims=True))
            return dmin, dmax
        dmin, dmax = lax.fori_loop(
            0, nkb, scan, (jnp.full((1, TQ), BIG_SCORE, f32), jnp.full((1, TQ), NEG_SCORE, f32)))
        new_tie = pending(cl, tie) & (dmin == dmax)
        return it, jnp.where(new_tie, dmin, lo), hi, cl, ch, jnp.where(new_tie, 1.0, tie)

    st = (jnp.int32(0), lo0, hi0, cl0, zero, zero)
    st = lax.while_loop(lambda st: any_pending(st) & (st[0] < TIE_CHECK_START), bisect, st)
    st = lax.while_loop(lambda st: any_pending(st) & (st[0] < MAX_SEARCH_ITERS),
                        lambda st: bisect(tie_check(st)), st)
    _, lo, hi, cl, ch, tie = st
    any_tie = jnp.max(tie) > 0.0

    @pl.when(any_tie)
    def _():
        r = lax.broadcasted_iota(jnp.int32, (TK, TK), 0)
        c = lax.broadcasted_iota(jnp.int32, (TK, TK), 1)
        tril = jnp.where(c <= r, 1.0, 0.0).astype(bf16)
        above = lax.fori_loop(
            0, nkb, lambda kb, n: n + jnp.sum(jnp.where(sc_ref[kb] > lo, 1.0, 0.0), axis=0, keepdims=True), zero)
        need = kf - above

        def body(kb, run):
            s = sc_ref[kb]
            eq = s == lo
            eqf = jnp.where(eq, 1.0, 0.0)
            cum = run + jnp.dot(tril, eqf.astype(bf16), preferred_element_type=f32)
            sel_tie = jnp.where((s > lo) | (eq & (cum <= need)), 1.0, -1.0)
            sel_all = jnp.where(s >= lo, 1.0, -1.0)
            sc_ref[kb] = jnp.where(tie > 0.0, sel_tie, sel_all)
            return run + jnp.sum(eqf, axis=0, keepdims=True)

        lax.fori_loop(0, nkb, body, zero)

    return jnp.where(any_tie, 0.0, lo)


def _rank_select(score, k):
    n, width = score.shape
    tiles = [score[v * SUBLANES:(v + 1) * SUBLANES] for v in range(n // SUBLANES)]
    ridx = lax.broadcasted_iota(jnp.int32, (SUBLANES, width), 0)
    ranks = [jnp.zeros((SUBLANES, width), f32) for _ in tiles]
    for i in range(n):
        si = jnp.broadcast_to(tiles[i // SUBLANES][i % SUBLANES:i % SUBLANES + 1], (SUBLANES, width))
        for v, x in enumerate(tiles):
            if v * SUBLANES > i:
                inc = jnp.where(si >= x, 1.0, 0.0)
            elif (v + 1) * SUBLANES <= i:
                inc = jnp.where(si > x, 1.0, 0.0)
            else:
                inc = jnp.where(ridx > i % SUBLANES, jnp.where(si >= x, 1.0, 0.0), jnp.where(si > x, 1.0, 0.0))
            ranks[v] = ranks[v] + inc
    flags = [jnp.where(r < float(k), 1.0, 0.0) for r in ranks]
    flags.append(jnp.zeros((LANES - n, width), f32))
    return jnp.concatenate(flags, axis=0)


Stream = collections.namedtuple("Stream", "q keys vT madd bias slot")


def _attn_init(m_ref, acc_ref):
    m_ref[...] = jnp.full(m_ref.shape, M_INIT, f32)
    acc_ref[...] = jnp.zeros(acc_ref.shape, f32)


def _attn_streams(streams, m_ref, acc_ref):
    def logits(i):
        st = streams[i]
        s = jnp.dot(st.keys, st.q, preferred_element_type=f32).astype(bf16)
        if st.bias is not None:
            s = s + st.bias
        if st.madd is not None:
            s = s + st.madd
        return s

    def values(i, alpha, p):
        st = streams[i]
        acc_ref[st.slot] = alpha * acc_ref[st.slot] + jnp.dot(st.vT, p, preferred_element_type=f32)

    n = len(streams)
    pending = {i: logits(i) for i in range(min(QK_LOOKAHEAD, n))}
    deferred = []
    for i in range(n):
        if i + QK_LOOKAHEAD < n:
            pending[i + QK_LOOKAHEAD] = logits(i + QK_LOOKAHEAD)
        s = pending.pop(i)
        slot = streams[i].slot
        m_old = m_ref[slot]
        m_new = jnp.maximum(m_old, jnp.max(s, axis=0, keepdims=True).astype(f32))
        alpha = jnp.exp2(m_old - m_new)
        p = jnp.exp2(s - m_new.astype(bf16))
        m_ref[slot] = m_new
        deferred.append((i, alpha, p))
        if len(deferred) > PV_DELAY:
            values(*deferred.pop(0))
    for d in deferred:
        values(*d)


def _attn_out(slot, acc_ref):
    acc = acc_ref[slot]
    return acc[0:HEAD_DIM] * (1.0 / jnp.maximum(acc[HEAD_DIM:HEAD_DIM + 1], 1e-30))


def _with_ones(vT):
    return jnp.concatenate([vT, jnp.ones((BF16_ROWS, vT.shape[1]), vT.dtype)], axis=0)


def _mask_add(mask):
    return jnp.where(mask, 0.0, NEG_MASK).astype(bf16)


def _pad_rows(x, start):
    parts = []
    if start:
        parts.append(jnp.zeros((start, x.shape[1]), x.dtype))
    parts.append(x)
    if LANES - start - x.shape[0]:
        parts.append(jnp.zeros((LANES - start - x.shape[0], x.shape[1]), x.dtype))
    return jnp.concatenate(parts, axis=0)


def _far_blocks(n_far, pair_fn, single_fn):
    def body(i, _):
        pair_fn(2 * i, 2 * i + 1)
        return 0
    lax.fori_loop(0, n_far // 2, body, 0)

    @pl.when(n_far % 2 == 1)
    def _():
        single_fn(n_far - 1)


def _dsa_kernel(qiT_ref, miscT_ref, keys_ref, qaT_ref, vT_ref, bias_ref, o_ref,
                sc_ref, sc16_ref, qi_ref, qa_ref, oT_ref, m_ref, acc_ref, *, k_top):
    j = pl.program_id(1)
    w_scale = (N_IDX_HEADS ** -0.5) * (IDX_DIM ** -0.5)
    w = miscT_ref[0:N_IDX_HEADS, :] * w_scale
    for h in range(N_IDX_HEADS):
        qi_ref[h] = _pad_rows(qiT_ref[h * IDX_DIM:(h + 1) * IDX_DIM, :], HEAD_DIM)
    for h in range(N_HEADS_A):
        qa_ref[h] = _pad_rows(qaT_ref[h * HEAD_DIM:(h + 1) * HEAD_DIM, :], 0)
    krow = lax.broadcasted_iota(jnp.int32, (TK, TQ), 0)
    qcol = lax.broadcasted_iota(jnp.int32, (TK, TQ), 1)
    causal = krow <= qcol

    def keys_blk(kb):
        return keys_ref[pl.ds(pl.multiple_of(kb * TK, TK), TK), :]

    def score_blk(kb):
        keys = keys_blk(kb)
        acc = None
        for h in range(N_IDX_HEADS):
            sh = jnp.dot(keys, qi_ref[h], preferred_element_type=f32)
            term = w[h:h + 1, :] * jnp.maximum(sh, 0.0)
            acc = term if acc is None else acc + term
        return acc

    def far_score(kb, carry):
        mn, mx = carry
        a = score_blk(kb)
        sc_ref[kb] = a
        sc16_ref[kb] = a.astype(bf16)
        return (jnp.minimum(mn, jnp.min(a, axis=0, keepdims=True)),
                jnp.maximum(mx, jnp.max(a, axis=0, keepdims=True)))

    def diag_score(carry):
        mn, mx = carry
        a = score_blk(j)
        a_diag = jnp.where(causal, a, NEG_SCORE)
        sc_ref[j] = a_diag
        sc16_ref[j] = a_diag.astype(bf16)
        return (jnp.minimum(mn, jnp.min(jnp.where(causal, a, BIG_SCORE), axis=0, keepdims=True)),
                jnp.maximum(mx, jnp.max(a_diag, axis=0, keepdims=True)))

    stats = lax.fori_loop(0, j // 2, lambda i, c: far_score(2 * i + 1, far_score(2 * i, c)),
                          (jnp.full((1, TQ), BIG_SCORE, f32), jnp.full((1, TQ), NEG_SCORE, f32)))
    rmin, rmax = lax.cond(j % 2 == 1, lambda c: diag_score(far_score(j - 1, c)), diag_score, stats)
    nvalid = (j * TQ + qcol[0:1, :] + 1).astype(f32)
    lo = _select_topk(sc_ref, sc16_ref, j + 1, rmin, rmax, nvalid, k_top)

    _attn_init(m_ref, acc_ref)

    def block_streams(kb, mask, bias_rows):
        keys, vT, madd = keys_blk(kb), _with_ones(vT_ref[kb]), _mask_add(mask)
        return [Stream(qa_ref[h], keys, vT, madd,
                       None if bias_rows is None else bias_ref[h, bias_rows, :], h) for h in range(N_HEADS_A)]

    def far(kb):
        return block_streams(kb, sc_ref[kb] >= lo, None)

    _far_blocks(jnp.maximum(j - 1, 0),
                lambda k0, k1: _attn_streams(far(k0) + far(k1), m_ref, acc_ref),
                lambda k0: _attn_streams(far(k0), m_ref, acc_ref))
    diag_mask = (sc_ref[j] >= lo) & causal

    @pl.when(j >= 1)
    def _():
        _attn_streams(block_streams(j - 1, sc_ref[j - 1] >= lo, slice(0, TK))
                      + block_streams(j, diag_mask, slice(TK, 2 * TK)), m_ref, acc_ref)

    @pl.when(j == 0)
    def _():
        _attn_streams(block_streams(j, diag_mask, slice(TK, 2 * TK)), m_ref, acc_ref)

    for h in range(N_HEADS_A):
        oT_ref[h * HEAD_DIM:(h + 1) * HEAD_DIM, :] = _attn_out(h, acc_ref)
    o_ref[...] = oT_ref[...].T.astype(o_ref.dtype)


def _attn_scratch(n_slots):
    return [pltpu.VMEM((n_slots, 1, TQ), f32), pltpu.VMEM((n_slots, ACC_ROWS, TQ), f32)]


def _dsa(qiT, miscT, keys, qaT, vT_blocks, biasT, B, S, k_top):
    nq = S // TQ
    nkb = S // TK
    col_spec = lambda n: pl.BlockSpec((n, TQ), lambda b, j: (0, b * nq + j))
    return pl.pallas_call(
        functools.partial(_dsa_kernel, k_top=k_top),
        grid=(B, nq),
        in_specs=[col_spec(W_QI), col_spec(W_MISC),
                  pl.BlockSpec((S, LANES), lambda b, j: (b, 0)),
                  col_spec(W_QA),
                  pl.BlockSpec((None, nkb, HEAD_DIM, TK), lambda b, j: (b, 0, 4, 0)),
                  pl.BlockSpec(biasT.shape, lambda b, j: (0, 0, 0))],
        out_specs=pl.BlockSpec((TQ, W_QA), lambda b, j: (b * nq + j, 0)),
        out_shape=jax.ShapeDtypeStruct((B * S, W_QA), bf16),
        scratch_shapes=[pltpu.VMEM((nkb, TK, TQ), f32),
                        pltpu.VMEM((nkb, TK, TQ), bf16),
                        pltpu.VMEM((N_IDX_HEADS, LANES, TQ), bf16),
                        pltpu.VMEM((N_HEADS_A, LANES, TQ), bf16),
                        pltpu.VMEM((W_QA, TQ), f32)] + _attn_scratch(N_HEADS_A),
        compiler_params=_cparams("parallel", "parallel"),
    )(qiT, miscT, keys, qaT, vT_blocks, biasT)


def _compress_kernel(xk_ref, xv_ref, pe_ref, w1_ref, w1tok_ref, w2bd_ref, w2bdT_ref, ck_ref, cvT_ref):
    nc = ck_ref.shape[0]
    G = N_KV_GROUPS_B

    def hidden(kind):
        x_ref = (xk_ref, xv_ref)[kind]
        const = jnp.dot(pe_ref[kind], w1_ref[kind], preferred_element_type=f32)[0:1]
        ab = jnp.zeros((nc, 2 * G * CMP_HIDDEN), f32)
        for i in range(CMP_STRIDE):
            tok = x_ref[pl.ds(i, nc, stride=CMP_STRIDE), :]
            ab = ab + jnp.dot(tok.astype(bf16), w1tok_ref[kind, i], preferred_element_type=f32)
        hs = []
        for g in range(G):
            first = ab[:, g * CMP_HIDDEN:(g + 1) * CMP_HIDDEN]
            second = pltpu.roll(ab[:, (G + g) * CMP_HIDDEN:(G + g + 1) * CMP_HIDDEN], nc - 1, 0)
            hs.append(jax.nn.gelu(first + second + const).astype(bf16))
        return jnp.concatenate(hs, axis=1)

    ck_ref[...] = jnp.dot(hidden(0), w2bd_ref[...], preferred_element_type=f32).astype(ck_ref.dtype)
    cvT_ref[...] = lax.dot_general(w2bdT_ref[...], hidden(1), NT_DIMS,
                                   preferred_element_type=f32).astype(cvT_ref.dtype)


def _compress(kcv, pe8, w1, w1tok, w2bd_k, w2bdT_v, B, S):
    nc = S // CMP_STRIDE
    cs = lambda a: pl.BlockSpec(a.shape, lambda b: (0,) * a.ndim)
    return pl.pallas_call(
        _compress_kernel,
        grid=(B,),
        in_specs=[pl.BlockSpec((S, LANES), lambda b: (b, 0)),
                  pl.BlockSpec((S, LANES), lambda b: (b, 1)),
                  cs(pe8), cs(w1), cs(w1tok), cs(w2bd_k), cs(w2bdT_v)],
        out_specs=[pl.BlockSpec((None, nc, LANES), lambda b: (b, 0, 0)),
                   pl.BlockSpec((None, LANES, nc), lambda b: (b, 0, 0))],
        out_shape=[jax.ShapeDtypeStruct((B, nc, LANES), bf16), jax.ShapeDtypeStruct((B, LANES, nc), bf16)],
        compiler_params=_cparams("parallel"),
    )(kcv, kcv, pe8, w1, w1tok, w2bd_k, w2bdT_v)


def _nsa_kernel(qbT_ref, miscT_ref, ck_ref, cvT_ref, ksw_ref, vT_ref, bias_ref, bcmp_ref, ovT_ref, e3_ref,
                o_ref, qb_ref, s_ref, oT_ref, m_ref, acc_ref, *, n_sel):
    j = pl.program_id(1)
    nc = ck_ref.shape[0]
    n_s = ovT_ref.shape[0]
    R = HEADS_PER_GROUP
    G = N_KV_GROUPS_B
    H = N_HEADS_B
    for h in range(H):
        qb_ref[h] = _pad_rows(qbT_ref[h * HEAD_DIM:(h + 1) * HEAD_DIM, :], (h // R) * HEAD_DIM)
    krow = lax.broadcasted_iota(jnp.int32, (TK, TQ), 0)
    qcol = lax.broadcasted_iota(jnp.int32, (TK, TQ), 1)
    causal = krow <= qcol
    t_c = j * TQ + lax.broadcasted_iota(jnp.int32, (nc, TQ), 1)
    c_c = lax.broadcasted_iota(jnp.int32, (nc, TQ), 0)
    mask_c = c_c * CMP_STRIDE + (CMP_BLOCK - 1) <= t_c
    blk = lax.broadcasted_iota(jnp.int32, (n_s, TQ), 0)
    cur = (j * TQ + lax.broadcasted_iota(jnp.int32, (n_s, TQ), 1)) // SLC_BLOCK
    valid = blk <= cur
    forced = (blk == 0) | (blk == cur) | (blk == cur - 1)
    gates = jax.nn.sigmoid(miscT_ref[N_IDX_HEADS:N_IDX_HEADS + W_GB, :])
    win_blocks = WINDOW // TK
    cmp_per_q = TQ // CMP_STRIDE
    half = CMP_WIN // 2

    for h in range(H):
        s_ref[h] = jnp.dot(ck_ref[...], qb_ref[h], preferred_element_type=f32)

    @pl.when(j == 0)
    def _():
        for h in range(H):
            s_ref[h, 0:half, :] += bcmp_ref[h, half:CMP_WIN, :]

    @pl.when(j > 0)
    def _():
        rows = pl.ds(pl.multiple_of(j * cmp_per_q - half, half), CMP_WIN)
        for h in range(H):
            s_ref[h, rows, :] += bcmp_ref[h]

    o_c = []
    psum = [jnp.zeros((nc, TQ), f32) for _ in range(G)]
    for h in range(H):
        g = h // R
        s = jnp.where(mask_c, s_ref[h], NEG_MASK)
        m = jnp.max(s, axis=0, keepdims=True)
        e = jnp.where(mask_c, jnp.exp2(s - m), 0.0)
        p = e * (1.0 / jnp.maximum(jnp.sum(e, axis=0, keepdims=True), 1e-30))
        psum[g] = psum[g] + p
        o_c.append(jnp.dot(cvT_ref[g * HEAD_DIM:(g + 1) * HEAD_DIM, :], p.astype(bf16),
                           preferred_element_type=f32))

    selT = []
    for g in range(G):
        p_hi = psum[g].astype(bf16)
        p_lo = (psum[g] - p_hi.astype(f32)).astype(bf16)
        imp = (jnp.dot(ovT_ref[...], p_hi, preferred_element_type=f32)
               + jnp.dot(ovT_ref[...], p_lo, preferred_element_type=f32))
        score = jnp.where(valid, jnp.where(forced, FORCED_SCORE, imp), NEG_SCORE)
        selT.append(_rank_select(score, n_sel).astype(bf16))

    _attn_init(m_ref, acc_ref)

    def branch_streams(kb, lanes, v_row0, masks, bias_rows, slot0):
        keys = ksw_ref[pl.ds(pl.multiple_of(kb * TK, TK), TK), lanes]
        out = []
        for g in range(G):
            vT = _with_ones(vT_ref[kb, v_row0 + g * HEAD_DIM:v_row0 + (g + 1) * HEAD_DIM, :])
            madd = None if masks[g] is None else _mask_add(masks[g])
            for h in range(g * R, (g + 1) * R):
                out.append(Stream(qb_ref[h], keys, vT, madd,
                                  None if bias_rows is None else bias_ref[h, bias_rows, :], slot0 + h))
        return out

    def tok_masks(kb, extra=None):
        ms = [jnp.dot(e3_ref[kb], selT[g], preferred_element_type=f32) > 0.5 for g in range(G)]
        return ms if extra is None else [m & extra for m in ms]

    def sel_streams(kb, extra=None, bias_rows=None):
        return branch_streams(kb, slice(0, LANES), 0, tok_masks(kb, extra), bias_rows, 0)

    def win_streams(kb, mask=None, bias_rows=None):
        return branch_streams(kb, slice(LANES, 2 * LANES), G * HEAD_DIM, [mask] * G, bias_rows, H)

    _far_blocks(jnp.maximum(j - 1, 0),
                lambda k0, k1: _attn_streams(sel_streams(k0) + sel_streams(k1), m_ref, acc_ref),
                lambda k0: _attn_streams(sel_streams(k0), m_ref, acc_ref))

    prev_rows, diag_rows = slice(0, TK), slice(TK, 2 * TK)

    def near_streams(n_before):
        out = []
        for d in range(min(n_before, win_blocks), 1, -1):
            out += win_streams(j - d, (krow > qcol) if d == win_blocks else None)
        if n_before >= 1:
            out += sel_streams(j - 1, None, prev_rows)
            out += win_streams(j - 1, (krow > qcol) if win_blocks == 1 else None, prev_rows)
        return out + sel_streams(j, causal, diag_rows) + win_streams(j, causal, diag_rows)

    for n_before in range(win_blocks + 1):
        @pl.when((j >= n_before) if n_before == win_blocks else (j == n_before))
        def _(n_before=n_before):
            _attn_streams(near_streams(n_before), m_ref, acc_ref)

    for h in range(H):
        oT_ref[h * HEAD_DIM:(h + 1) * HEAD_DIM, :] = (
            gates[h:h + 1, :] * o_c[h]
            + gates[H + h:H + h + 1, :] * _attn_out(h, acc_ref)
            + gates[2 * H + h:2 * H + h + 1, :] * _attn_out(H + h, acc_ref))
    o_ref[...] = oT_ref[...].T.astype(o_ref.dtype)


def _nsa(qbT, miscT, ck, cvT, ksw, vT_blocks, biasT, bcmpT, ovT, e3T, B, S, n_sel):
    nq = S // TQ
    nkb = S // TK
    nc = ck.shape[1]
    col_spec = lambda n: pl.BlockSpec((n, TQ), lambda b, j: (0, b * nq + j))
    const = lambda a: pl.BlockSpec(a.shape, lambda b, j: (0,) * a.ndim)
    return pl.pallas_call(
        functools.partial(_nsa_kernel, n_sel=n_sel),
        grid=(B, nq),
        in_specs=[col_spec(W_QB), col_spec(W_MISC),
                  pl.BlockSpec((None, nc, LANES), lambda b, j: (b, 0, 0)),
                  pl.BlockSpec((None, LANES, nc), lambda b, j: (b, 0, 0)),
                  pl.BlockSpec((S, 2 * LANES), lambda b, j: (b, 0)),
                  pl.BlockSpec((None, nkb, 4 * HEAD_DIM, TK), lambda b, j: (b, 0, 0, 0)),
                  const(biasT), const(bcmpT), const(ovT), const(e3T)],
        out_specs=pl.BlockSpec((TQ, W_QB), lambda b, j: (b * nq + j, 0)),
        out_shape=jax.ShapeDtypeStruct((B * S, W_QB), bf16),
        scratch_shapes=[pltpu.VMEM((N_HEADS_B, LANES, TQ), bf16),
                        pltpu.VMEM((N_HEADS_B, nc, TQ), f32),
                        pltpu.VMEM((W_QB, TQ), f32)] + _attn_scratch(2 * N_HEADS_B),
        compiler_params=_cparams("parallel", "parallel"),
    )(qbT, miscT, ck, cvT, ksw, vT_blocks, biasT, bcmpT, ovT, e3T)


def _mix_kernel(oa_ref, ob_ref, gate_ref, x_ref, wa_ref, wb_ref, wo_ref, x1_ref):
    d = x_ref.shape[-1]
    ga = jax.nn.sigmoid(gate_ref[:, :d].astype(f32))
    gb = jax.nn.sigmoid(gate_ref[:, d:].astype(f32))
    mix = (ga * jnp.dot(oa_ref[...], wa_ref[...], preferred_element_type=f32)
           + gb * jnp.dot(ob_ref[...], wb_ref[...], preferred_element_type=f32))
    x1_ref[...] = x_ref[...] + jnp.dot(mix.astype(bf16), wo_ref[...], preferred_element_type=f32)


def _mix(oa, ob, gate, x2, wa, wb, wo, tm=512):
    rows, d = x2.shape
    rs = lambda w: pl.BlockSpec((tm, w), lambda i: (i, 0))
    cs = lambda a: pl.BlockSpec(a.shape, lambda i: (0, 0))
    return pl.pallas_call(
        _mix_kernel,
        grid=(rows // tm,),
        in_specs=[rs(W_QA), rs(W_QB), rs(2 * d), rs(d), cs(wa), cs(wb), cs(wo)],
        out_specs=rs(d),
        out_shape=jax.ShapeDtypeStruct((rows, d), f32),
        compiler_params=_cparams("parallel"),
    )(oa, ob, gate, x2, wa, wb, wo)


def _mlp_kernel(x1_ref, g_ref, w1_ref, w2_ref, gf_ref, o_ref, *, chunk):
    x1 = x1_ref[...]
    h = _rms(x1, g_ref[...]).astype(bf16)
    acc = jnp.zeros(x1.shape, f32)
    for c0 in range(0, w1_ref.shape[1], chunk):
        u = jnp.dot(h, w1_ref[:, c0:c0 + chunk], preferred_element_type=f32)
        u = jnp.square(jnp.maximum(u, 0.0)).astype(bf16)
        acc = acc + jnp.dot(u, w2_ref[c0:c0 + chunk, :], preferred_element_type=f32)
    o_ref[...] = _rms(x1 + acc, gf_ref[...])


def _mlp(x1, g, w1, w2, gf, tm=256, chunk=512):
    rows, d = x1.shape
    rs = pl.BlockSpec((tm, d), lambda i: (i, 0))
    cs = lambda a: pl.BlockSpec(a.shape, lambda i: (0, 0))
    ws = lambda a: pl.BlockSpec(a.shape, lambda i: (0, 0), pipeline_mode=pl.Buffered(1))
    return pl.pallas_call(
        functools.partial(_mlp_kernel, chunk=chunk),
        grid=(rows // tm,),
        in_specs=[rs, cs(g), ws(w1), ws(w2), cs(gf)],
        out_specs=rs,
        out_shape=jax.ShapeDtypeStruct((rows, d), f32),
        compiler_params=_cparams("parallel"),
    )(x1, g, w1, w2, gf)


def _t5_bucket_np(dist):
    n = np.maximum(dist, 0)
    max_exact = N_BUCKETS // 2
    nf = np.maximum(n, 1).astype(np.float32)
    large = max_exact + (np.log(nf / np.float32(max_exact)) / np.float32(math.log(MAX_DISTANCE / max_exact))
                         * np.float32(N_BUCKETS - max_exact)).astype(np.int32)
    large = np.minimum(large, N_BUCKETS - 1)
    return np.where(n < max_exact, n, large)


def _bias_tables(rel_bias):
    shifted = (rel_bias.astype(f32) - rel_bias[N_BUCKETS - 1].astype(f32)) * LOG2E

    def lookup(dist, tab):
        bucket = np.where((dist >= 0) & (dist < MAX_DISTANCE), _t5_bucket_np(dist), N_BUCKETS - 1)
        onehot = (jnp.asarray(bucket, jnp.int32)[..., None] == jnp.arange(N_BUCKETS)).astype(f32)
        return jnp.einsum('kqb,bh->hkq', onehot, tab, precision=lax.Precision.HIGHEST)

    q = np.arange(TQ)[None, :]
    near = lookup(q + TK - np.arange(2 * TK)[:, None], shifted).astype(bf16)
    m = np.arange(CMP_WIN)[:, None]
    cmp = lookup(q - (CMP_BLOCK - 1) - CMP_STRIDE * (m - CMP_WIN // 2), shifted[:, N_HEADS_A:])
    return near[:N_HEADS_A], near[N_HEADS_A:], cmp


def _static_tables(S, nc):
    n_s = S // SLC_BLOCK
    cmp_start = np.arange(nc)[None, :] * CMP_STRIDE
    slc_start = np.arange(n_s)[:, None] * SLC_BLOCK
    ovT = np.clip(np.minimum(cmp_start + CMP_BLOCK, slc_start + SLC_BLOCK)
                  - np.maximum(cmp_start, slc_start), 0, None).astype(np.float32) / CMP_BLOCK
    nkb = S // TK
    tok_blk = (np.arange(nkb)[:, None, None] * TK + np.arange(TK)[None, :, None]) // SLC_BLOCK
    e3T = (np.arange(LANES)[None, None, :] == tok_blk).astype(np.float32)
    return jnp.asarray(ovT, bf16), jnp.asarray(e3T, bf16)


def _compress_weights(w1, w2):
    z = jnp.zeros((HEAD_DIM, CMP_HIDDEN), w1.dtype)
    first = w1[:CMP_STRIDE * HEAD_DIM].reshape(CMP_STRIDE, HEAD_DIM, CMP_HIDDEN)
    second = w1[CMP_STRIDE * HEAD_DIM:].reshape(CMP_STRIDE, HEAD_DIM, CMP_HIDDEN)
    zz = jnp.broadcast_to(z, first.shape)
    tok = jnp.concatenate([jnp.concatenate([first, zz, second, zz], axis=2),
                           jnp.concatenate([zz, first, zz, second], axis=2)], axis=1)
    z2 = jnp.zeros((CMP_HIDDEN, HEAD_DIM), w2.dtype)
    return tok, jnp.block([[w2, z2], [z2, w2]])


def kernel(x, norm_mix, w_in, cmp_pe_k, cmp_w1_k, cmp_w2_k, cmp_pe_v, cmp_w1_v, cmp_w2_v, rel_bias,
           w_branch_a, w_branch_b, w_out, norm_mlp, w_mlp_in, w_mlp_out, norm_final):
    B, S, D = x.shape
    assert norm_mix.shape[0] == 1 and S % TQ == 0 and (S // CMP_STRIDE) % LANES == 0
    assert WINDOW % TK == 0 and (S // SLC_BLOCK) % SUBLANES == 0 and S // SLC_BLOCK <= LANES
    assert S // COUNT_ROWS <= 256
    G = N_KV_GROUPS_B
    rows = B * S
    nkb = S // TK
    nc = S // CMP_STRIDE
    k_top = min(TOPK_TOKENS, S // 4)
    n_sel = min(N_SLC_BLOCKS, S // SLC_BLOCK)
    x2 = x.reshape(rows, D)

    w = w_in[0]
    o_ka = W_QA
    o_va = o_ka + HEAD_DIM
    o_qi = o_va + HEAD_DIM
    o_ki = o_qi + W_QI
    o_wi = o_ki + IDX_DIM
    o_qb = o_wi + N_IDX_HEADS
    o_kvb = o_qb + W_QB
    o_gb = o_kvb + W_KVB
    o_gate = o_gb + W_GB
    kv = lambda kind: w[:, o_kvb + kind * G * HEAD_DIM:o_kvb + (kind + 1) * G * HEAD_DIM]
    pad = jnp.zeros((D, LANES - HEAD_DIM - IDX_DIM), w.dtype)
    w_rm = jnp.concatenate([w[:, o_ka:o_va], w[:, o_ki:o_wi], pad, kv(0), kv(1), kv(2), kv(4),
                            w[:, o_gate:]], axis=1).astype(bf16)
    w_t = jnp.concatenate([w[:, :o_ka], w[:, o_qi:o_ki], w[:, o_wi:o_qb], w[:, o_gb:o_gate],
                           w[:, o_qb:o_kvb], kv(3), kv(5), w[:, o_va:o_qi]], axis=1).T.astype(bf16)
    keys_a, kcv, ksw, gate, qaT, qiT, miscT, qbT, vT = _inproj(x2, norm_mix[0][None], w_rm, w_t)
    vT_blocks = vT.reshape(B, nkb, W_VT, TK)

    bias_a, bias_b, bcmp = _bias_tables(rel_bias)
    ovT, e3T = _static_tables(S, nc)

    o_a = _dsa(qiT, miscT, keys_a, qaT, vT_blocks, bias_a, B, S, k_top)

    chunk_w = CMP_STRIDE * HEAD_DIM
    pe = jnp.stack([cmp_pe_k[0], cmp_pe_v[0]]).reshape(2, 1, 2 * chunk_w)
    pe8 = jnp.broadcast_to(pe, (2, SUBLANES, 2 * chunk_w)).astype(bf16)
    w1 = jnp.stack([cmp_w1_k[0], cmp_w1_v[0]]).astype(bf16)
    tok_k, w2bd_k = _compress_weights(cmp_w1_k[0], cmp_w2_k[0])
    tok_v, w2bd_v = _compress_weights(cmp_w1_v[0], cmp_w2_v[0])
    ck, cvT = _compress(kcv, pe8, w1, jnp.stack([tok_k, tok_v]).astype(bf16),
                        w2bd_k.astype(bf16), w2bd_v.T.astype(bf16), B, S)

    o_b = _nsa(qbT, miscT, ck, cvT, ksw, vT_blocks, bias_b, bcmp, ovT, e3T, B, S, n_sel)

    x1 = _mix(o_a, o_b, gate, x2, w_branch_a[0].astype(bf16), w_branch_b[0].astype(bf16),
              w_out[0].astype(bf16))
    out = _mlp(x1, norm_mlp[0][None], w_mlp_in[0].astype(bf16), w_mlp_out[0].astype(bf16),
               norm_final[None])
    return out.reshape(B, S, D)
```

```python
import collections
import functools
import math

import numpy as np
import jax
import jax.numpy as jnp
from jax import lax
from jax.experimental import pallas as pl
from jax.experimental.pallas import tpu as pltpu

f32 = jnp.float32
bf16 = jnp.bfloat16

HEAD_DIM = 64
N_HEADS_A = 8
N_IDX_HEADS = 8
IDX_DIM = 32
TOPK_TOKENS = 256
N_HEADS_B = 8
N_KV_GROUPS_B = 2
HEADS_PER_GROUP = N_HEADS_B // N_KV_GROUPS_B
CMP_BLOCK = 32
CMP_STRIDE = 16
CMP_HIDDEN = 128
SLC_BLOCK = 64
N_SLC_BLOCKS = 16
WINDOW = 512
FORCED_SCORE = 1.0e4
N_BUCKETS = 32
MAX_DISTANCE = 128
EPS = 1e-6

W_QA = N_HEADS_A * HEAD_DIM
W_QI = N_IDX_HEADS * IDX_DIM
W_QB = N_HEADS_B * HEAD_DIM
W_KVB = 6 * N_KV_GROUPS_B * HEAD_DIM
W_GB = 3 * N_HEADS_B
W_MISC = N_IDX_HEADS + W_GB
W_VT = 5 * HEAD_DIM

LANES = 128
SUBLANES = 8
BF16_ROWS = 16
TQ = 256
TK = 256
QK_LOOKAHEAD = 3
PV_DELAY = 2
ACC_ROWS = HEAD_DIM + BF16_ROWS
CMP_WIN = 32
LOG2E = 1.4426950408889634
NEG_MASK = -2.0 ** 101
M_INIT = -2.0 ** 100
NEG_SCORE = -3.0e38
BIG_SCORE = 3.0e38
BIG_SCORE16 = 2.0 ** 127
COARSE_ITERS = 12
FINE_FIXED_ITERS = 9
MAX_SEARCH_ITERS = 400
TIE_CHECK_START = 14
COUNT_ROWS = 32
VMEM_LIMIT = 56 * 1024 * 1024

NT_DIMS = (((1,), (1,)), ((), ()))


def _cparams(*sem):
    return pltpu.CompilerParams(dimension_semantics=sem, vmem_limit_bytes=VMEM_LIMIT)


def _rms(x, g):
    return x * lax.rsqrt(jnp.mean(x * x, axis=-1, keepdims=True) + EPS) * g


def _inproj_kernel(x_ref, g_ref, w_ref, wT_ref, keys_ref, kcv_ref, ksw_ref, gate_ref,
                   qaT_ref, qiT_ref, miscT_ref, qbT_ref, vT_ref):
    h = _rms(x_ref[...], g_ref[...]).astype(bf16)
    q_scale = HEAD_DIM ** -0.5 * LOG2E
    off = 0
    for ref in (keys_ref, kcv_ref, ksw_ref, gate_ref):
        width = ref.shape[-1]
        for c0 in range(0, width, 512):
            c1 = min(c0 + 512, width)
            y = jnp.dot(h, w_ref[:, off + c0:off + c1], preferred_element_type=f32)
            ref[:, c0:c1] = y.astype(ref.dtype)
        off += width
    off = 0
    for ref, scale in ((qaT_ref, q_scale), (qiT_ref, None), (miscT_ref, None), (qbT_ref, q_scale),
                       (vT_ref, None)):
        n = ref.shape[0]
        y = lax.dot_general(wT_ref[off:off + n, :], h, NT_DIMS, preferred_element_type=f32)
        if scale is not None:
            y = y * scale
        ref[...] = y.astype(ref.dtype)
        off += n


def _inproj(x2, g, w_rm, w_t):
    rows, d = x2.shape
    tm = TK
    rm = ((LANES, bf16), (2 * LANES, f32), (2 * LANES, bf16), (2 * d, bf16))
    tr = ((W_QA, bf16), (W_QI, bf16), (W_MISC, f32), (W_QB, bf16))
    return pl.pallas_call(
        _inproj_kernel,
        grid=(rows // tm,),
        in_specs=[pl.BlockSpec((tm, d), lambda i: (i, 0)),
                  pl.BlockSpec((1, d), lambda i: (0, 0)),
                  pl.BlockSpec(w_rm.shape, lambda i: (0, 0)),
                  pl.BlockSpec(w_t.shape, lambda i: (0, 0))],
        out_specs=([pl.BlockSpec((tm, w), lambda i: (i, 0)) for w, _ in rm]
                   + [pl.BlockSpec((n, tm), lambda i: (0, i)) for n, _ in tr]
                   + [pl.BlockSpec((None, W_VT, tm), lambda i: (i, 0, 0))]),
        out_shape=([jax.ShapeDtypeStruct((rows, w), dt) for w, dt in rm]
                   + [jax.ShapeDtypeStruct((n, rows), dt) for n, dt in tr]
                   + [jax.ShapeDtypeStruct((rows // tm, W_VT, tm), bf16)]),
        compiler_params=_cparams("parallel"),
    )(x2, g, w_rm, w_t)


def _count_ge(sc_ref, nkb, p):
    def body(kb, acc):
        hit = jnp.where(sc_ref[kb] >= p, 1.0, 0.0)
        return acc + jnp.sum(hit.reshape(TK // COUNT_ROWS, COUNT_ROWS, TQ), axis=0)
    acc = lax.fori_loop(0, nkb, body, jnp.zeros((COUNT_ROWS, TQ), f32))
    return jnp.sum(acc, axis=0, keepdims=True)


def _count_ge16(sc16_ref, nkb, p16):
    one, nil = jnp.ones((), bf16), jnp.zeros((), bf16)
    def body(kb, acc):
        hit = jnp.where(sc16_ref[kb] >= p16, one, nil).reshape(TK // COUNT_ROWS, COUNT_ROWS, TQ)
        for i in range(TK // COUNT_ROWS):
            acc = acc + hit[i]
        return acc
    acc = lax.fori_loop(0, nkb, body, jnp.zeros((COUNT_ROWS, TQ), bf16))
    return jnp.sum(acc.astype(f32), axis=0, keepdims=True)


def _coarse_bracket(sc16_ref, nkb, rmin, rmax, active, kf):
    rmax16 = rmax.astype(bf16).astype(f32)

    def pivot(lo16, hi16):
        mid = ((lo16 + jnp.minimum(hi16, rmax16)) * 0.5).astype(bf16).astype(f32)
        return mid, active & (mid > lo16) & (mid < hi16)

    def body(_, st):
        lo16, hi16 = st
        mid, ok = pivot(lo16, hi16)
        ge = _count_ge16(sc16_ref, nkb, mid.astype(bf16)) >= kf
        return jnp.where(ok & ge, mid, lo16), jnp.where(ok & jnp.logical_not(ge), mid, hi16)

    lo16, hi16 = lax.fori_loop(
        0, COARSE_ITERS, body, (rmin.astype(bf16).astype(f32), jnp.full_like(rmin, BIG_SCORE16)))
    below = jnp.maximum(jnp.abs(lo16) * 2.0 ** -7, 1e-37)
    return lo16 - below, jnp.where(hi16 < BIG_SCORE16, hi16, BIG_SCORE)


def _select_topk(sc_ref, sc16_ref, nkb, rmin, rmax, nvalid, k):
    kf = float(k)
    active = nvalid > kf
    lo_c, hi_c = _coarse_bracket(sc16_ref, nkb, rmin, rmax, active, kf)
    lo0 = jnp.where(active, lo_c, NEG_SCORE)
    cl0 = jnp.where(active, nvalid, kf)
    hi0 = jnp.where(active, hi_c, BIG_SCORE)
    zero = jnp.zeros_like(lo0)

    def pending(cl, tie):
        return (cl != kf) & (tie == 0.0)

    def any_pending(st):
        return jnp.max(jnp.where(pending(st[3], st[5]), 1.0, 0.0)) > 0.0

    def bisect(st):
        it, lo, hi, cl, ch, tie = st
        pend = pending(cl, tie)
        hie = jnp.minimum(hi, rmax)
        p = lo + (hie - lo) * 0.5
        p = jnp.where(p > lo, p, hie)
        cnt = _count_ge(sc_ref, nkb, p)
        ge = cnt >= kf
        up_lo = pend & ge
        up_hi = pend & jnp.logical_not(ge)
        lo = jnp.where(up_lo, p, lo)
        cl = jnp.where(up_lo, cnt, cl)
        hi = jnp.where(up_hi, p, hi)
        ch = jnp.where(up_hi, cnt, ch)
        return it + 1, lo, hi, cl, ch, tie

    def tie_check(st):
        it, lo, hi, cl, ch, tie = st
        def scan(kb, c):
            dmin, dmax = c
            s = sc_ref[kb]
            dmin = jnp.minimum(dmin, jnp.min(jnp.where(s >= lo, s, BIG_SCORE), axis=0, keepdims=True))
            dmax = jnp.maximum(dmax, jnp.max(jnp.where(s < hi, s, NEG_SCORE), axis=0, keepdims=True))
            return dmin, dmax
        dmin, dmax = lax.fori_loop(
            0, nkb, scan, (jnp.full((1, TQ), BIG_SCORE, f32), jnp.full((1, TQ), NEG_SCORE, f32)))
        new_tie = pending(cl, tie) & (dmin == dmax)
        return it, jnp.where(new_tie, dmin, lo), hi, cl, ch, jnp.where(new_tie, 1.0, tie)

    st = (jnp.int32(0), lo0, hi0, cl0, zero, zero)
    st = lax.fori_loop(0, FINE_FIXED_ITERS, lambda _, st: bisect(st), st)
    st = lax.while_loop(lambda st: any_pending(st) & (st[0] < TIE_CHECK_START), bisect, st)
    st = lax.while_loop(lambda st: any_pending(st) & (st[0] < MAX_SEARCH_ITERS),
                        lambda st: bisect(tie_check(st)), st)
    _, lo, hi, cl, ch, tie = st
    any_tie = jnp.max(tie) > 0.0

    @pl.when(any_tie)
    def _():
        r = lax.broadcasted_iota(jnp.int32, (TK, TK), 0)
        c = lax.broadcasted_iota(jnp.int32, (TK, TK), 1)
        tril = jnp.where(c <= r, 1.0, 0.0).astype(bf16)
        above = lax.fori_loop(
            0, nkb, lambda kb, n: n + jnp.sum(jnp.where(sc_ref[kb] > lo, 1.0, 0.0), axis=0, keepdims=True), zero)
        need = kf - above

        def body(kb, run):
            s = sc_ref[kb]
            eq = s == lo
            eqf = jnp.where(eq, 1.0, 0.0)
            cum = run + jnp.dot(tril, eqf.astype(bf16), preferred_element_type=f32)
            sel_tie = jnp.where((s > lo) | (eq & (cum <= need)), 1.0, -1.0)
            sel_all = jnp.where(s >= lo, 1.0, -1.0)
            sc_ref[kb] = jnp.where(tie > 0.0, sel_tie, sel_all)
            return run + jnp.sum(eqf, axis=0, keepdims=True)

        lax.fori_loop(0, nkb, body, zero)

    return jnp.where(any_tie, 0.0, lo)


def _rank_select(score, k):
    n, width = score.shape
    tiles = [score[v * SUBLANES:(v + 1) * SUBLANES] for v in range(n // SUBLANES)]
    ridx = lax.broadcasted_iota(jnp.int32, (SUBLANES, width), 0)
    ranks = [jnp.zeros((SUBLANES, width), f32) for _ in tiles]
    for i in range(n):
        si = jnp.broadcast_to(tiles[i // SUBLANES][i % SUBLANES:i % SUBLANES + 1], (SUBLANES, width))
        for v, x in enumerate(tiles):
            if v * SUBLANES > i:
                inc = jnp.where(si >= x, 1.0, 0.0)
            elif (v + 1) * SUBLANES <= i:
                inc = jnp.where(si > x, 1.0, 0.0)
            else:
                inc = jnp.where(ridx > i % SUBLANES, jnp.where(si >= x, 1.0, 0.0), jnp.where(si > x, 1.0, 0.0))
            ranks[v] = ranks[v] + inc
    flags = [jnp.where(r < float(k), 1.0, 0.0) for r in ranks]
    flags.append(jnp.zeros((LANES - n, width), f32))
    return jnp.concatenate(flags, axis=0)


Stream = collections.namedtuple("Stream", "q keys vT madd bias slot")


def _attn_init(m_ref, acc_ref):
    m_ref[...] = jnp.full(m_ref.shape, M_INIT, f32)
    acc_ref[...] = jnp.zeros(acc_ref.shape, f32)


def _attn_streams(streams, m_ref, acc_ref):
    def logits(i):
        st = streams[i]
        s = jnp.dot(st.keys, st.q, preferred_element_type=f32).astype(bf16)
        if st.bias is not None:
            s = s + st.bias
        if st.madd is not None:
            s = s + st.madd
        return s

    def values(i, alpha, p):
        st = streams[i]
        acc_ref[st.slot] = alpha * acc_ref[st.slot] + jnp.dot(st.vT, p, preferred_element_type=f32)

    n = len(streams)
    pending = {i: logits(i) for i in range(min(QK_LOOKAHEAD, n))}
    deferred = []
    for i in range(n):
        if i + QK_LOOKAHEAD < n:
            pending[i + QK_LOOKAHEAD] = logits(i + QK_LOOKAHEAD)
        s = pending.pop(i)
        slot = streams[i].slot
        m_old = m_ref[slot]
        m_new = jnp.maximum(m_old, jnp.max(s, axis=0, keepdims=True).astype(f32))
        alpha = jnp.exp2(m_old - m_new)
        p = jnp.exp2(s - m_new.astype(bf16))
        m_ref[slot] = m_new
        deferred.append((i, alpha, p))
        if len(deferred) > PV_DELAY:
            values(*deferred.pop(0))
    for d in deferred:
        values(*d)


def _attn_out(slot, acc_ref):
    acc = acc_ref[slot]
    return acc[0:HEAD_DIM] * (1.0 / jnp.maximum(acc[HEAD_DIM:HEAD_DIM + 1], 1e-30))


def _with_ones(vT):
    return jnp.concatenate([vT, jnp.ones((BF16_ROWS, vT.shape[1]), vT.dtype)], axis=0)


def _mask_add(mask):
    return jnp.where(mask, 0.0, NEG_MASK).astype(bf16)


def _pad_rows(x, start):
    parts = []
    if start:
        parts.append(jnp.zeros((start, x.shape[1]), x.dtype))
    parts.append(x)
    if LANES - start - x.shape[0]:
        parts.append(jnp.zeros((LANES - start - x.shape[0], x.shape[1]), x.dtype))
    return jnp.concatenate(parts, axis=0)


def _far_blocks(n_far, pair_fn, single_fn):
    def body(i, _):
        pair_fn(2 * i, 2 * i + 1)
        return 0
    lax.fori_loop(0, n_far // 2, body, 0)

    @pl.when(n_far % 2 == 1)
    def _():
        single_fn(n_far - 1)


def _dsa_kernel(qiT_ref, miscT_ref, keys_ref, qaT_ref, vT_ref, bias_ref, o_ref,
                sc_ref, sc16_ref, qi_ref, qa_ref, oT_ref, m_ref, acc_ref, *, k_top):
    j = pl.program_id(1)
    w_scale = (N_IDX_HEADS ** -0.5) * (IDX_DIM ** -0.5)
    w = miscT_ref[0:N_IDX_HEADS, :] * w_scale
    for h in range(N_IDX_HEADS):
        qi_ref[h] = _pad_rows(qiT_ref[h * IDX_DIM:(h + 1) * IDX_DIM, :], HEAD_DIM)
    for h in range(N_HEADS_A):
        qa_ref[h] = _pad_rows(qaT_ref[h * HEAD_DIM:(h + 1) * HEAD_DIM, :], 0)
    krow = lax.broadcasted_iota(jnp.int32, (TK, TQ), 0)
    qcol = lax.broadcasted_iota(jnp.int32, (TK, TQ), 1)
    causal = krow <= qcol

    def keys_blk(kb):
        return keys_ref[pl.ds(pl.multiple_of(kb * TK, TK), TK), :]

    def score_blk(kb):
        keys = keys_blk(kb)
        acc = None
        for h in range(N_IDX_HEADS):
            sh = jnp.dot(keys, qi_ref[h], preferred_element_type=f32)
            term = w[h:h + 1, :] * jnp.maximum(sh, 0.0)
            acc = term if acc is None else acc + term
        return acc

    def far_score(kb, carry):
        mn, mx = carry
        a = score_blk(kb)
        sc_ref[kb] = a
        sc16_ref[kb] = a.astype(bf16)
        return (jnp.minimum(mn, jnp.min(a, axis=0, keepdims=True)),
                jnp.maximum(mx, jnp.max(a, axis=0, keepdims=True)))

    def diag_score(carry):
        mn, mx = carry
        a = score_blk(j)
        a_diag = jnp.where(causal, a, NEG_SCORE)
        sc_ref[j] = a_diag
        sc16_ref[j] = a_diag.astype(bf16)
        return (jnp.minimum(mn, jnp.min(jnp.where(causal, a, BIG_SCORE), axis=0, keepdims=True)),
                jnp.maximum(mx, jnp.max(a_diag, axis=0, keepdims=True)))

    stats = lax.fori_loop(0, j // 2, lambda i, c: far_score(2 * i + 1, far_score(2 * i, c)),
                          (jnp.full((1, TQ), BIG_SCORE, f32), jnp.full((1, TQ), NEG_SCORE, f32)))
    rmin, rmax = lax.cond(j % 2 == 1, lambda c: diag_score(far_score(j - 1, c)), diag_score, stats)
    nvalid = (j * TQ + qcol[0:1, :] + 1).astype(f32)
    lo = _select_topk(sc_ref, sc16_ref, j + 1, rmin, rmax, nvalid, k_top)

    _attn_init(m_ref, acc_ref)

    def block_streams(kb, mask, bias_rows):
        keys, vT, madd = keys_blk(kb), _with_ones(vT_ref[kb]), _mask_add(mask)
        return [Stream(qa_ref[h], keys, vT, madd,
                       None if bias_rows is None else bias_ref[h, bias_rows, :], h) for h in range(N_HEADS_A)]

    def far(kb):
        return block_streams(kb, sc_ref[kb] >= lo, None)

    _far_blocks(jnp.maximum(j - 1, 0),
                lambda k0, k1: _attn_streams(far(k0) + far(k1), m_ref, acc_ref),
                lambda k0: _attn_streams(far(k0), m_ref, acc_ref))
    diag_mask = (sc_ref[j] >= lo) & causal

    @pl.when(j >= 1)
    def _():
        _attn_streams(block_streams(j - 1, sc_ref[j - 1] >= lo, slice(0, TK))
                      + block_streams(j, diag_mask, slice(TK, 2 * TK)), m_ref, acc_ref)

    @pl.when(j == 0)
    def _():
        _attn_streams(block_streams(j, diag_mask, slice(TK, 2 * TK)), m_ref, acc_ref)

    for h in range(N_HEADS_A):
        oT_ref[h * HEAD_DIM:(h + 1) * HEAD_DIM, :] = _attn_out(h, acc_ref)
    o_ref[...] = oT_ref[...].T.astype(o_ref.dtype)


def _attn_scratch(n_slots):
    return [pltpu.VMEM((n_slots, 1, TQ), f32), pltpu.VMEM((n_slots, ACC_ROWS, TQ), f32)]


def _dsa(qiT, miscT, keys, qaT, vT_blocks, biasT, B, S, k_top):
    nq = S // TQ
    nkb = S // TK
    col_spec = lambda n: pl.BlockSpec((n, TQ), lambda b, j: (0, b * nq + j))
    return pl.pallas_call(
        functools.partial(_dsa_kernel, k_top=k_top),
        grid=(B, nq),
        in_specs=[col_spec(W_QI), col_spec(W_MISC),
                  pl.BlockSpec((S, LANES), lambda b, j: (b, 0)),
                  col_spec(W_QA),
                  pl.BlockSpec((None, nkb, HEAD_DIM, TK), lambda b, j: (b, 0, 4, 0)),
                  pl.BlockSpec(biasT.shape, lambda b, j: (0, 0, 0))],
        out_specs=pl.BlockSpec((TQ, W_QA), lambda b, j: (b * nq + j, 0)),
        out_shape=jax.ShapeDtypeStruct((B * S, W_QA), bf16),
        scratch_shapes=[pltpu.VMEM((nkb, TK, TQ), f32),
                        pltpu.VMEM((nkb, TK, TQ), bf16),
                        pltpu.VMEM((N_IDX_HEADS, LANES, TQ), bf16),
                        pltpu.VMEM((N_HEADS_A, LANES, TQ), bf16),
                        pltpu.VMEM((W_QA, TQ), f32)] + _attn_scratch(N_HEADS_A),
        compiler_params=_cparams("parallel", "parallel"),
    )(qiT, miscT, keys, qaT, vT_blocks, biasT)


def _compress_kernel(xk_ref, xv_ref, pe_ref, w1_ref, w1tok_ref, w2bd_ref, w2bdT_ref, ck_ref, cvT_ref):
    nc = ck_ref.shape[0]
    G = N_KV_GROUPS_B

    def hidden(kind):
        x_ref = (xk_ref, xv_ref)[kind]
        const = jnp.dot(pe_ref[kind], w1_ref[kind], preferred_element_type=f32)[0:1]
        ab = jnp.zeros((nc, 2 * G * CMP_HIDDEN), f32)
        for i in range(CMP_STRIDE):
            tok = x_ref[pl.ds(i, nc, stride=CMP_STRIDE), :]
            ab = ab + jnp.dot(tok.astype(bf16), w1tok_ref[kind, i], preferred_element_type=f32)
        hs = []
        for g in range(G):
            first = ab[:, g * CMP_HIDDEN:(g + 1) * CMP_HIDDEN]
            second = pltpu.roll(ab[:, (G + g) * CMP_HIDDEN:(G + g + 1) * CMP_HIDDEN], nc - 1, 0)
            hs.append(jax.nn.gelu(first + second + const).astype(bf16))
        return jnp.concatenate(hs, axis=1)

    ck_ref[...] = jnp.dot(hidden(0), w2bd_ref[...], preferred_element_type=f32).astype(ck_ref.dtype)
    cvT_ref[...] = lax.dot_general(w2bdT_ref[...], hidden(1), NT_DIMS,
                                   preferred_element_type=f32).astype(cvT_ref.dtype)


def _compress(kcv, pe8, w1, w1tok, w2bd_k, w2bdT_v, B, S):
    nc = S // CMP_STRIDE
    cs = lambda a: pl.BlockSpec(a.shape, lambda b: (0,) * a.ndim)
    return pl.pallas_call(
        _compress_kernel,
        grid=(B,),
        in_specs=[pl.BlockSpec((S, LANES), lambda b: (b, 0)),
                  pl.BlockSpec((S, LANES), lambda b: (b, 1)),
                  cs(pe8), cs(w1), cs(w1tok), cs(w2bd_k), cs(w2bdT_v)],
        out_specs=[pl.BlockSpec((None, nc, LANES), lambda b: (b, 0, 0)),
                   pl.BlockSpec((None, LANES, nc), lambda b: (b, 0, 0))],
        out_shape=[jax.ShapeDtypeStruct((B, nc, LANES), bf16), jax.ShapeDtypeStruct((B, LANES, nc), bf16)],
        compiler_params=_cparams("parallel"),
    )(kcv, kcv, pe8, w1, w1tok, w2bd_k, w2bdT_v)


def _nsa_kernel(qbT_ref, miscT_ref, ck_ref, cvT_ref, ksw_ref, vT_ref, bias_ref, bcmp_ref, ovT_ref, e3_ref,
                o_ref, qb_ref, s_ref, oT_ref, m_ref, acc_ref, *, n_sel):
    j = pl.program_id(1)
    nc = ck_ref.shape[0]
    n_s = ovT_ref.shape[0]
    R = HEADS_PER_GROUP
    G = N_KV_GROUPS_B
    H = N_HEADS_B
    for h in range(H):
        qb_ref[h] = _pad_rows(qbT_ref[h * HEAD_DIM:(h + 1) * HEAD_DIM, :], (h // R) * HEAD_DIM)
    krow = lax.broadcasted_iota(jnp.int32, (TK, TQ), 0)
    qcol = lax.broadcasted_iota(jnp.int32, (TK, TQ), 1)
    causal = krow <= qcol
    t_c = j * TQ + lax.broadcasted_iota(jnp.int32, (nc, TQ), 1)
    c_c = lax.broadcasted_iota(jnp.int32, (nc, TQ), 0)
    mask_c = c_c * CMP_STRIDE + (CMP_BLOCK - 1) <= t_c
    blk = lax.broadcasted_iota(jnp.int32, (n_s, TQ), 0)
    cur = (j * TQ + lax.broadcasted_iota(jnp.int32, (n_s, TQ), 1)) // SLC_BLOCK
    valid = blk <= cur
    forced = (blk == 0) | (blk == cur) | (blk == cur - 1)
    gates = jax.nn.sigmoid(miscT_ref[N_IDX_HEADS:N_IDX_HEADS + W_GB, :])
    win_blocks = WINDOW // TK
    cmp_per_q = TQ // CMP_STRIDE
    half = CMP_WIN // 2

    for h in range(H):
        s_ref[h] = jnp.dot(ck_ref[...], qb_ref[h], preferred_element_type=f32)

    @pl.when(j == 0)
    def _():
        for h in range(H):
            s_ref[h, 0:half, :] += bcmp_ref[h, half:CMP_WIN, :]

    @pl.when(j > 0)
    def _():
        rows = pl.ds(pl.multiple_of(j * cmp_per_q - half, half), CMP_WIN)
        for h in range(H):
            s_ref[h, rows, :] += bcmp_ref[h]

    o_c = []
    psum = [jnp.zeros((nc, TQ), f32) for _ in range(G)]
    for h in range(H):
        g = h // R
        s = jnp.where(mask_c, s_ref[h], NEG_MASK)
        m = jnp.max(s, axis=0, keepdims=True)
        e = jnp.where(mask_c, jnp.exp2(s - m), 0.0)
        p = e * (1.0 / jnp.maximum(jnp.sum(e, axis=0, keepdims=True), 1e-30))
        psum[g] = psum[g] + p
        o_c.append(jnp.dot(cvT_ref[g * HEAD_DIM:(g + 1) * HEAD_DIM, :], p.astype(bf16),
                           preferred_element_type=f32))

    selT = []
    for g in range(G):
        p_hi = psum[g].astype(bf16)
        p_lo = (psum[g] - p_hi.astype(f32)).astype(bf16)
        imp = (jnp.dot(ovT_ref[...], p_hi, preferred_element_type=f32)
               + jnp.dot(ovT_ref[...], p_lo, preferred_element_type=f32))
        score = jnp.where(valid, jnp.where(forced, FORCED_SCORE, imp), NEG_SCORE)
        selT.append(_rank_select(score, n_sel).astype(bf16))

    _attn_init(m_ref, acc_ref)

    def branch_streams(kb, lanes, v_row0, masks, bias_rows, slot0):
        keys = ksw_ref[pl.ds(pl.multiple_of(kb * TK, TK), TK), lanes]
        out = []
        for g in range(G):
            vT = _with_ones(vT_ref[kb, v_row0 + g * HEAD_DIM:v_row0 + (g + 1) * HEAD_DIM, :])
            madd = None if masks[g] is None else _mask_add(masks[g])
            for h in range(g * R, (g + 1) * R):
                out.append(Stream(qb_ref[h], keys, vT, madd,
                                  None if bias_rows is None else bias_ref[h, bias_rows, :], slot0 + h))
        return out

    def tok_masks(kb, extra=None):
        ms = [jnp.dot(e3_ref[kb], selT[g], preferred_element_type=f32) > 0.5 for g in range(G)]
        return ms if extra is None else [m & extra for m in ms]

    def sel_streams(kb, extra=None, bias_rows=None):
        return branch_streams(kb, slice(0, LANES), 0, tok_masks(kb, extra), bias_rows, 0)

    def win_streams(kb, mask=None, bias_rows=None):
        return branch_streams(kb, slice(LANES, 2 * LANES), G * HEAD_DIM, [mask] * G, bias_rows, H)

    _far_blocks(jnp.maximum(j - 1, 0),
                lambda k0, k1: _attn_streams(sel_streams(k0) + sel_streams(k1), m_ref, acc_ref),
                lambda k0: _attn_streams(sel_streams(k0), m_ref, acc_ref))

    prev_rows, diag_rows = slice(0, TK), slice(TK, 2 * TK)

    def near_streams(n_before):
        out = []
        for d in range(min(n_before, win_blocks), 1, -1):
            out += win_streams(j - d, (krow > qcol) if d == win_blocks else None)
        if n_before >= 1:
            out += sel_streams(j - 1, None, prev_rows)
            out += win_streams(j - 1, (krow > qcol) if win_blocks == 1 else None, prev_rows)
        return out + sel_streams(j, causal, diag_rows) + win_streams(j, causal, diag_rows)

    for n_before in range(win_blocks + 1):
        @pl.when((j >= n_before) if n_before == win_blocks else (j == n_before))
        def _(n_before=n_before):
            _attn_streams(near_streams(n_before), m_ref, acc_ref)

    for h in range(H):
        oT_ref[h * HEAD_DIM:(h + 1) * HEAD_DIM, :] = (
            gates[h:h + 1, :] * o_c[h]
            + gates[H + h:H + h + 1, :] * _attn_out(h, acc_ref)
            + gates[2 * H + h:2 * H + h + 1, :] * _attn_out(H + h, acc_ref))
    o_ref[...] = oT_ref[...].T.astype(o_ref.dtype)


def _nsa(qbT, miscT, ck, cvT, ksw, vT_blocks, biasT, bcmpT, ovT, e3T, B, S, n_sel):
    nq = S // TQ
    nkb = S // TK
    nc = ck.shape[1]
    col_spec = lambda n: pl.BlockSpec((n, TQ), lambda b, j: (0, b * nq + j))
    const = lambda a: pl.BlockSpec(a.shape, lambda b, j: (0,) * a.ndim)
    return pl.pallas_call(
        functools.partial(_nsa_kernel, n_sel=n_sel),
        grid=(B, nq),
        in_specs=[col_spec(W_QB), col_spec(W_MISC),
                  pl.BlockSpec((None, nc, LANES), lambda b, j: (b, 0, 0)),
                  pl.BlockSpec((None, LANES, nc), lambda b, j: (b, 0, 0)),
                  pl.BlockSpec((S, 2 * LANES), lambda b, j: (b, 0)),
                  pl.BlockSpec((None, nkb, 4 * HEAD_DIM, TK), lambda b, j: (b, 0, 0, 0)),
                  const(biasT), const(bcmpT), const(ovT), const(e3T)],
        out_specs=pl.BlockSpec((TQ, W_QB), lambda b, j: (b * nq + j, 0)),
        out_shape=jax.ShapeDtypeStruct((B * S, W_QB), bf16),
        scratch_shapes=[pltpu.VMEM((N_HEADS_B, LANES, TQ), bf16),
                        pltpu.VMEM((N_HEADS_B, nc, TQ), f32),
                        pltpu.VMEM((W_QB, TQ), f32)] + _attn_scratch(2 * N_HEADS_B),
        compiler_params=_cparams("parallel", "parallel"),
    )(qbT, miscT, ck, cvT, ksw, vT_blocks, biasT, bcmpT, ovT, e3T)


def _mix_kernel(oa_ref, ob_ref, gate_ref, x_ref, wa_ref, wb_ref, wo_ref, x1_ref):
    d = x_ref.shape[-1]
    ga = jax.nn.sigmoid(gate_ref[:, :d].astype(f32))
    gb = jax.nn.sigmoid(gate_ref[:, d:].astype(f32))
    mix = (ga * jnp.dot(oa_ref[...], wa_ref[...], preferred_element_type=f32)
           + gb * jnp.dot(ob_ref[...], wb_ref[...], preferred_element_type=f32))
    x1_ref[...] = x_ref[...] + jnp.dot(mix.astype(bf16), wo_ref[...], preferred_element_type=f32)


def _mix(oa, ob, gate, x2, wa, wb, wo, tm=512):
    rows, d = x2.shape
    rs = lambda w: pl.BlockSpec((tm, w), lambda i: (i, 0))
    cs = lambda a: pl.BlockSpec(a.shape, lambda i: (0, 0))
    return pl.pallas_call(
        _mix_kernel,
        grid=(rows // tm,),
        in_specs=[rs(W_QA), rs(W_QB), rs(2 * d), rs(d), cs(wa), cs(wb), cs(wo)],
        out_specs=rs(d),
        out_shape=jax.ShapeDtypeStruct((rows, d), f32),
        compiler_params=_cparams("parallel"),
    )(oa, ob, gate, x2, wa, wb, wo)


def _mlp_kernel(x1_ref, g_ref, w1_ref, w2_ref, gf_ref, o_ref, *, chunk):
    x1 = x1_ref[...]
    h = _rms(x1, g_ref[...]).astype(bf16)
    acc = jnp.zeros(x1.shape, f32)
    for c0 in range(0, w1_ref.shape[1], chunk):
        u = jnp.dot(h, w1_ref[:, c0:c0 + chunk], preferred_element_type=f32)
        u = jnp.square(jnp.maximum(u, 0.0)).astype(bf16)
        acc = acc + jnp.dot(u, w2_ref[c0:c0 + chunk, :], preferred_element_type=f32)
    o_ref[...] = _rms(x1 + acc, gf_ref[...])


def _mlp(x1, g, w1, w2, gf, tm=256, chunk=512):
    rows, d = x1.shape
    rs = pl.BlockSpec((tm, d), lambda i: (i, 0))
    cs = lambda a: pl.BlockSpec(a.shape, lambda i: (0, 0))
    ws = lambda a: pl.BlockSpec(a.shape, lambda i: (0, 0), pipeline_mode=pl.Buffered(1))
    return pl.pallas_call(
        functools.partial(_mlp_kernel, chunk=chunk),
        grid=(rows // tm,),
        in_specs=[rs, cs(g), ws(w1), ws(w2), cs(gf)],
        out_specs=rs,
        out_shape=jax.ShapeDtypeStruct((rows, d), f32),
        compiler_params=_cparams("parallel"),
    )(x1, g, w1, w2, gf)


def _t5_bucket_np(dist):
    n = np.maximum(dist, 0)
    max_exact = N_BUCKETS // 2
    nf = np.maximum(n, 1).astype(np.float32)
    large = max_exact + (np.log(nf / np.float32(max_exact)) / np.float32(math.log(MAX_DISTANCE / max_exact))
                         * np.float32(N_BUCKETS - max_exact)).astype(np.int32)
    large = np.minimum(large, N_BUCKETS - 1)
    return np.where(n < max_exact, n, large)


def _bias_tables(rel_bias):
    shifted = (rel_bias.astype(f32) - rel_bias[N_BUCKETS - 1].astype(f32)) * LOG2E

    def lookup(dist, tab):
        bucket = np.where((dist >= 0) & (dist < MAX_DISTANCE), _t5_bucket_np(dist), N_BUCKETS - 1)
        onehot = (jnp.asarray(bucket, jnp.int32)[..., None] == jnp.arange(N_BUCKETS)).astype(f32)
        return jnp.einsum('kqb,bh->hkq', onehot, tab, precision=lax.Precision.HIGHEST)

    q = np.arange(TQ)[None, :]
    near = lookup(q + TK - np.arange(2 * TK)[:, None], shifted).astype(bf16)
    m = np.arange(CMP_WIN)[:, None]
    cmp = lookup(q - (CMP_BLOCK - 1) - CMP_STRIDE * (m - CMP_WIN // 2), shifted[:, N_HEADS_A:])
    return near[:N_HEADS_A], near[N_HEADS_A:], cmp


def _static_tables(S, nc):
    n_s = S // SLC_BLOCK
    cmp_start = np.arange(nc)[None, :] * CMP_STRIDE
    slc_start = np.arange(n_s)[:, None] * SLC_BLOCK
    ovT = np.clip(np.minimum(cmp_start + CMP_BLOCK, slc_start + SLC_BLOCK)
                  - np.maximum(cmp_start, slc_start), 0, None).astype(np.float32) / CMP_BLOCK
    nkb = S // TK
    tok_blk = (np.arange(nkb)[:, None, None] * TK + np.arange(TK)[None, :, None]) // SLC_BLOCK
    e3T = (np.arange(LANES)[None, None, :] == tok_blk).astype(np.float32)
    return jnp.asarray(ovT, bf16), jnp.asarray(e3T, bf16)


def _compress_weights(w1, w2):
    z = jnp.zeros((HEAD_DIM, CMP_HIDDEN), w1.dtype)
    first = w1[:CMP_STRIDE * HEAD_DIM].reshape(CMP_STRIDE, HEAD_DIM, CMP_HIDDEN)
    second = w1[CMP_STRIDE * HEAD_DIM:].reshape(CMP_STRIDE, HEAD_DIM, CMP_HIDDEN)
    zz = jnp.broadcast_to(z, first.shape)
    tok = jnp.concatenate([jnp.concatenate([first, zz, second, zz], axis=2),
                           jnp.concatenate([zz, first, zz, second], axis=2)], axis=1)
    z2 = jnp.zeros((CMP_HIDDEN, HEAD_DIM), w2.dtype)
    return tok, jnp.block([[w2, z2], [z2, w2]])


def kernel(x, norm_mix, w_in, cmp_pe_k, cmp_w1_k, cmp_w2_k, cmp_pe_v, cmp_w1_v, cmp_w2_v, rel_bias,
           w_branch_a, w_branch_b, w_out, norm_mlp, w_mlp_in, w_mlp_out, norm_final):
    B, S, D = x.shape
    assert norm_mix.shape[0] == 1 and S % TQ == 0 and (S // CMP_STRIDE) % LANES == 0
    assert WINDOW % TK == 0 and (S // SLC_BLOCK) % SUBLANES == 0 and S // SLC_BLOCK <= LANES
    assert S // COUNT_ROWS <= 256
    G = N_KV_GROUPS_B
    rows = B * S
    nkb = S // TK
    nc = S // CMP_STRIDE
    k_top = min(TOPK_TOKENS, S // 4)
    n_sel = min(N_SLC_BLOCKS, S // SLC_BLOCK)
    x2 = x.reshape(rows, D)

    w = w_in[0]
    o_ka = W_QA
    o_va = o_ka + HEAD_DIM
    o_qi = o_va + HEAD_DIM
    o_ki = o_qi + W_QI
    o_wi = o_ki + IDX_DIM
    o_qb = o_wi + N_IDX_HEADS
    o_kvb = o_qb + W_QB
    o_gb = o_kvb + W_KVB
    o_gate = o_gb + W_GB
    kv = lambda kind: w[:, o_kvb + kind * G * HEAD_DIM:o_kvb + (kind + 1) * G * HEAD_DIM]
    pad = jnp.zeros((D, LANES - HEAD_DIM - IDX_DIM), w.dtype)
    w_rm = jnp.concatenate([w[:, o_ka:o_va], w[:, o_ki:o_wi], pad, kv(0), kv(1), kv(2), kv(4),
                            w[:, o_gate:]], axis=1).astype(bf16)
    w_t = jnp.concatenate([w[:, :o_ka], w[:, o_qi:o_ki], w[:, o_wi:o_qb], w[:, o_gb:o_gate],
                           w[:, o_qb:o_kvb], kv(3), kv(5), w[:, o_va:o_qi]], axis=1).T.astype(bf16)
    keys_a, kcv, ksw, gate, qaT, qiT, miscT, qbT, vT = _inproj(x2, norm_mix[0][None], w_rm, w_t)
    vT_blocks = vT.reshape(B, nkb, W_VT, TK)

    bias_a, bias_b, bcmp = _bias_tables(rel_bias)
    ovT, e3T = _static_tables(S, nc)

    o_a = _dsa(qiT, miscT, keys_a, qaT, vT_blocks, bias_a, B, S, k_top)

    chunk_w = CMP_STRIDE * HEAD_DIM
    pe = jnp.stack([cmp_pe_k[0], cmp_pe_v[0]]).reshape(2, 1, 2 * chunk_w)
    pe8 = jnp.broadcast_to(pe, (2, SUBLANES, 2 * chunk_w)).astype(bf16)
    w1 = jnp.stack([cmp_w1_k[0], cmp_w1_v[0]]).astype(bf16)
    tok_k, w2bd_k = _compress_weights(cmp_w1_k[0], cmp_w2_k[0])
    tok_v, w2bd_v = _compress_weights(cmp_w1_v[0], cmp_w2_v[0])
    ck, cvT = _compress(kcv, pe8, w1, jnp.stack([tok_k, tok_v]).astype(bf16),
                        w2bd_k.astype(bf16), w2bd_v.T.astype(bf16), B, S)

    o_b = _nsa(qbT, miscT, ck, cvT, ksw, vT_blocks, bias_b, bcmp, ovT, e3T, B, S, n_sel)

    x1 = _mix(o_a, o_b, gate, x2, w_branch_a[0].astype(bf16), w_branch_b[0].astype(bf16),
              w_out[0].astype(bf16))
    out = _mlp(x1, norm_mlp[0][None], w_mlp_in[0].astype(bf16), w_mlp_out[0].astype(bf16),
               norm_final[None])
    return out.reshape(B, S, D)
```

```python
import collections
import functools
import math

import numpy as np
import jax
import jax.numpy as jnp
from jax import lax
from jax.experimental import pallas as pl
from jax.experimental.pallas import tpu as pltpu

f32 = jnp.float32
bf16 = jnp.bfloat16

HEAD_DIM = 64
N_HEADS_A = 8
N_IDX_HEADS = 8
IDX_DIM = 32
TOPK_TOKENS = 256
N_HEADS_B = 8
N_KV_GROUPS_B = 2
HEADS_PER_GROUP = N_HEADS_B // N_KV_GROUPS_B
CMP_BLOCK = 32
CMP_STRIDE = 16
CMP_HIDDEN = 128
SLC_BLOCK = 64
N_SLC_BLOCKS = 16
WINDOW = 512
FORCED_SCORE = 1.0e4
N_BUCKETS = 32
MAX_DISTANCE = 128
EPS = 1e-6

W_QA = N_HEADS_A * HEAD_DIM
W_QI = N_IDX_HEADS * IDX_DIM
W_QB = N_HEADS_B * HEAD_DIM
W_KVB = 6 * N_KV_GROUPS_B * HEAD_DIM
W_GB = 3 * N_HEADS_B
W_MISC = N_IDX_HEADS + W_GB
W_VT = 5 * HEAD_DIM

LANES = 128
SUBLANES = 8
BF16_ROWS = 16
TQ = 256
TK = 256
QK_LOOKAHEAD = 3
PV_DELAY = 2
FAR_GROUP = 4
ACC_ROWS = HEAD_DIM + BF16_ROWS
CMP_WIN = 32
LOG2E = 1.4426950408889634
NEG_MASK = -2.0 ** 101
M_INIT = -2.0 ** 100
NEG_SCORE = -3.0e38
BIG_SCORE = 3.0e38
BIG_SCORE16 = 2.0 ** 127
COARSE_ITERS = 12
FINE_FIXED_ITERS = 9
MAX_SEARCH_ITERS = 400
TIE_CHECK_START = 14
COUNT_ROWS = 32
VMEM_LIMIT = 56 * 1024 * 1024

NT_DIMS = (((1,), (1,)), ((), ()))


def _cparams(*sem):
    return pltpu.CompilerParams(dimension_semantics=sem, vmem_limit_bytes=VMEM_LIMIT)


def _rms(x, g):
    return x * lax.rsqrt(jnp.mean(x * x, axis=-1, keepdims=True) + EPS) * g


def _inproj_kernel(x_ref, g_ref, w_ref, wT_ref, keys_ref, kcv_ref, ksw_ref, gate_ref,
                   qaT_ref, qiT_ref, miscT_ref, qbT_ref, vT_ref):
    q_scale = HEAD_DIM ** -0.5 * LOG2E
    for t in range(vT_ref.shape[0]):
        rs = slice(t * TK, (t + 1) * TK)
        h = _rms(x_ref[rs, :], g_ref[...]).astype(bf16)
        off = 0
        for ref in (keys_ref, kcv_ref, ksw_ref, gate_ref):
            width = ref.shape[-1]
            for c0 in range(0, width, 512):
                c1 = min(c0 + 512, width)
                y = jnp.dot(h, w_ref[:, off + c0:off + c1], preferred_element_type=f32)
                ref[rs, c0:c1] = y.astype(ref.dtype)
            off += width
        off = 0
        for ref, scale in ((qaT_ref, q_scale), (qiT_ref, None), (miscT_ref, None), (qbT_ref, q_scale),
                           (vT_ref.at[t], None)):
            n = ref.shape[0]
            y = lax.dot_general(wT_ref[off:off + n, :], h, NT_DIMS, preferred_element_type=f32)
            if scale is not None:
                y = y * scale
            if ref.shape[1] == TK:
                ref[...] = y.astype(ref.dtype)
            else:
                ref[:, rs] = y.astype(ref.dtype)
            off += n


def _inproj(x2, g, w_rm, w_t, sub_tiles=2):
    rows, d = x2.shape
    tm = sub_tiles * TK
    rm = ((LANES, bf16), (2 * LANES, f32), (2 * LANES, bf16), (2 * d, bf16))
    tr = ((W_QA, bf16), (W_QI, bf16), (W_MISC, f32), (W_QB, bf16))
    return pl.pallas_call(
        _inproj_kernel,
        grid=(rows // tm,),
        in_specs=[pl.BlockSpec((tm, d), lambda i: (i, 0)),
                  pl.BlockSpec((1, d), lambda i: (0, 0)),
                  pl.BlockSpec(w_rm.shape, lambda i: (0, 0)),
                  pl.BlockSpec(w_t.shape, lambda i: (0, 0))],
        out_specs=([pl.BlockSpec((tm, w), lambda i: (i, 0)) for w, _ in rm]
                   + [pl.BlockSpec((n, tm), lambda i: (0, i)) for n, _ in tr]
                   + [pl.BlockSpec((sub_tiles, W_VT, TK), lambda i: (i, 0, 0))]),
        out_shape=([jax.ShapeDtypeStruct((rows, w), dt) for w, dt in rm]
                   + [jax.ShapeDtypeStruct((n, rows), dt) for n, dt in tr]
                   + [jax.ShapeDtypeStruct((rows // TK, W_VT, TK), bf16)]),
        compiler_params=_cparams("parallel"),
    )(x2, g, w_rm, w_t)


def _count_ge(sc_ref, nkb, p):
    def body(kb, acc):
        hit = jnp.where(sc_ref[kb] >= p, 1.0, 0.0)
        return acc + jnp.sum(hit.reshape(TK // COUNT_ROWS, COUNT_ROWS, TQ), axis=0)
    acc = lax.fori_loop(0, nkb, body, jnp.zeros((COUNT_ROWS, TQ), f32))
    return jnp.sum(acc, axis=0, keepdims=True)


def _count_ge16(sc16_ref, nkb, p16):
    one, nil = jnp.ones((), bf16), jnp.zeros((), bf16)
    def body(kb, acc):
        hit = jnp.where(sc16_ref[kb] >= p16, one, nil).reshape(TK // COUNT_ROWS, COUNT_ROWS, TQ)
        for i in range(TK // COUNT_ROWS):
            acc = acc + hit[i]
        return acc
    acc = lax.fori_loop(0, nkb, body, jnp.zeros((COUNT_ROWS, TQ), bf16))
    return jnp.sum(acc.astype(f32), axis=0, keepdims=True)


def _coarse_bracket(sc16_ref, nkb, rmin, rmax, active, kf):
    rmax16 = rmax.astype(bf16).astype(f32)

    def pivot(lo16, hi16):
        mid = ((lo16 + jnp.minimum(hi16, rmax16)) * 0.5).astype(bf16).astype(f32)
        return mid, active & (mid > lo16) & (mid < hi16)

    def body(_, st):
        lo16, hi16 = st
        mid, ok = pivot(lo16, hi16)
        ge = _count_ge16(sc16_ref, nkb, mid.astype(bf16)) >= kf
        return jnp.where(ok & ge, mid, lo16), jnp.where(ok & jnp.logical_not(ge), mid, hi16)

    lo16, hi16 = lax.fori_loop(
        0, COARSE_ITERS, body, (rmin.astype(bf16).astype(f32), jnp.full_like(rmin, BIG_SCORE16)))
    below = jnp.maximum(jnp.abs(lo16) * 2.0 ** -7, 1e-37)
    return lo16 - below, jnp.where(hi16 < BIG_SCORE16, hi16, BIG_SCORE)


def _select_topk(sc_ref, sc16_ref, nkb, rmin, rmax, nvalid, k):
    kf = float(k)
    active = nvalid > kf
    lo_c, hi_c = _coarse_bracket(sc16_ref, nkb, rmin, rmax, active, kf)
    lo0 = jnp.where(active, lo_c, NEG_SCORE)
    cl0 = jnp.where(active, nvalid, kf)
    hi0 = jnp.where(active, hi_c, BIG_SCORE)
    zero = jnp.zeros_like(lo0)

    def pending(cl, tie):
        return (cl != kf) & (tie == 0.0)

    def any_pending(st):
        return jnp.max(jnp.where(pending(st[3], st[5]), 1.0, 0.0)) > 0.0

    def bisect(st):
        it, lo, hi, cl, ch, tie = st
        pend = pending(cl, tie)
        hie = jnp.minimum(hi, rmax)
        p = lo + (hie - lo) * 0.5
        p = jnp.where(p > lo, p, hie)
        cnt = _count_ge(sc_ref, nkb, p)
        ge = cnt >= kf
        up_lo = pend & ge
        up_hi = pend & jnp.logical_not(ge)
        lo = jnp.where(up_lo, p, lo)
        cl = jnp.where(up_lo, cnt, cl)
        hi = jnp.where(up_hi, p, hi)
        ch = jnp.where(up_hi, cnt, ch)
        return it + 1, lo, hi, cl, ch, tie

    def tie_check(st):
        it, lo, hi, cl, ch, tie = st
        def scan(kb, c):
            dmin, dmax = c
            s = sc_ref[kb]
            dmin = jnp.minimum(dmin, jnp.min(jnp.where(s >= lo, s, BIG_SCORE), axis=0, keepdims=True))
            dmax = jnp.maximum(dmax, jnp.max(jnp.where(s < hi, s, NEG_SCORE), axis=0, keepdims=True))
            return dmin, dmax
        dmin, dmax = lax.fori_loop(
            0, nkb, scan, (jnp.full((1, TQ), BIG_SCORE, f32), jnp.full((1, TQ), NEG_SCORE, f32)))
        new_tie = pending(cl, tie) & (dmin == dmax)
        return it, jnp.where(new_tie, dmin, lo), hi, cl, ch, jnp.where(new_tie, 1.0, tie)

    st = (jnp.int32(0), lo0, hi0, cl0, zero, zero)
    st = lax.fori_loop(0, FINE_FIXED_ITERS, lambda _, st: bisect(st), st)
    st = lax.while_loop(lambda st: any_pending(st) & (st[0] < TIE_CHECK_START), bisect, st)
    st = lax.while_loop(lambda st: any_pending(st) & (st[0] < MAX_SEARCH_ITERS),
                        lambda st: bisect(tie_check(st)), st)
    _, lo, hi, cl, ch, tie = st
    any_tie = jnp.max(tie) > 0.0

    @pl.when(any_tie)
    def _():
        r = lax.broadcasted_iota(jnp.int32, (TK, TK), 0)
        c = lax.broadcasted_iota(jnp.int32, (TK, TK), 1)
        tril = jnp.where(c <= r, 1.0, 0.0).astype(bf16)
        above = lax.fori_loop(
            0, nkb, lambda kb, n: n + jnp.sum(jnp.where(sc_ref[kb] > lo, 1.0, 0.0), axis=0, keepdims=True), zero)
        need = kf - above

        def body(kb, run):
            s = sc_ref[kb]
            eq = s == lo
            eqf = jnp.where(eq, 1.0, 0.0)
            cum = run + jnp.dot(tril, eqf.astype(bf16), preferred_element_type=f32)
            sel_tie = jnp.where((s > lo) | (eq & (cum <= need)), 1.0, -1.0)
            sel_all = jnp.where(s >= lo, 1.0, -1.0)
            sc_ref[kb] = jnp.where(tie > 0.0, sel_tie, sel_all)
            return run + jnp.sum(eqf, axis=0, keepdims=True)

        lax.fori_loop(0, nkb, body, zero)

    return jnp.where(any_tie, 0.0, lo)


def _rank_select(score, k):
    n, width = score.shape
    tiles = [score[v * SUBLANES:(v + 1) * SUBLANES] for v in range(n // SUBLANES)]
    ridx = lax.broadcasted_iota(jnp.int32, (SUBLANES, width), 0)
    ranks = [jnp.zeros((SUBLANES, width), f32) for _ in tiles]
    for i in range(n):
        si = jnp.broadcast_to(tiles[i // SUBLANES][i % SUBLANES:i % SUBLANES + 1], (SUBLANES, width))
        for v, x in enumerate(tiles):
            if v * SUBLANES > i:
                inc = jnp.where(si >= x, 1.0, 0.0)
            elif (v + 1) * SUBLANES <= i:
                inc = jnp.where(si > x, 1.0, 0.0)
            else:
                inc = jnp.where(ridx > i % SUBLANES, jnp.where(si >= x, 1.0, 0.0), jnp.where(si > x, 1.0, 0.0))
            ranks[v] = ranks[v] + inc
    flags = [jnp.where(r < float(k), 1.0, 0.0) for r in ranks]
    flags.append(jnp.zeros((LANES - n, width), f32))
    return jnp.concatenate(flags, axis=0)


Stream = collections.namedtuple("Stream", "q keys vT madd bias slot")


def _attn_init(m_ref, acc_ref):
    m_ref[...] = jnp.full(m_ref.shape, M_INIT, f32)
    acc_ref[...] = jnp.zeros(acc_ref.shape, f32)


def _attn_streams(streams, m_ref, acc_ref):
    def logits(i):
        st = streams[i]
        s = jnp.dot(st.keys, st.q, preferred_element_type=f32).astype(bf16)
        if st.bias is not None:
            s = s + st.bias
        if st.madd is not None:
            s = s + st.madd
        return s

    def values(i, alpha, p):
        st = streams[i]
        acc_ref[st.slot] = alpha * acc_ref[st.slot] + jnp.dot(st.vT, p, preferred_element_type=f32)

    n = len(streams)
    pending = {i: logits(i) for i in range(min(QK_LOOKAHEAD, n))}
    deferred = []
    for i in range(n):
        if i + QK_LOOKAHEAD < n:
            pending[i + QK_LOOKAHEAD] = logits(i + QK_LOOKAHEAD)
        s = pending.pop(i)
        slot = streams[i].slot
        m_old = m_ref[slot]
        m_new = jnp.maximum(m_old, jnp.max(s, axis=0, keepdims=True).astype(f32))
        alpha = jnp.exp2(m_old - m_new)
        p = jnp.exp2(s - m_new.astype(bf16))
        m_ref[slot] = m_new
        deferred.append((i, alpha, p))
        if len(deferred) > PV_DELAY:
            values(*deferred.pop(0))
    for d in deferred:
        values(*d)


def _attn_out(slot, acc_ref):
    acc = acc_ref[slot]
    return acc[0:HEAD_DIM] * (1.0 / jnp.maximum(acc[HEAD_DIM:HEAD_DIM + 1], 1e-30))


def _with_ones(vT):
    return jnp.concatenate([vT, jnp.ones((BF16_ROWS, vT.shape[1]), vT.dtype)], axis=0)


def _mask_add(mask):
    return jnp.where(mask, 0.0, NEG_MASK).astype(bf16)


def _pad_rows(x, start):
    parts = []
    if start:
        parts.append(jnp.zeros((start, x.shape[1]), x.dtype))
    parts.append(x)
    if LANES - start - x.shape[0]:
        parts.append(jnp.zeros((LANES - start - x.shape[0], x.shape[1]), x.dtype))
    return jnp.concatenate(parts, axis=0)


def _far_blocks(n_far, run):
    def body(i, _):
        run([FAR_GROUP * i + u for u in range(FAR_GROUP)])
        return 0
    lax.fori_loop(0, n_far // FAR_GROUP, body, 0)
    rem = n_far % FAR_GROUP
    start = n_far - rem
    size = FAR_GROUP // 2
    while size >= 1:
        @pl.when((rem & size) != 0)
        def _(start=start, size=size):
            run([start + u for u in range(size)])
        start = start + (rem & size)
        size //= 2


def _dsa_kernel(qiT_ref, miscT_ref, keys_ref, qaT_ref, vT_ref, bias_ref, o_ref,
                sc_ref, sc16_ref, qi_ref, qa_ref, oT_ref, m_ref, acc_ref, *, k_top):
    j = pl.program_id(1)
    w_scale = (N_IDX_HEADS ** -0.5) * (IDX_DIM ** -0.5)
    w = miscT_ref[0:N_IDX_HEADS, :] * w_scale
    for h in range(N_IDX_HEADS):
        qi_ref[h] = _pad_rows(qiT_ref[h * IDX_DIM:(h + 1) * IDX_DIM, :], HEAD_DIM)
    for h in range(N_HEADS_A):
        qa_ref[h] = _pad_rows(qaT_ref[h * HEAD_DIM:(h + 1) * HEAD_DIM, :], 0)
    krow = lax.broadcasted_iota(jnp.int32, (TK, TQ), 0)
    qcol = lax.broadcasted_iota(jnp.int32, (TK, TQ), 1)
    causal = krow <= qcol

    def keys_blk(kb):
        return keys_ref[pl.ds(pl.multiple_of(kb * TK, TK), TK), :]

    def score_blk(kb):
        keys = keys_blk(kb)
        acc = None
        for h in range(N_IDX_HEADS):
            sh = jnp.dot(keys, qi_ref[h], preferred_element_type=f32)
            term = w[h:h + 1, :] * jnp.maximum(sh, 0.0)
            acc = term if acc is None else acc + term
        return acc

    def far_score(kb, carry):
        mn, mx = carry
        a = score_blk(kb)
        sc_ref[kb] = a
        sc16_ref[kb] = a.astype(bf16)
        return (jnp.minimum(mn, jnp.min(a, axis=0, keepdims=True)),
                jnp.maximum(mx, jnp.max(a, axis=0, keepdims=True)))

    def diag_score(carry):
        mn, mx = carry
        a = score_blk(j)
        a_diag = jnp.where(causal, a, NEG_SCORE)
        sc_ref[j] = a_diag
        sc16_ref[j] = a_diag.astype(bf16)
        return (jnp.minimum(mn, jnp.min(jnp.where(causal, a, BIG_SCORE), axis=0, keepdims=True)),
                jnp.maximum(mx, jnp.max(a_diag, axis=0, keepdims=True)))

    stats = lax.fori_loop(0, j // 2, lambda i, c: far_score(2 * i + 1, far_score(2 * i, c)),
                          (jnp.full((1, TQ), BIG_SCORE, f32), jnp.full((1, TQ), NEG_SCORE, f32)))
    rmin, rmax = lax.cond(j % 2 == 1, lambda c: diag_score(far_score(j - 1, c)), diag_score, stats)
    nvalid = (j * TQ + qcol[0:1, :] + 1).astype(f32)
    lo = _select_topk(sc_ref, sc16_ref, j + 1, rmin, rmax, nvalid, k_top)

    _attn_init(m_ref, acc_ref)

    def block_streams(kb, mask, bias_rows):
        keys, vT, madd = keys_blk(kb), _with_ones(vT_ref[kb]), _mask_add(mask)
        return [Stream(qa_ref[h], keys, vT, madd,
                       None if bias_rows is None else bias_ref[h, bias_rows, :], h) for h in range(N_HEADS_A)]

    def far(kb):
        return block_streams(kb, sc_ref[kb] >= lo, None)

    _far_blocks(jnp.maximum(j - 1, 0),
                lambda kbs: _attn_streams([st for kb in kbs for st in far(kb)], m_ref, acc_ref))
    diag_mask = (sc_ref[j] >= lo) & causal

    @pl.when(j >= 1)
    def _():
        _attn_streams(block_streams(j - 1, sc_ref[j - 1] >= lo, slice(0, TK))
                      + block_streams(j, diag_mask, slice(TK, 2 * TK)), m_ref, acc_ref)

    @pl.when(j == 0)
    def _():
        _attn_streams(block_streams(j, diag_mask, slice(TK, 2 * TK)), m_ref, acc_ref)

    for h in range(N_HEADS_A):
        oT_ref[h * HEAD_DIM:(h + 1) * HEAD_DIM, :] = _attn_out(h, acc_ref)
    o_ref[...] = oT_ref[...].T.astype(o_ref.dtype)


def _attn_scratch(n_slots):
    return [pltpu.VMEM((n_slots, 1, TQ), f32), pltpu.VMEM((n_slots, ACC_ROWS, TQ), f32)]


def _dsa(qiT, miscT, keys, qaT, vT_blocks, biasT, B, S, k_top):
    nq = S // TQ
    nkb = S // TK
    col_spec = lambda n: pl.BlockSpec((n, TQ), lambda b, j: (0, b * nq + j))
    return pl.pallas_call(
        functools.partial(_dsa_kernel, k_top=k_top),
        grid=(B, nq),
        in_specs=[col_spec(W_QI), col_spec(W_MISC),
                  pl.BlockSpec((S, LANES), lambda b, j: (b, 0)),
                  col_spec(W_QA),
                  pl.BlockSpec((None, nkb, HEAD_DIM, TK), lambda b, j: (b, 0, 4, 0)),
                  pl.BlockSpec(biasT.shape, lambda b, j: (0, 0, 0))],
        out_specs=pl.BlockSpec((TQ, W_QA), lambda b, j: (b * nq + j, 0)),
        out_shape=jax.ShapeDtypeStruct((B * S, W_QA), bf16),
        scratch_shapes=[pltpu.VMEM((nkb, TK, TQ), f32),
                        pltpu.VMEM((nkb, TK, TQ), bf16),
                        pltpu.VMEM((N_IDX_HEADS, LANES, TQ), bf16),
                        pltpu.VMEM((N_HEADS_A, LANES, TQ), bf16),
                        pltpu.VMEM((W_QA, TQ), f32)] + _attn_scratch(N_HEADS_A),
        compiler_params=_cparams("parallel", "parallel"),
    )(qiT, miscT, keys, qaT, vT_blocks, biasT)


def _compress_kernel(xk_ref, xv_ref, pe_ref, w1_ref, w1tok_ref, w2bd_ref, w2bdT_ref, ck_ref, cvT_ref):
    nc = ck_ref.shape[0]
    G = N_KV_GROUPS_B

    def hidden(kind):
        x_ref = (xk_ref, xv_ref)[kind]
        const = jnp.dot(pe_ref[kind], w1_ref[kind], preferred_element_type=f32)[0:1]
        ab = jnp.zeros((nc, 2 * G * CMP_HIDDEN), f32)
        for i in range(CMP_STRIDE):
            tok = x_ref[pl.ds(i, nc, stride=CMP_STRIDE), :]
            ab = ab + jnp.dot(tok.astype(bf16), w1tok_ref[kind, i], preferred_element_type=f32)
        hs = []
        for g in range(G):
            first = ab[:, g * CMP_HIDDEN:(g + 1) * CMP_HIDDEN]
            second = pltpu.roll(ab[:, (G + g) * CMP_HIDDEN:(G + g + 1) * CMP_HIDDEN], nc - 1, 0)
            hs.append(jax.nn.gelu(first + second + const).astype(bf16))
        return jnp.concatenate(hs, axis=1)

    ck_ref[...] = jnp.dot(hidden(0), w2bd_ref[...], preferred_element_type=f32).astype(ck_ref.dtype)
    cvT_ref[...] = lax.dot_general(w2bdT_ref[...], hidden(1), NT_DIMS,
                                   preferred_element_type=f32).astype(cvT_ref.dtype)


def _compress(kcv, pe8, w1, w1tok, w2bd_k, w2bdT_v, B, S):
    nc = S // CMP_STRIDE
    cs = lambda a: pl.BlockSpec(a.shape, lambda b: (0,) * a.ndim)
    return pl.pallas_call(
        _compress_kernel,
        grid=(B,),
        in_specs=[pl.BlockSpec((S, LANES), lambda b: (b, 0)),
                  pl.BlockSpec((S, LANES), lambda b: (b, 1)),
                  cs(pe8), cs(w1), cs(w1tok), cs(w2bd_k), cs(w2bdT_v)],
        out_specs=[pl.BlockSpec((None, nc, LANES), lambda b: (b, 0, 0)),
                   pl.BlockSpec((None, LANES, nc), lambda b: (b, 0, 0))],
        out_shape=[jax.ShapeDtypeStruct((B, nc, LANES), bf16), jax.ShapeDtypeStruct((B, LANES, nc), bf16)],
        compiler_params=_cparams("parallel"),
    )(kcv, kcv, pe8, w1, w1tok, w2bd_k, w2bdT_v)


def _nsa_kernel(qbT_ref, miscT_ref, ck_ref, cvT_ref, ksw_ref, vT_ref, bias_ref, bcmp_ref, ovT_ref, e3_ref,
                o_ref, qb_ref, s_ref, oT_ref, m_ref, acc_ref, *, n_sel):
    j = pl.program_id(1)
    nc = ck_ref.shape[0]
    n_s = ovT_ref.shape[0]
    R = HEADS_PER_GROUP
    G = N_KV_GROUPS_B
    H = N_HEADS_B
    for h in range(H):
        qb_ref[h] = _pad_rows(qbT_ref[h * HEAD_DIM:(h + 1) * HEAD_DIM, :], (h // R) * HEAD_DIM)
    krow = lax.broadcasted_iota(jnp.int32, (TK, TQ), 0)
    qcol = lax.broadcasted_iota(jnp.int32, (TK, TQ), 1)
    causal = krow <= qcol
    t_c = j * TQ + lax.broadcasted_iota(jnp.int32, (nc, TQ), 1)
    c_c = lax.broadcasted_iota(jnp.int32, (nc, TQ), 0)
    mask_c = c_c * CMP_STRIDE + (CMP_BLOCK - 1) <= t_c
    blk = lax.broadcasted_iota(jnp.int32, (n_s, TQ), 0)
    cur = (j * TQ + lax.broadcasted_iota(jnp.int32, (n_s, TQ), 1)) // SLC_BLOCK
    valid = blk <= cur
    forced = (blk == 0) | (blk == cur) | (blk == cur - 1)
    gates = jax.nn.sigmoid(miscT_ref[N_IDX_HEADS:N_IDX_HEADS + W_GB, :])
    win_blocks = WINDOW // TK
    cmp_per_q = TQ // CMP_STRIDE
    half = CMP_WIN // 2

    for h in range(H):
        s_ref[h] = jnp.dot(ck_ref[...], qb_ref[h], preferred_element_type=f32)

    @pl.when(j == 0)
    def _():
        for h in range(H):
            s_ref[h, 0:half, :] += bcmp_ref[h, half:CMP_WIN, :]

    @pl.when(j > 0)
    def _():
        rows = pl.ds(pl.multiple_of(j * cmp_per_q - half, half), CMP_WIN)
        for h in range(H):
            s_ref[h, rows, :] += bcmp_ref[h]

    o_c = []
    psum = [jnp.zeros((nc, TQ), f32) for _ in range(G)]
    for h in range(H):
        g = h // R
        s = jnp.where(mask_c, s_ref[h], NEG_MASK)
        m = jnp.max(s, axis=0, keepdims=True)
        e = jnp.where(mask_c, jnp.exp2(s - m), 0.0)
        p = e * (1.0 / jnp.maximum(jnp.sum(e, axis=0, keepdims=True), 1e-30))
        psum[g] = psum[g] + p
        o_c.append(jnp.dot(cvT_ref[g * HEAD_DIM:(g + 1) * HEAD_DIM, :], p.astype(bf16),
                           preferred_element_type=f32))

    selT = []
    for g in range(G):
        p_hi = psum[g].astype(bf16)
        p_lo = (psum[g] - p_hi.astype(f32)).astype(bf16)
        imp = (jnp.dot(ovT_ref[...], p_hi, preferred_element_type=f32)
               + jnp.dot(ovT_ref[...], p_lo, preferred_element_type=f32))
        score = jnp.where(valid, jnp.where(forced, FORCED_SCORE, imp), NEG_SCORE)
        selT.append(_rank_select(score, n_sel).astype(bf16))

    _attn_init(m_ref, acc_ref)

    def branch_streams(kb, lanes, v_row0, masks, bias_rows, slot0):
        keys = ksw_ref[pl.ds(pl.multiple_of(kb * TK, TK), TK), lanes]
        out = []
        for g in range(G):
            vT = _with_ones(vT_ref[kb, v_row0 + g * HEAD_DIM:v_row0 + (g + 1) * HEAD_DIM, :])
            madd = None if masks[g] is None else _mask_add(masks[g])
            for h in range(g * R, (g + 1) * R):
                out.append(Stream(qb_ref[h], keys, vT, madd,
                                  None if bias_rows is None else bias_ref[h, bias_rows, :], slot0 + h))
        return out

    def tok_masks(kb, extra=None):
        ms = [jnp.dot(e3_ref[kb], selT[g], preferred_element_type=f32) > 0.5 for g in range(G)]
        return ms if extra is None else [m & extra for m in ms]

    def sel_streams(kb, extra=None, bias_rows=None):
        return branch_streams(kb, slice(0, LANES), 0, tok_masks(kb, extra), bias_rows, 0)

    def win_streams(kb, mask=None, bias_rows=None):
        return branch_streams(kb, slice(LANES, 2 * LANES), G * HEAD_DIM, [mask] * G, bias_rows, H)

    _far_blocks(jnp.maximum(j - 1, 0),
                lambda kbs: _attn_streams([st for kb in kbs for st in sel_streams(kb)], m_ref, acc_ref))

    prev_rows, diag_rows = slice(0, TK), slice(TK, 2 * TK)

    def near_streams(n_before):
        out = []
        for d in range(min(n_before, win_blocks), 1, -1):
            out += win_streams(j - d, (krow > qcol) if d == win_blocks else None)
        if n_before >= 1:
            out += sel_streams(j - 1, None, prev_rows)
            out += win_streams(j - 1, (krow > qcol) if win_blocks == 1 else None, prev_rows)
        return out + sel_streams(j, causal, diag_rows) + win_streams(j, causal, diag_rows)

    for n_before in range(win_blocks + 1):
        @pl.when((j >= n_before) if n_before == win_blocks else (j == n_before))
        def _(n_before=n_before):
            _attn_streams(near_streams(n_before), m_ref, acc_ref)

    for h in range(H):
        oT_ref[h * HEAD_DIM:(h + 1) * HEAD_DIM, :] = (
            gates[h:h + 1, :] * o_c[h]
            + gates[H + h:H + h + 1, :] * _attn_out(h, acc_ref)
            + gates[2 * H + h:2 * H + h + 1, :] * _attn_out(H + h, acc_ref))
    o_ref[...] = oT_ref[...].T.astype(o_ref.dtype)


def _nsa(qbT, miscT, ck, cvT, ksw, vT_blocks, biasT, bcmpT, ovT, e3T, B, S, n_sel):
    nq = S // TQ
    nkb = S // TK
    nc = ck.shape[1]
    col_spec = lambda n: pl.BlockSpec((n, TQ), lambda b, j: (0, b * nq + j))
    const = lambda a: pl.BlockSpec(a.shape, lambda b, j: (0,) * a.ndim)
    return pl.pallas_call(
        functools.partial(_nsa_kernel, n_sel=n_sel),
        grid=(B, nq),
        in_specs=[col_spec(W_QB), col_spec(W_MISC),
                  pl.BlockSpec((None, nc, LANES), lambda b, j: (b, 0, 0)),
                  pl.BlockSpec((None, LANES, nc), lambda b, j: (b, 0, 0)),
                  pl.BlockSpec((S, 2 * LANES), lambda b, j: (b, 0)),
                  pl.BlockSpec((None, nkb, 4 * HEAD_DIM, TK), lambda b, j: (b, 0, 0, 0)),
                  const(biasT), const(bcmpT), const(ovT), const(e3T)],
        out_specs=pl.BlockSpec((TQ, W_QB), lambda b, j: (b * nq + j, 0)),
        out_shape=jax.ShapeDtypeStruct((B * S, W_QB), bf16),
        scratch_shapes=[pltpu.VMEM((N_HEADS_B, LANES, TQ), bf16),
                        pltpu.VMEM((N_HEADS_B, nc, TQ), f32),
                        pltpu.VMEM((W_QB, TQ), f32)] + _attn_scratch(2 * N_HEADS_B),
        compiler_params=_cparams("parallel", "parallel"),
    )(qbT, miscT, ck, cvT, ksw, vT_blocks, biasT, bcmpT, ovT, e3T)


def _mix_kernel(oa_ref, ob_ref, gate_ref, x_ref, wa_ref, wb_ref, wo_ref, x1_ref):
    d = x_ref.shape[-1]
    ga = jax.nn.sigmoid(gate_ref[:, :d].astype(f32))
    gb = jax.nn.sigmoid(gate_ref[:, d:].astype(f32))
    mix = (ga * jnp.dot(oa_ref[...], wa_ref[...], preferred_element_type=f32)
           + gb * jnp.dot(ob_ref[...], wb_ref[...], preferred_element_type=f32))
    x1_ref[...] = x_ref[...] + jnp.dot(mix.astype(bf16), wo_ref[...], preferred_element_type=f32)


def _mix(oa, ob, gate, x2, wa, wb, wo, tm=512):
    rows, d = x2.shape
    rs = lambda w: pl.BlockSpec((tm, w), lambda i: (i, 0))
    cs = lambda a: pl.BlockSpec(a.shape, lambda i: (0, 0))
    return pl.pallas_call(
        _mix_kernel,
        grid=(rows // tm,),
        in_specs=[rs(W_QA), rs(W_QB), rs(2 * d), rs(d), cs(wa), cs(wb), cs(wo)],
        out_specs=rs(d),
        out_shape=jax.ShapeDtypeStruct((rows, d), f32),
        compiler_params=_cparams("parallel"),
    )(oa, ob, gate, x2, wa, wb, wo)


def _mlp_kernel(x1_ref, g_ref, w1_ref, w2_ref, gf_ref, o_ref, *, chunk, sub_rows):
    for r0 in range(0, x1_ref.shape[0], sub_rows):
        x1 = x1_ref[r0:r0 + sub_rows, :]
        h = _rms(x1, g_ref[...]).astype(bf16)
        acc = jnp.zeros(x1.shape, f32)
        for c0 in range(0, w1_ref.shape[1], chunk):
            u = jnp.dot(h, w1_ref[:, c0:c0 + chunk], preferred_element_type=f32)
            u = jnp.square(jnp.maximum(u, 0.0)).astype(bf16)
            acc = acc + jnp.dot(u, w2_ref[c0:c0 + chunk, :], preferred_element_type=f32)
        o_ref[r0:r0 + sub_rows, :] = _rms(x1 + acc, gf_ref[...])


def _mlp(x1, g, w1, w2, gf, tm=512, chunk=512, sub_rows=256):
    rows, d = x1.shape
    rs = pl.BlockSpec((tm, d), lambda i: (i, 0))
    cs = lambda a: pl.BlockSpec(a.shape, lambda i: (0, 0))
    ws = lambda a: pl.BlockSpec(a.shape, lambda i: (0, 0), pipeline_mode=pl.Buffered(1))
    return pl.pallas_call(
        functools.partial(_mlp_kernel, chunk=chunk, sub_rows=sub_rows),
        grid=(rows // tm,),
        in_specs=[rs, cs(g), ws(w1), ws(w2), cs(gf)],
        out_specs=rs,
        out_shape=jax.ShapeDtypeStruct((rows, d), f32),
        compiler_params=_cparams("parallel"),
    )(x1, g, w1, w2, gf)


def _t5_bucket_np(dist):
    n = np.maximum(dist, 0)
    max_exact = N_BUCKETS // 2
    nf = np.maximum(n, 1).astype(np.float32)
    large = max_exact + (np.log(nf / np.float32(max_exact)) / np.float32(math.log(MAX_DISTANCE / max_exact))
                         * np.float32(N_BUCKETS - max_exact)).astype(np.int32)
    large = np.minimum(large, N_BUCKETS - 1)
    return np.where(n < max_exact, n, large)


def _bias_tables(rel_bias):
    shifted = (rel_bias.astype(f32) - rel_bias[N_BUCKETS - 1].astype(f32)) * LOG2E

    def lookup(dist, tab):
        bucket = np.where((dist >= 0) & (dist < MAX_DISTANCE), _t5_bucket_np(dist), N_BUCKETS - 1)
        onehot = (jnp.asarray(bucket, jnp.int32)[..., None] == jnp.arange(N_BUCKETS)).astype(f32)
        return jnp.einsum('kqb,bh->hkq', onehot, tab, precision=lax.Precision.HIGHEST)

    q = np.arange(TQ)[None, :]
    near = lookup(q + TK - np.arange(2 * TK)[:, None], shifted).astype(bf16)
    m = np.arange(CMP_WIN)[:, None]
    cmp = lookup(q - (CMP_BLOCK - 1) - CMP_STRIDE * (m - CMP_WIN // 2), shifted[:, N_HEADS_A:])
    return near[:N_HEADS_A], near[N_HEADS_A:], cmp


def _static_tables(S, nc):
    n_s = S // SLC_BLOCK
    cmp_start = np.arange(nc)[None, :] * CMP_STRIDE
    slc_start = np.arange(n_s)[:, None] * SLC_BLOCK
    ovT = np.clip(np.minimum(cmp_start + CMP_BLOCK, slc_start + SLC_BLOCK)
                  - np.maximum(cmp_start, slc_start), 0, None).astype(np.float32) / CMP_BLOCK
    nkb = S // TK
    tok_blk = (np.arange(nkb)[:, None, None] * TK + np.arange(TK)[None, :, None]) // SLC_BLOCK
    e3T = (np.arange(LANES)[None, None, :] == tok_blk).astype(np.float32)
    return jnp.asarray(ovT, bf16), jnp.asarray(e3T, bf16)


def _compress_weights(w1, w2):
    z = jnp.zeros((HEAD_DIM, CMP_HIDDEN), w1.dtype)
    first = w1[:CMP_STRIDE * HEAD_DIM].reshape(CMP_STRIDE, HEAD_DIM, CMP_HIDDEN)
    second = w1[CMP_STRIDE * HEAD_DIM:].reshape(CMP_STRIDE, HEAD_DIM, CMP_HIDDEN)
    zz = jnp.broadcast_to(z, first.shape)
    tok = jnp.concatenate([jnp.concatenate([first, zz, second, zz], axis=2),
                           jnp.concatenate([zz, first, zz, second], axis=2)], axis=1)
    z2 = jnp.zeros((CMP_HIDDEN, HEAD_DIM), w2.dtype)
    return tok, jnp.block([[w2, z2], [z2, w2]])


def kernel(x, norm_mix, w_in, cmp_pe_k, cmp_w1_k, cmp_w2_k, cmp_pe_v, cmp_w1_v, cmp_w2_v, rel_bias,
           w_branch_a, w_branch_b, w_out, norm_mlp, w_mlp_in, w_mlp_out, norm_final):
    B, S, D = x.shape
    assert norm_mix.shape[0] == 1 and S % TQ == 0 and (S // CMP_STRIDE) % LANES == 0
    assert WINDOW % TK == 0 and (S // SLC_BLOCK) % SUBLANES == 0 and S // SLC_BLOCK <= LANES
    assert S // COUNT_ROWS <= 256
    G = N_KV_GROUPS_B
    rows = B * S
    nkb = S // TK
    nc = S // CMP_STRIDE
    k_top = min(TOPK_TOKENS, S // 4)
    n_sel = min(N_SLC_BLOCKS, S // SLC_BLOCK)
    x2 = x.reshape(rows, D)

    w = w_in[0]
    o_ka = W_QA
    o_va = o_ka + HEAD_DIM
    o_qi = o_va + HEAD_DIM
    o_ki = o_qi + W_QI
    o_wi = o_ki + IDX_DIM
    o_qb = o_wi + N_IDX_HEADS
    o_kvb = o_qb + W_QB
    o_gb = o_kvb + W_KVB
    o_gate = o_gb + W_GB
    kv = lambda kind: w[:, o_kvb + kind * G * HEAD_DIM:o_kvb + (kind + 1) * G * HEAD_DIM]
    pad = jnp.zeros((D, LANES - HEAD_DIM - IDX_DIM), w.dtype)
    w_rm = jnp.concatenate([w[:, o_ka:o_va], w[:, o_ki:o_wi], pad, kv(0), kv(1), kv(2), kv(4),
                            w[:, o_gate:]], axis=1).astype(bf16)
    w_t = jnp.concatenate([w[:, :o_ka], w[:, o_qi:o_ki], w[:, o_wi:o_qb], w[:, o_gb:o_gate],
                           w[:, o_qb:o_kvb], kv(3), kv(5), w[:, o_va:o_qi]], axis=1).T.astype(bf16)
    keys_a, kcv, ksw, gate, qaT, qiT, miscT, qbT, vT = _inproj(x2, norm_mix[0][None], w_rm, w_t)
    vT_blocks = vT.reshape(B, nkb, W_VT, TK)

    bias_a, bias_b, bcmp = _bias_tables(rel_bias)
    ovT, e3T = _static_tables(S, nc)

    o_a = _dsa(qiT, miscT, keys_a, qaT, vT_blocks, bias_a, B, S, k_top)

    chunk_w = CMP_STRIDE * HEAD_DIM
    pe = jnp.stack([cmp_pe_k[0], cmp_pe_v[0]]).reshape(2, 1, 2 * chunk_w)
    pe8 = jnp.broadcast_to(pe, (2, SUBLANES, 2 * chunk_w)).astype(bf16)
    w1 = jnp.stack([cmp_w1_k[0], cmp_w1_v[0]]).astype(bf16)
    tok_k, w2bd_k = _compress_weights(cmp_w1_k[0], cmp_w2_k[0])
    tok_v, w2bd_v = _compress_weights(cmp_w1_v[0], cmp_w2_v[0])
    ck, cvT = _compress(kcv, pe8, w1, jnp.stack([tok_k, tok_v]).astype(bf16),
                        w2bd_k.astype(bf16), w2bd_v.T.astype(bf16), B, S)

    o_b = _nsa(qbT, miscT, ck, cvT, ksw, vT_blocks, bias_b, bcmp, ovT, e3T, B, S, n_sel)

    x1 = _mix(o_a, o_b, gate, x2, w_branch_a[0].astype(bf16), w_branch_b[0].astype(bf16),
              w_out[0].astype(bf16))
    out = _mlp(x1, norm_mlp[0][None], w_mlp_in[0].astype(bf16), w_mlp_out[0].astype(bf16),
               norm_final[None])
    return out.reshape(B, S, D)
```

```python
import collections
import functools
import math

import numpy as np
import jax
import jax.numpy as jnp
from jax import lax
from jax.experimental import pallas as pl
from jax.experimental.pallas import tpu as pltpu

f32 = jnp.float32
bf16 = jnp.bfloat16

HEAD_DIM = 64
N_HEADS_A = 8
N_IDX_HEADS = 8
IDX_DIM = 32
TOPK_TOKENS = 256
N_HEADS_B = 8
N_KV_GROUPS_B = 2
HEADS_PER_GROUP = N_HEADS_B // N_KV_GROUPS_B
CMP_BLOCK = 32
CMP_STRIDE = 16
CMP_HIDDEN = 128
SLC_BLOCK = 64
N_SLC_BLOCKS = 16
WINDOW = 512
FORCED_SCORE = 1.0e4
N_BUCKETS = 32
MAX_DISTANCE = 128
EPS = 1e-6

W_QA = N_HEADS_A * HEAD_DIM
W_QI = N_IDX_HEADS * IDX_DIM
W_QB = N_HEADS_B * HEAD_DIM
W_KVB = 6 * N_KV_GROUPS_B * HEAD_DIM
W_GB = 3 * N_HEADS_B
W_MISC = N_IDX_HEADS + W_GB
W_VT = 5 * HEAD_DIM

LANES = 128
SUBLANES = 8
BF16_ROWS = 16
TQ = 256
TK = 256
QK_LOOKAHEAD = 3
PV_DELAY = 2
FAR_GROUP = 4
ACC_ROWS = HEAD_DIM + BF16_ROWS
CMP_WIN = 32
LOG2E = 1.4426950408889634
NEG_MASK = -2.0 ** 101
M_INIT = -2.0 ** 100
NEG_SCORE = -3.0e38
BIG_SCORE = 3.0e38
BIG_SCORE16 = 2.0 ** 127
COARSE_ITERS = 12
FINE_FIXED_ITERS = 9
MAX_SEARCH_ITERS = 400
TIE_CHECK_START = 14
COUNT_ROWS = 32
VMEM_LIMIT = 56 * 1024 * 1024

NT_DIMS = (((1,), (1,)), ((), ()))


def _cparams(*sem):
    return pltpu.CompilerParams(dimension_semantics=sem, vmem_limit_bytes=VMEM_LIMIT)


def _rms(x, g):
    return x * lax.rsqrt(jnp.mean(x * x, axis=-1, keepdims=True) + EPS) * g


def _inproj_kernel(x_ref, g_ref, w_ref, wT_ref, keys_ref, kcv_ref, ksw_ref, gate_ref,
                   qaT_ref, qiT_ref, miscT_ref, qbT_ref, vT_ref):
    q_scale = HEAD_DIM ** -0.5 * LOG2E
    for t in range(vT_ref.shape[0]):
        rs = slice(t * TK, (t + 1) * TK)
        h = _rms(x_ref[rs, :], g_ref[...]).astype(bf16)
        off = 0
        for ref in (keys_ref, kcv_ref, ksw_ref, gate_ref):
            width = ref.shape[-1]
            for c0 in range(0, width, 512):
                c1 = min(c0 + 512, width)
                y = jnp.dot(h, w_ref[:, off + c0:off + c1], preferred_element_type=f32)
                ref[rs, c0:c1] = y.astype(ref.dtype)
            off += width
        off = 0
        for ref, scale in ((qaT_ref, q_scale), (qiT_ref, None), (miscT_ref, None), (qbT_ref, q_scale),
                           (vT_ref.at[t], None)):
            n = ref.shape[0]
            y = lax.dot_general(wT_ref[off:off + n, :], h, NT_DIMS, preferred_element_type=f32)
            if scale is not None:
                y = y * scale
            if ref.shape[1] == TK:
                ref[...] = y.astype(ref.dtype)
            else:
                ref[:, rs] = y.astype(ref.dtype)
            off += n


def _inproj(x2, g, w_rm, w_t, sub_tiles=2):
    rows, d = x2.shape
    tm = sub_tiles * TK
    rm = ((LANES, bf16), (2 * LANES, f32), (2 * LANES, bf16), (2 * d, bf16))
    tr = ((W_QA, bf16), (W_QI, bf16), (W_MISC, f32), (W_QB, bf16))
    return pl.pallas_call(
        _inproj_kernel,
        grid=(rows // tm,),
        in_specs=[pl.BlockSpec((tm, d), lambda i: (i, 0)),
                  pl.BlockSpec((1, d), lambda i: (0, 0)),
                  pl.BlockSpec(w_rm.shape, lambda i: (0, 0)),
                  pl.BlockSpec(w_t.shape, lambda i: (0, 0))],
        out_specs=([pl.BlockSpec((tm, w), lambda i: (i, 0)) for w, _ in rm]
                   + [pl.BlockSpec((n, tm), lambda i: (0, i)) for n, _ in tr]
                   + [pl.BlockSpec((sub_tiles, W_VT, TK), lambda i: (i, 0, 0))]),
        out_shape=([jax.ShapeDtypeStruct((rows, w), dt) for w, dt in rm]
                   + [jax.ShapeDtypeStruct((n, rows), dt) for n, dt in tr]
                   + [jax.ShapeDtypeStruct((rows // TK, W_VT, TK), bf16)]),
        compiler_params=_cparams("parallel"),
    )(x2, g, w_rm, w_t)


def _loop_by_pairs(n, body, init):
    acc = lax.fori_loop(0, n // 2, lambda i, a: body(2 * i + 1, body(2 * i, a)), init)
    return lax.cond(n % 2 == 1, lambda a: body(n - 1, a), lambda a: a, acc)


def _count_ge(sc_ref, nkb, p):
    def body(kb, acc):
        hit = jnp.where(sc_ref[kb] >= p, 1.0, 0.0)
        return acc + jnp.sum(hit.reshape(TK // COUNT_ROWS, COUNT_ROWS, TQ), axis=0)
    acc = _loop_by_pairs(nkb, body, jnp.zeros((COUNT_ROWS, TQ), f32))
    return jnp.sum(acc, axis=0, keepdims=True)


def _count_ge16(sc16_ref, nkb, p16):
    one, nil = jnp.ones((), bf16), jnp.zeros((), bf16)
    def body(kb, acc):
        hit = jnp.where(sc16_ref[kb] >= p16, one, nil).reshape(TK // COUNT_ROWS, COUNT_ROWS, TQ)
        for i in range(TK // COUNT_ROWS):
            acc = acc + hit[i]
        return acc
    acc = _loop_by_pairs(nkb, body, jnp.zeros((COUNT_ROWS, TQ), bf16))
    return jnp.sum(acc.astype(f32), axis=0, keepdims=True)


def _coarse_bracket(sc16_ref, nkb, rmin, rmax, active, kf):
    rmax16 = rmax.astype(bf16).astype(f32)

    def pivot(lo16, hi16):
        mid = ((lo16 + jnp.minimum(hi16, rmax16)) * 0.5).astype(bf16).astype(f32)
        return mid, active & (mid > lo16) & (mid < hi16)

    def body(_, st):
        lo16, hi16 = st
        mid, ok = pivot(lo16, hi16)
        ge = _count_ge16(sc16_ref, nkb, mid.astype(bf16)) >= kf
        return jnp.where(ok & ge, mid, lo16), jnp.where(ok & jnp.logical_not(ge), mid, hi16)

    lo16, hi16 = lax.fori_loop(
        0, COARSE_ITERS, body, (rmin.astype(bf16).astype(f32), jnp.full_like(rmin, BIG_SCORE16)))
    below = jnp.maximum(jnp.abs(lo16) * 2.0 ** -7, 1e-37)
    return lo16 - below, jnp.where(hi16 < BIG_SCORE16, hi16, BIG_SCORE)


def _select_topk(sc_ref, sc16_ref, nkb, rmin, rmax, nvalid, k):
    kf = float(k)
    active = nvalid > kf
    lo_c, hi_c = _coarse_bracket(sc16_ref, nkb, rmin, rmax, active, kf)
    lo0 = jnp.where(active, lo_c, NEG_SCORE)
    cl0 = jnp.where(active, nvalid, kf)
    hi0 = jnp.where(active, hi_c, BIG_SCORE)
    zero = jnp.zeros_like(lo0)

    def pending(cl, tie):
        return (cl != kf) & (tie == 0.0)

    def any_pending(st):
        return jnp.max(jnp.where(pending(st[3], st[5]), 1.0, 0.0)) > 0.0

    def bisect(st):
        it, lo, hi, cl, ch, tie = st
        pend = pending(cl, tie)
        hie = jnp.minimum(hi, rmax)
        p = lo + (hie - lo) * 0.5
        p = jnp.where(p > lo, p, hie)
        cnt = _count_ge(sc_ref, nkb, p)
        ge = cnt >= kf
        up_lo = pend & ge
        up_hi = pend & jnp.logical_not(ge)
        lo = jnp.where(up_lo, p, lo)
        cl = jnp.where(up_lo, cnt, cl)
        hi = jnp.where(up_hi, p, hi)
        ch = jnp.where(up_hi, cnt, ch)
        return it + 1, lo, hi, cl, ch, tie

    def tie_check(st):
        it, lo, hi, cl, ch, tie = st
        def scan(kb, c):
            dmin, dmax = c
            s = sc_ref[kb]
            dmin = jnp.minimum(dmin, jnp.min(jnp.where(s >= lo, s, BIG_SCORE), axis=0, keepdims=True))
            dmax = jnp.maximum(dmax, jnp.max(jnp.where(s < hi, s, NEG_SCORE), axis=0, keepdims=True))
            return dmin, dmax
        dmin, dmax = lax.fori_loop(
            0, nkb, scan, (jnp.full((1, TQ), BIG_SCORE, f32), jnp.full((1, TQ), NEG_SCORE, f32)))
        new_tie = pending(cl, tie) & (dmin == dmax)
        return it, jnp.where(new_tie, dmin, lo), hi, cl, ch, jnp.where(new_tie, 1.0, tie)

    st = (jnp.int32(0), lo0, hi0, cl0, zero, zero)
    st = lax.fori_loop(0, FINE_FIXED_ITERS, lambda _, st: bisect(st), st)
    st = lax.while_loop(lambda st: any_pending(st) & (st[0] < TIE_CHECK_START), bisect, st)
    st = lax.while_loop(lambda st: any_pending(st) & (st[0] < MAX_SEARCH_ITERS),
                        lambda st: bisect(tie_check(st)), st)
    _, lo, hi, cl, ch, tie = st
    any_tie = jnp.max(tie) > 0.0

    @pl.when(any_tie)
    def _():
        r = lax.broadcasted_iota(jnp.int32, (TK, TK), 0)
        c = lax.broadcasted_iota(jnp.int32, (TK, TK), 1)
        tril = jnp.where(c <= r, 1.0, 0.0).astype(bf16)
        above = lax.fori_loop(
            0, nkb, lambda kb, n: n + jnp.sum(jnp.where(sc_ref[kb] > lo, 1.0, 0.0), axis=0, keepdims=True), zero)
        need = kf - above

        def body(kb, run):
            s = sc_ref[kb]
            eq = s == lo
            eqf = jnp.where(eq, 1.0, 0.0)
            cum = run + jnp.dot(tril, eqf.astype(bf16), preferred_element_type=f32)
            sel_tie = jnp.where((s > lo) | (eq & (cum <= need)), 1.0, -1.0)
            sel_all = jnp.where(s >= lo, 1.0, -1.0)
            sc_ref[kb] = jnp.where(tie > 0.0, sel_tie, sel_all)
            return run + jnp.sum(eqf, axis=0, keepdims=True)

        lax.fori_loop(0, nkb, body, zero)

    return jnp.where(any_tie, 0.0, lo)


def _rank_select(score, k):
    n, width = score.shape
    tiles = [score[v * SUBLANES:(v + 1) * SUBLANES] for v in range(n // SUBLANES)]
    ridx = lax.broadcasted_iota(jnp.int32, (SUBLANES, width), 0)
    ranks = [jnp.zeros((SUBLANES, width), f32) for _ in tiles]
    for i in range(n):
        si = jnp.broadcast_to(tiles[i // SUBLANES][i % SUBLANES:i % SUBLANES + 1], (SUBLANES, width))
        for v, x in enumerate(tiles):
            if v * SUBLANES > i:
                inc = jnp.where(si >= x, 1.0, 0.0)
            elif (v + 1) * SUBLANES <= i:
                inc = jnp.where(si > x, 1.0, 0.0)
            else:
                inc = jnp.where(ridx > i % SUBLANES, jnp.where(si >= x, 1.0, 0.0), jnp.where(si > x, 1.0, 0.0))
            ranks[v] = ranks[v] + inc
    flags = [jnp.where(r < float(k), 1.0, 0.0) for r in ranks]
    flags.append(jnp.zeros((LANES - n, width), f32))
    return jnp.concatenate(flags, axis=0)


Stream = collections.namedtuple("Stream", "q keys vT madd bias slot")


def _attn_init(m_ref, acc_ref):
    m_ref[...] = jnp.full(m_ref.shape, M_INIT, f32)
    acc_ref[...] = jnp.zeros(acc_ref.shape, f32)


def _attn_streams(streams, m_ref, acc_ref):
    def logits(i):
        st = streams[i]
        s = jnp.dot(st.keys, st.q, preferred_element_type=f32).astype(bf16)
        if st.bias is not None:
            s = s + st.bias
        if st.madd is not None:
            s = s + st.madd
        return s

    def values(i, alpha, p):
        st = streams[i]
        acc_ref[st.slot] = alpha * acc_ref[st.slot] + jnp.dot(st.vT, p, preferred_element_type=f32)

    n = len(streams)
    pending = {i: logits(i) for i in range(min(QK_LOOKAHEAD, n))}
    deferred = []
    for i in range(n):
        if i + QK_LOOKAHEAD < n:
            pending[i + QK_LOOKAHEAD] = logits(i + QK_LOOKAHEAD)
        s = pending.pop(i)
        slot = streams[i].slot
        m_old = m_ref[slot]
        m_new = jnp.maximum(m_old, jnp.max(s, axis=0, keepdims=True).astype(f32))
        alpha = jnp.exp2(m_old - m_new)
        p = jnp.exp2(s - m_new.astype(bf16))
        m_ref[slot] = m_new
        deferred.append((i, alpha, p))
        if len(deferred) > PV_DELAY:
            values(*deferred.pop(0))
    for d in deferred:
        values(*d)


def _attn_out(slot, acc_ref):
    acc = acc_ref[slot]
    return acc[0:HEAD_DIM] * (1.0 / jnp.maximum(acc[HEAD_DIM:HEAD_DIM + 1], 1e-30))


def _with_ones(vT):
    return jnp.concatenate([vT, jnp.ones((BF16_ROWS, vT.shape[1]), vT.dtype)], axis=0)


def _mask_add(mask):
    return jnp.where(mask, 0.0, NEG_MASK).astype(bf16)


def _pad_rows(x, start):
    parts = []
    if start:
        parts.append(jnp.zeros((start, x.shape[1]), x.dtype))
    parts.append(x)
    if LANES - start - x.shape[0]:
        parts.append(jnp.zeros((LANES - start - x.shape[0], x.shape[1]), x.dtype))
    return jnp.concatenate(parts, axis=0)


def _far_blocks(n_far, run):
    def body(i, _):
        run([FAR_GROUP * i + u for u in range(FAR_GROUP)])
        return 0
    lax.fori_loop(0, n_far // FAR_GROUP, body, 0)
    rem = n_far % FAR_GROUP
    start = n_far - rem
    size = FAR_GROUP // 2
    while size >= 1:
        @pl.when((rem & size) != 0)
        def _(start=start, size=size):
            run([start + u for u in range(size)])
        start = start + (rem & size)
        size //= 2


def _dsa_kernel(qiT_ref, miscT_ref, keys_ref, qaT_ref, vT_ref, bias_ref, o_ref,
                sc_ref, sc16_ref, qi_ref, qa_ref, oT_ref, m_ref, acc_ref, *, k_top):
    j = pl.program_id(1)
    w_scale = (N_IDX_HEADS ** -0.5) * (IDX_DIM ** -0.5)
    w = miscT_ref[0:N_IDX_HEADS, :] * w_scale
    for h in range(N_IDX_HEADS):
        qi_ref[h] = _pad_rows(qiT_ref[h * IDX_DIM:(h + 1) * IDX_DIM, :], HEAD_DIM)
    for h in range(N_HEADS_A):
        qa_ref[h] = _pad_rows(qaT_ref[h * HEAD_DIM:(h + 1) * HEAD_DIM, :], 0)
    krow = lax.broadcasted_iota(jnp.int32, (TK, TQ), 0)
    qcol = lax.broadcasted_iota(jnp.int32, (TK, TQ), 1)
    causal = krow <= qcol

    def keys_blk(kb):
        return keys_ref[pl.ds(pl.multiple_of(kb * TK, TK), TK), :]

    def score_blk(kb):
        keys = keys_blk(kb)
        acc = None
        for h in range(N_IDX_HEADS):
            sh = jnp.dot(keys, qi_ref[h], preferred_element_type=f32)
            term = w[h:h + 1, :] * jnp.maximum(sh, 0.0)
            acc = term if acc is None else acc + term
        return acc

    def far_score(kb, carry):
        mn, mx = carry
        a = score_blk(kb)
        sc_ref[kb] = a
        sc16_ref[kb] = a.astype(bf16)
        return (jnp.minimum(mn, jnp.min(a, axis=0, keepdims=True)),
                jnp.maximum(mx, jnp.max(a, axis=0, keepdims=True)))

    def diag_score(carry):
        mn, mx = carry
        a = score_blk(j)
        a_diag = jnp.where(causal, a, NEG_SCORE)
        sc_ref[j] = a_diag
        sc16_ref[j] = a_diag.astype(bf16)
        return (jnp.minimum(mn, jnp.min(jnp.where(causal, a, BIG_SCORE), axis=0, keepdims=True)),
                jnp.maximum(mx, jnp.max(a_diag, axis=0, keepdims=True)))

    def far_group(start, size, c):
        for u in range(size):
            c = far_score(start + u, c)
        return c

    stats = lax.fori_loop(0, j // FAR_GROUP, lambda i, c: far_group(FAR_GROUP * i, FAR_GROUP, c),
                          (jnp.full((1, TQ), BIG_SCORE, f32), jnp.full((1, TQ), NEG_SCORE, f32)))
    rem = j % FAR_GROUP
    start = j - rem
    size = FAR_GROUP // 2
    while size >= 2:
        stats = lax.cond((rem & size) != 0, functools.partial(far_group, start, size), lambda c: c, stats)
        start = start + (rem & size)
        size //= 2
    rmin, rmax = lax.cond((rem & 1) != 0, lambda c: diag_score(far_score(j - 1, c)), diag_score, stats)
    nvalid = (j * TQ + qcol[0:1, :] + 1).astype(f32)
    lo = _select_topk(sc_ref, sc16_ref, j + 1, rmin, rmax, nvalid, k_top)

    _attn_init(m_ref, acc_ref)

    def block_streams(kb, mask, bias_rows):
        keys, vT, madd = keys_blk(kb), _with_ones(vT_ref[kb]), _mask_add(mask)
        return [Stream(qa_ref[h], keys, vT, madd,
                       None if bias_rows is None else bias_ref[h, bias_rows, :], h) for h in range(N_HEADS_A)]

    def far(kb):
        return block_streams(kb, sc_ref[kb] >= lo, None)

    _far_blocks(jnp.maximum(j - 1, 0),
                lambda kbs: _attn_streams([st for kb in kbs for st in far(kb)], m_ref, acc_ref))
    diag_mask = (sc_ref[j] >= lo) & causal

    @pl.when(j >= 1)
    def _():
        _attn_streams(block_streams(j - 1, sc_ref[j - 1] >= lo, slice(0, TK))
                      + block_streams(j, diag_mask, slice(TK, 2 * TK)), m_ref, acc_ref)

    @pl.when(j == 0)
    def _():
        _attn_streams(block_streams(j, diag_mask, slice(TK, 2 * TK)), m_ref, acc_ref)

    for h in range(N_HEADS_A):
        oT_ref[h * HEAD_DIM:(h + 1) * HEAD_DIM, :] = _attn_out(h, acc_ref)
    o_ref[...] = oT_ref[...].T.astype(o_ref.dtype)


def _attn_scratch(n_slots):
    return [pltpu.VMEM((n_slots, 1, TQ), f32), pltpu.VMEM((n_slots, ACC_ROWS, TQ), f32)]


def _dsa(qiT, miscT, keys, qaT, vT_blocks, biasT, B, S, k_top):
    nq = S // TQ
    nkb = S // TK
    col_spec = lambda n: pl.BlockSpec((n, TQ), lambda b, j: (0, b * nq + j))
    return pl.pallas_call(
        functools.partial(_dsa_kernel, k_top=k_top),
        grid=(B, nq),
        in_specs=[col_spec(W_QI), col_spec(W_MISC),
                  pl.BlockSpec((S, LANES), lambda b, j: (b, 0)),
                  col_spec(W_QA),
                  pl.BlockSpec((None, nkb, HEAD_DIM, TK), lambda b, j: (b, 0, 4, 0)),
                  pl.BlockSpec(biasT.shape, lambda b, j: (0, 0, 0))],
        out_specs=pl.BlockSpec((TQ, W_QA), lambda b, j: (b * nq + j, 0)),
        out_shape=jax.ShapeDtypeStruct((B * S, W_QA), bf16),
        scratch_shapes=[pltpu.VMEM((nkb, TK, TQ), f32),
                        pltpu.VMEM((nkb, TK, TQ), bf16),
                        pltpu.VMEM((N_IDX_HEADS, LANES, TQ), bf16),
                        pltpu.VMEM((N_HEADS_A, LANES, TQ), bf16),
                        pltpu.VMEM((W_QA, TQ), f32)] + _attn_scratch(N_HEADS_A),
        compiler_params=_cparams("parallel", "parallel"),
    )(qiT, miscT, keys, qaT, vT_blocks, biasT)


def _compress_kernel(xk_ref, xv_ref, pe_ref, w1_ref, w1tok_ref, w2bd_ref, w2bdT_ref, ck_ref, cvT_ref):
    nc = ck_ref.shape[0]
    G = N_KV_GROUPS_B

    def hidden(kind):
        x_ref = (xk_ref, xv_ref)[kind]
        const = jnp.dot(pe_ref[kind], w1_ref[kind], preferred_element_type=f32)[0:1]
        ab = jnp.zeros((nc, 2 * G * CMP_HIDDEN), f32)
        for i in range(CMP_STRIDE):
            tok = x_ref[pl.ds(i, nc, stride=CMP_STRIDE), :]
            ab = ab + jnp.dot(tok.astype(bf16), w1tok_ref[kind, i], preferred_element_type=f32)
        hs = []
        for g in range(G):
            first = ab[:, g * CMP_HIDDEN:(g + 1) * CMP_HIDDEN]
            second = pltpu.roll(ab[:, (G + g) * CMP_HIDDEN:(G + g + 1) * CMP_HIDDEN], nc - 1, 0)
            hs.append(jax.nn.gelu(first + second + const).astype(bf16))
        return jnp.concatenate(hs, axis=1)

    ck_ref[...] = jnp.dot(hidden(0), w2bd_ref[...], preferred_element_type=f32).astype(ck_ref.dtype)
    cvT_ref[...] = lax.dot_general(w2bdT_ref[...], hidden(1), NT_DIMS,
                                   preferred_element_type=f32).astype(cvT_ref.dtype)


def _compress(kcv, pe8, w1, w1tok, w2bd_k, w2bdT_v, B, S):
    nc = S // CMP_STRIDE
    cs = lambda a: pl.BlockSpec(a.shape, lambda b: (0,) * a.ndim)
    return pl.pallas_call(
        _compress_kernel,
        grid=(B,),
        in_specs=[pl.BlockSpec((S, LANES), lambda b: (b, 0)),
                  pl.BlockSpec((S, LANES), lambda b: (b, 1)),
                  cs(pe8), cs(w1), cs(w1tok), cs(w2bd_k), cs(w2bdT_v)],
        out_specs=[pl.BlockSpec((None, nc, LANES), lambda b: (b, 0, 0)),
                   pl.BlockSpec((None, LANES, nc), lambda b: (b, 0, 0))],
        out_shape=[jax.ShapeDtypeStruct((B, nc, LANES), bf16), jax.ShapeDtypeStruct((B, LANES, nc), bf16)],
        compiler_params=_cparams("parallel"),
    )(kcv, kcv, pe8, w1, w1tok, w2bd_k, w2bdT_v)


def _nsa_kernel(qbT_ref, miscT_ref, ck_ref, cvT_ref, ksw_ref, vT_ref, bias_ref, bcmp_ref, ovT_ref, e3_ref,
                o_ref, qb_ref, s_ref, oT_ref, m_ref, acc_ref, *, n_sel):
    j = pl.program_id(1)
    nc = ck_ref.shape[0]
    n_s = ovT_ref.shape[0]
    R = HEADS_PER_GROUP
    G = N_KV_GROUPS_B
    H = N_HEADS_B
    for h in range(H):
        qb_ref[h] = _pad_rows(qbT_ref[h * HEAD_DIM:(h + 1) * HEAD_DIM, :], (h // R) * HEAD_DIM)
    krow = lax.broadcasted_iota(jnp.int32, (TK, TQ), 0)
    qcol = lax.broadcasted_iota(jnp.int32, (TK, TQ), 1)
    causal = krow <= qcol
    t_c = j * TQ + lax.broadcasted_iota(jnp.int32, (nc, TQ), 1)
    c_c = lax.broadcasted_iota(jnp.int32, (nc, TQ), 0)
    mask_c = c_c * CMP_STRIDE + (CMP_BLOCK - 1) <= t_c
    blk = lax.broadcasted_iota(jnp.int32, (n_s, TQ), 0)
    cur = (j * TQ + lax.broadcasted_iota(jnp.int32, (n_s, TQ), 1)) // SLC_BLOCK
    valid = blk <= cur
    forced = (blk == 0) | (blk == cur) | (blk == cur - 1)
    gates = jax.nn.sigmoid(miscT_ref[N_IDX_HEADS:N_IDX_HEADS + W_GB, :])
    win_blocks = WINDOW // TK
    cmp_per_q = TQ // CMP_STRIDE
    half = CMP_WIN // 2

    for h in range(H):
        s_ref[h] = jnp.dot(ck_ref[...], qb_ref[h], preferred_element_type=f32)

    @pl.when(j == 0)
    def _():
        for h in range(H):
            s_ref[h, 0:half, :] += bcmp_ref[h, half:CMP_WIN, :]

    @pl.when(j > 0)
    def _():
        rows = pl.ds(pl.multiple_of(j * cmp_per_q - half, half), CMP_WIN)
        for h in range(H):
            s_ref[h, rows, :] += bcmp_ref[h]

    o_c = []
    psum = [jnp.zeros((nc, TQ), f32) for _ in range(G)]
    for h in range(H):
        g = h // R
        s = jnp.where(mask_c, s_ref[h], NEG_MASK)
        m = jnp.max(s, axis=0, keepdims=True)
        e = jnp.where(mask_c, jnp.exp2(s - m), 0.0)
        p = e * (1.0 / jnp.maximum(jnp.sum(e, axis=0, keepdims=True), 1e-30))
        psum[g] = psum[g] + p
        o_c.append(jnp.dot(cvT_ref[g * HEAD_DIM:(g + 1) * HEAD_DIM, :], p.astype(bf16),
                           preferred_element_type=f32))

    selT = []
    for g in range(G):
        p_hi = psum[g].astype(bf16)
        p_lo = (psum[g] - p_hi.astype(f32)).astype(bf16)
        imp = (jnp.dot(ovT_ref[...], p_hi, preferred_element_type=f32)
               + jnp.dot(ovT_ref[...], p_lo, preferred_element_type=f32))
        score = jnp.where(valid, jnp.where(forced, FORCED_SCORE, imp), NEG_SCORE)
        selT.append(_rank_select(score, n_sel).astype(bf16))

    _attn_init(m_ref, acc_ref)

    def branch_streams(kb, lanes, v_row0, masks, bias_rows, slot0):
        keys = ksw_ref[pl.ds(pl.multiple_of(kb * TK, TK), TK), lanes]
        out = []
        for g in range(G):
            vT = _with_ones(vT_ref[kb, v_row0 + g * HEAD_DIM:v_row0 + (g + 1) * HEAD_DIM, :])
            madd = None if masks[g] is None else _mask_add(masks[g])
            for h in range(g * R, (g + 1) * R):
                out.append(Stream(qb_ref[h], keys, vT, madd,
                                  None if bias_rows is None else bias_ref[h, bias_rows, :], slot0 + h))
        return out

    def tok_masks(kb, extra=None):
        ms = [jnp.dot(e3_ref[kb], selT[g], preferred_element_type=f32) > 0.5 for g in range(G)]
        return ms if extra is None else [m & extra for m in ms]

    def sel_streams(kb, extra=None, bias_rows=None):
        return branch_streams(kb, slice(0, LANES), 0, tok_masks(kb, extra), bias_rows, 0)

    def win_streams(kb, mask=None, bias_rows=None):
        return branch_streams(kb, slice(LANES, 2 * LANES), G * HEAD_DIM, [mask] * G, bias_rows, H)

    _far_blocks(jnp.maximum(j - 1, 0),
                lambda kbs: _attn_streams([st for kb in kbs for st in sel_streams(kb)], m_ref, acc_ref))

    prev_rows, diag_rows = slice(0, TK), slice(TK, 2 * TK)

    def near_streams(n_before):
        out = []
        for d in range(min(n_before, win_blocks), 1, -1):
            out += win_streams(j - d, (krow > qcol) if d == win_blocks else None)
        if n_before >= 1:
            out += sel_streams(j - 1, None, prev_rows)
            out += win_streams(j - 1, (krow > qcol) if win_blocks == 1 else None, prev_rows)
        return out + sel_streams(j, causal, diag_rows) + win_streams(j, causal, diag_rows)

    for n_before in range(win_blocks + 1):
        @pl.when((j >= n_before) if n_before == win_blocks else (j == n_before))
        def _(n_before=n_before):
            _attn_streams(near_streams(n_before), m_ref, acc_ref)

    for h in range(H):
        oT_ref[h * HEAD_DIM:(h + 1) * HEAD_DIM, :] = (
            gates[h:h + 1, :] * o_c[h]
            + gates[H + h:H + h + 1, :] * _attn_out(h, acc_ref)
            + gates[2 * H + h:2 * H + h + 1, :] * _attn_out(H + h, acc_ref))
    o_ref[...] = oT_ref[...].T.astype(o_ref.dtype)


def _nsa(qbT, miscT, ck, cvT, ksw, vT_blocks, biasT, bcmpT, ovT, e3T, B, S, n_sel):
    nq = S // TQ
    nkb = S // TK
    nc = ck.shape[1]
    col_spec = lambda n: pl.BlockSpec((n, TQ), lambda b, j: (0, b * nq + j))
    const = lambda a: pl.BlockSpec(a.shape, lambda b, j: (0,) * a.ndim)
    return pl.pallas_call(
        functools.partial(_nsa_kernel, n_sel=n_sel),
        grid=(B, nq),
        in_specs=[col_spec(W_QB), col_spec(W_MISC),
                  pl.BlockSpec((None, nc, LANES), lambda b, j: (b, 0, 0)),
                  pl.BlockSpec((None, LANES, nc), lambda b, j: (b, 0, 0)),
                  pl.BlockSpec((S, 2 * LANES), lambda b, j: (b, 0)),
                  pl.BlockSpec((None, nkb, 4 * HEAD_DIM, TK), lambda b, j: (b, 0, 0, 0)),
                  const(biasT), const(bcmpT), const(ovT), const(e3T)],
        out_specs=pl.BlockSpec((TQ, W_QB), lambda b, j: (b * nq + j, 0)),
        out_shape=jax.ShapeDtypeStruct((B * S, W_QB), bf16),
        scratch_shapes=[pltpu.VMEM((N_HEADS_B, LANES, TQ), bf16),
                        pltpu.VMEM((N_HEADS_B, nc, TQ), f32),
                        pltpu.VMEM((W_QB, TQ), f32)] + _attn_scratch(2 * N_HEADS_B),
        compiler_params=_cparams("parallel", "parallel"),
    )(qbT, miscT, ck, cvT, ksw, vT_blocks, biasT, bcmpT, ovT, e3T)


def _mix_kernel(oa_ref, ob_ref, gate_ref, x_ref, wa_ref, wb_ref, wo_ref, x1_ref):
    d = x_ref.shape[-1]
    ga = jax.nn.sigmoid(gate_ref[:, :d].astype(f32))
    gb = jax.nn.sigmoid(gate_ref[:, d:].astype(f32))
    mix = (ga * jnp.dot(oa_ref[...], wa_ref[...], preferred_element_type=f32)
           + gb * jnp.dot(ob_ref[...], wb_ref[...], preferred_element_type=f32))
    x1_ref[...] = x_ref[...] + jnp.dot(mix.astype(bf16), wo_ref[...], preferred_element_type=f32)


def _mix(oa, ob, gate, x2, wa, wb, wo, tm=512):
    rows, d = x2.shape
    rs = lambda w: pl.BlockSpec((tm, w), lambda i: (i, 0))
    cs = lambda a: pl.BlockSpec(a.shape, lambda i: (0, 0))
    return pl.pallas_call(
        _mix_kernel,
        grid=(rows // tm,),
        in_specs=[rs(W_QA), rs(W_QB), rs(2 * d), rs(d), cs(wa), cs(wb), cs(wo)],
        out_specs=rs(d),
        out_shape=jax.ShapeDtypeStruct((rows, d), f32),
        compiler_params=_cparams("parallel"),
    )(oa, ob, gate, x2, wa, wb, wo)


def _mlp_kernel(x1_ref, g_ref, w1_ref, w2_ref, gf_ref, o_ref, *, chunk, sub_rows):
    for r0 in range(0, x1_ref.shape[0], sub_rows):
        x1 = x1_ref[r0:r0 + sub_rows, :]
        h = _rms(x1, g_ref[...]).astype(bf16)
        acc = jnp.zeros(x1.shape, f32)
        for c0 in range(0, w1_ref.shape[1], chunk):
            u = jnp.dot(h, w1_ref[:, c0:c0 + chunk], preferred_element_type=f32)
            u = jnp.square(jnp.maximum(u, 0.0)).astype(bf16)
            acc = acc + jnp.dot(u, w2_ref[c0:c0 + chunk, :], preferred_element_type=f32)
        o_ref[r0:r0 + sub_rows, :] = _rms(x1 + acc, gf_ref[...])


def _mlp(x1, g, w1, w2, gf, tm=512, chunk=512, sub_rows=256):
    rows, d = x1.shape
    rs = pl.BlockSpec((tm, d), lambda i: (i, 0))
    cs = lambda a: pl.BlockSpec(a.shape, lambda i: (0, 0))
    ws = lambda a: pl.BlockSpec(a.shape, lambda i: (0, 0), pipeline_mode=pl.Buffered(1))
    return pl.pallas_call(
        functools.partial(_mlp_kernel, chunk=chunk, sub_rows=sub_rows),
        grid=(rows // tm,),
        in_specs=[rs, cs(g), ws(w1), ws(w2), cs(gf)],
        out_specs=rs,
        out_shape=jax.ShapeDtypeStruct((rows, d), f32),
        compiler_params=_cparams("parallel"),
    )(x1, g, w1, w2, gf)


def _t5_bucket_np(dist):
    n = np.maximum(dist, 0)
    max_exact = N_BUCKETS // 2
    nf = np.maximum(n, 1).astype(np.float32)
    large = max_exact + (np.log(nf / np.float32(max_exact)) / np.float32(math.log(MAX_DISTANCE / max_exact))
                         * np.float32(N_BUCKETS - max_exact)).astype(np.int32)
    large = np.minimum(large, N_BUCKETS - 1)
    return np.where(n < max_exact, n, large)


def _bias_tables(rel_bias):
    shifted = (rel_bias.astype(f32) - rel_bias[N_BUCKETS - 1].astype(f32)) * LOG2E

    def lookup(dist, tab):
        bucket = np.where((dist >= 0) & (dist < MAX_DISTANCE), _t5_bucket_np(dist), N_BUCKETS - 1)
        onehot = (jnp.asarray(bucket, jnp.int32)[..., None] == jnp.arange(N_BUCKETS)).astype(f32)
        return jnp.einsum('kqb,bh->hkq', onehot, tab, precision=lax.Precision.HIGHEST)

    q = np.arange(TQ)[None, :]
    near = lookup(q + TK - np.arange(2 * TK)[:, None], shifted).astype(bf16)
    m = np.arange(CMP_WIN)[:, None]
    cmp = lookup(q - (CMP_BLOCK - 1) - CMP_STRIDE * (m - CMP_WIN // 2), shifted[:, N_HEADS_A:])
    return near[:N_HEADS_A], near[N_HEADS_A:], cmp


def _static_tables(S, nc):
    n_s = S // SLC_BLOCK
    cmp_start = np.arange(nc)[None, :] * CMP_STRIDE
    slc_start = np.arange(n_s)[:, None] * SLC_BLOCK
    ovT = np.clip(np.minimum(cmp_start + CMP_BLOCK, slc_start + SLC_BLOCK)
                  - np.maximum(cmp_start, slc_start), 0, None).astype(np.float32) / CMP_BLOCK
    nkb = S // TK
    tok_blk = (np.arange(nkb)[:, None, None] * TK + np.arange(TK)[None, :, None]) // SLC_BLOCK
    e3T = (np.arange(LANES)[None, None, :] == tok_blk).astype(np.float32)
    return jnp.asarray(ovT, bf16), jnp.asarray(e3T, bf16)


def _compress_weights(w1, w2):
    z = jnp.zeros((HEAD_DIM, CMP_HIDDEN), w1.dtype)
    first = w1[:CMP_STRIDE * HEAD_DIM].reshape(CMP_STRIDE, HEAD_DIM, CMP_HIDDEN)
    second = w1[CMP_STRIDE * HEAD_DIM:].reshape(CMP_STRIDE, HEAD_DIM, CMP_HIDDEN)
    zz = jnp.broadcast_to(z, first.shape)
    tok = jnp.concatenate([jnp.concatenate([first, zz, second, zz], axis=2),
                           jnp.concatenate([zz, first, zz, second], axis=2)], axis=1)
    z2 = jnp.zeros((CMP_HIDDEN, HEAD_DIM), w2.dtype)
    return tok, jnp.block([[w2, z2], [z2, w2]])


def kernel(x, norm_mix, w_in, cmp_pe_k, cmp_w1_k, cmp_w2_k, cmp_pe_v, cmp_w1_v, cmp_w2_v, rel_bias,
           w_branch_a, w_branch_b, w_out, norm_mlp, w_mlp_in, w_mlp_out, norm_final):
    B, S, D = x.shape
    assert norm_mix.shape[0] == 1 and S % TQ == 0 and (S // CMP_STRIDE) % LANES == 0
    assert WINDOW % TK == 0 and (S // SLC_BLOCK) % SUBLANES == 0 and S // SLC_BLOCK <= LANES
    assert S // COUNT_ROWS <= 256
    G = N_KV_GROUPS_B
    rows = B * S
    nkb = S // TK
    nc = S // CMP_STRIDE
    k_top = min(TOPK_TOKENS, S // 4)
    n_sel = min(N_SLC_BLOCKS, S // SLC_BLOCK)
    x2 = x.reshape(rows, D)

    w = w_in[0]
    o_ka = W_QA
    o_va = o_ka + HEAD_DIM
    o_qi = o_va + HEAD_DIM
    o_ki = o_qi + W_QI
    o_wi = o_ki + IDX_DIM
    o_qb = o_wi + N_IDX_HEADS
    o_kvb = o_qb + W_QB
    o_gb = o_kvb + W_KVB
    o_gate = o_gb + W_GB
    kv = lambda kind: w[:, o_kvb + kind * G * HEAD_DIM:o_kvb + (kind + 1) * G * HEAD_DIM]
    pad = jnp.zeros((D, LANES - HEAD_DIM - IDX_DIM), w.dtype)
    w_rm = jnp.concatenate([w[:, o_ka:o_va], w[:, o_ki:o_wi], pad, kv(0), kv(1), kv(2), kv(4),
                            w[:, o_gate:]], axis=1).astype(bf16)
    w_t = jnp.concatenate([w[:, :o_ka], w[:, o_qi:o_ki], w[:, o_wi:o_qb], w[:, o_gb:o_gate],
                           w[:, o_qb:o_kvb], kv(3), kv(5), w[:, o_va:o_qi]], axis=1).T.astype(bf16)
    keys_a, kcv, ksw, gate, qaT, qiT, miscT, qbT, vT = _inproj(x2, norm_mix[0][None], w_rm, w_t)
    vT_blocks = vT.reshape(B, nkb, W_VT, TK)

    bias_a, bias_b, bcmp = _bias_tables(rel_bias)
    ovT, e3T = _static_tables(S, nc)

    o_a = _dsa(qiT, miscT, keys_a, qaT, vT_blocks, bias_a, B, S, k_top)

    chunk_w = CMP_STRIDE * HEAD_DIM
    pe = jnp.stack([cmp_pe_k[0], cmp_pe_v[0]]).reshape(2, 1, 2 * chunk_w)
    pe8 = jnp.broadcast_to(pe, (2, SUBLANES, 2 * chunk_w)).astype(bf16)
    w1 = jnp.stack([cmp_w1_k[0], cmp_w1_v[0]]).astype(bf16)
    tok_k, w2bd_k = _compress_weights(cmp_w1_k[0], cmp_w2_k[0])
    tok_v, w2bd_v = _compress_weights(cmp_w1_v[0], cmp_w2_v[0])
    ck, cvT = _compress(kcv, pe8, w1, jnp.stack([tok_k, tok_v]).astype(bf16),
                        w2bd_k.astype(bf16), w2bd_v.T.astype(bf16), B, S)

    o_b = _nsa(qbT, miscT, ck, cvT, ksw, vT_blocks, bias_b, bcmp, ovT, e3T, B, S, n_sel)

    x1 = _mix(o_a, o_b, gate, x2, w_branch_a[0].astype(bf16), w_branch_b[0].astype(bf16),
              w_out[0].astype(bf16))
    out = _mlp(x1, norm_mlp[0][None], w_mlp_in[0].astype(bf16), w_mlp_out[0].astype(bf16),
               norm_final[None])
    return out.reshape(B, S, D)
```

```python
import collections
import functools
import math

import numpy as np
import jax
import jax.numpy as jnp
from jax import lax
from jax.experimental import pallas as pl
from jax.experimental.pallas import tpu as pltpu

f32 = jnp.float32
bf16 = jnp.bfloat16

HEAD_DIM = 64
N_HEADS_A = 8
N_IDX_HEADS = 8
IDX_DIM = 32
TOPK_TOKENS = 256
N_HEADS_B = 8
N_KV_GROUPS_B = 2
HEADS_PER_GROUP = N_HEADS_B // N_KV_GROUPS_B
CMP_BLOCK = 32
CMP_STRIDE = 16
CMP_HIDDEN = 128
SLC_BLOCK = 64
N_SLC_BLOCKS = 16
WINDOW = 512
FORCED_SCORE = 1.0e4
N_BUCKETS = 32
MAX_DISTANCE = 128
EPS = 1e-6

W_QA = N_HEADS_A * HEAD_DIM
W_QI = N_IDX_HEADS * IDX_DIM
W_QB = N_HEADS_B * HEAD_DIM
W_KVB = 6 * N_KV_GROUPS_B * HEAD_DIM
W_GB = 3 * N_HEADS_B
W_MISC = N_IDX_HEADS + W_GB
W_VT = 5 * HEAD_DIM

LANES = 128
SUBLANES = 8
BF16_ROWS = 16
TQ = 256
TK = 256
QK_LOOKAHEAD = 3
PV_DELAY = 2
FAR_GROUP = 8
ACC_ROWS = HEAD_DIM + BF16_ROWS
CMP_WIN = 32
LOG2E = 1.4426950408889634
NEG_MASK = -2.0 ** 101
M_INIT = -2.0 ** 100
NEG_SCORE = -3.0e38
BIG_SCORE = 3.0e38
BIG_SCORE16 = 2.0 ** 127
COARSE_ITERS = 12
FINE_FIXED_ITERS = 9
MAX_SEARCH_ITERS = 400
TIE_CHECK_START = 14
COUNT_ROWS = 32
VMEM_LIMIT = 56 * 1024 * 1024

NT_DIMS = (((1,), (1,)), ((), ()))


def _cparams(*sem):
    return pltpu.CompilerParams(dimension_semantics=sem, vmem_limit_bytes=VMEM_LIMIT)


def _rms(x, g):
    return x * lax.rsqrt(jnp.mean(x * x, axis=-1, keepdims=True) + EPS) * g


def _inproj_kernel(x_ref, g_ref, w_ref, wT_ref, keys_ref, kcv_ref, ksw_ref, gate_ref,
                   qaT_ref, qiT_ref, miscT_ref, qbT_ref, vT_ref):
    q_scale = HEAD_DIM ** -0.5 * LOG2E
    for t in range(vT_ref.shape[0]):
        rs = slice(t * TK, (t + 1) * TK)
        h = _rms(x_ref[rs, :], g_ref[...]).astype(bf16)
        off = 0
        for ref in (keys_ref, kcv_ref, ksw_ref, gate_ref):
            width = ref.shape[-1]
            for c0 in range(0, width, 512):
                c1 = min(c0 + 512, width)
                y = jnp.dot(h, w_ref[:, off + c0:off + c1], preferred_element_type=f32)
                ref[rs, c0:c1] = y.astype(ref.dtype)
            off += width
        off = 0
        for ref, scale in ((qaT_ref, q_scale), (qiT_ref, None), (miscT_ref, None), (qbT_ref, q_scale),
                           (vT_ref.at[t], None)):
            n = ref.shape[0]
            y = lax.dot_general(wT_ref[off:off + n, :], h, NT_DIMS, preferred_element_type=f32)
            if scale is not None:
                y = y * scale
            if ref.shape[1] == TK:
                ref[...] = y.astype(ref.dtype)
            else:
                ref[:, rs] = y.astype(ref.dtype)
            off += n


def _inproj(x2, g, w_rm, w_t, sub_tiles=2):
    rows, d = x2.shape
    tm = sub_tiles * TK
    rm = ((LANES, bf16), (2 * LANES, f32), (2 * LANES, bf16), (2 * d, bf16))
    tr = ((W_QA, bf16), (W_QI, bf16), (W_MISC, f32), (W_QB, bf16))
    return pl.pallas_call(
        _inproj_kernel,
        grid=(rows // tm,),
        in_specs=[pl.BlockSpec((tm, d), lambda i: (i, 0)),
                  pl.BlockSpec((1, d), lambda i: (0, 0)),
                  pl.BlockSpec(w_rm.shape, lambda i: (0, 0)),
                  pl.BlockSpec(w_t.shape, lambda i: (0, 0))],
        out_specs=([pl.BlockSpec((tm, w), lambda i: (i, 0)) for w, _ in rm]
                   + [pl.BlockSpec((n, tm), lambda i: (0, i)) for n, _ in tr]
                   + [pl.BlockSpec((sub_tiles, W_VT, TK), lambda i: (i, 0, 0))]),
        out_shape=([jax.ShapeDtypeStruct((rows, w), dt) for w, dt in rm]
                   + [jax.ShapeDtypeStruct((n, rows), dt) for n, dt in tr]
                   + [jax.ShapeDtypeStruct((rows // TK, W_VT, TK), bf16)]),
        compiler_params=_cparams("parallel"),
    )(x2, g, w_rm, w_t)


def _loop_by_pairs(n, body, init):
    acc = lax.fori_loop(0, n // 2, lambda i, a: body(2 * i + 1, body(2 * i, a)), init)
    return lax.cond(n % 2 == 1, lambda a: body(n - 1, a), lambda a: a, acc)


def _count_ge(sc_ref, nkb, p):
    def body(kb, acc):
        hit = jnp.where(sc_ref[kb] >= p, 1.0, 0.0)
        return acc + jnp.sum(hit.reshape(TK // COUNT_ROWS, COUNT_ROWS, TQ), axis=0)
    acc = _loop_by_pairs(nkb, body, jnp.zeros((COUNT_ROWS, TQ), f32))
    return jnp.sum(acc, axis=0, keepdims=True)


def _count_ge16(sc16_ref, nkb, p16):
    one, nil = jnp.ones((), bf16), jnp.zeros((), bf16)
    def body(kb, acc):
        hit = jnp.where(sc16_ref[kb] >= p16, one, nil).reshape(TK // COUNT_ROWS, COUNT_ROWS, TQ)
        for i in range(TK // COUNT_ROWS):
            acc = acc + hit[i]
        return acc
    acc = _loop_by_pairs(nkb, body, jnp.zeros((COUNT_ROWS, TQ), bf16))
    return jnp.sum(acc.astype(f32), axis=0, keepdims=True)


def _coarse_bracket(sc16_ref, nkb, rmin, rmax, active, kf):
    rmax16 = rmax.astype(bf16).astype(f32)

    def pivot(lo16, hi16):
        mid = ((lo16 + jnp.minimum(hi16, rmax16)) * 0.5).astype(bf16).astype(f32)
        return mid, active & (mid > lo16) & (mid < hi16)

    def body(_, st):
        lo16, hi16 = st
        mid, ok = pivot(lo16, hi16)
        ge = _count_ge16(sc16_ref, nkb, mid.astype(bf16)) >= kf
        return jnp.where(ok & ge, mid, lo16), jnp.where(ok & jnp.logical_not(ge), mid, hi16)

    lo16, hi16 = lax.fori_loop(
        0, COARSE_ITERS, body, (rmin.astype(bf16).astype(f32), jnp.full_like(rmin, BIG_SCORE16)))
    below = jnp.maximum(jnp.abs(lo16) * 2.0 ** -7, 1e-37)
    return lo16 - below, jnp.where(hi16 < BIG_SCORE16, hi16, BIG_SCORE)


def _select_topk(sc_ref, sc16_ref, nkb, rmin, rmax, nvalid, k):
    kf = float(k)
    active = nvalid > kf
    lo_c, hi_c = _coarse_bracket(sc16_ref, nkb, rmin, rmax, active, kf)
    lo0 = jnp.where(active, lo_c, NEG_SCORE)
    cl0 = jnp.where(active, nvalid, kf)
    hi0 = jnp.where(active, hi_c, BIG_SCORE)
    zero = jnp.zeros_like(lo0)

    def pending(cl, tie):
        return (cl != kf) & (tie == 0.0)

    def any_pending(st):
        return jnp.max(jnp.where(pending(st[3], st[5]), 1.0, 0.0)) > 0.0

    def bisect(st):
        it, lo, hi, cl, ch, tie = st
        pend = pending(cl, tie)
        hie = jnp.minimum(hi, rmax)
        p = lo + (hie - lo) * 0.5
        p = jnp.where(p > lo, p, hie)
        cnt = _count_ge(sc_ref, nkb, p)
        ge = cnt >= kf
        up_lo = pend & ge
        up_hi = pend & jnp.logical_not(ge)
        lo = jnp.where(up_lo, p, lo)
        cl = jnp.where(up_lo, cnt, cl)
        hi = jnp.where(up_hi, p, hi)
        ch = jnp.where(up_hi, cnt, ch)
        return it + 1, lo, hi, cl, ch, tie

    def tie_check(st):
        it, lo, hi, cl, ch, tie = st
        def scan(kb, c):
            dmin, dmax = c
            s = sc_ref[kb]
            dmin = jnp.minimum(dmin, jnp.min(jnp.where(s >= lo, s, BIG_SCORE), axis=0, keepdims=True))
            dmax = jnp.maximum(dmax, jnp.max(jnp.where(s < hi, s, NEG_SCORE), axis=0, keepdims=True))
            return dmin, dmax
        dmin, dmax = lax.fori_loop(
            0, nkb, scan, (jnp.full((1, TQ), BIG_SCORE, f32), jnp.full((1, TQ), NEG_SCORE, f32)))
        new_tie = pending(cl, tie) & (dmin == dmax)
        return it, jnp.where(new_tie, dmin, lo), hi, cl, ch, jnp.where(new_tie, 1.0, tie)

    st = (jnp.int32(0), lo0, hi0, cl0, zero, zero)
    st = lax.fori_loop(0, FINE_FIXED_ITERS, lambda _, st: bisect(st), st)
    st = lax.while_loop(lambda st: any_pending(st) & (st[0] < TIE_CHECK_START), bisect, st)
    st = lax.while_loop(lambda st: any_pending(st) & (st[0] < MAX_SEARCH_ITERS),
                        lambda st: bisect(tie_check(st)), st)
    _, lo, hi, cl, ch, tie = st
    any_tie = jnp.max(tie) > 0.0

    @pl.when(any_tie)
    def _():
        r = lax.broadcasted_iota(jnp.int32, (TK, TK), 0)
        c = lax.broadcasted_iota(jnp.int32, (TK, TK), 1)
        tril = jnp.where(c <= r, 1.0, 0.0).astype(bf16)
        above = lax.fori_loop(
            0, nkb, lambda kb, n: n + jnp.sum(jnp.where(sc_ref[kb] > lo, 1.0, 0.0), axis=0, keepdims=True), zero)
        need = kf - above

        def body(kb, run):
            s = sc_ref[kb]
            eq = s == lo
            eqf = jnp.where(eq, 1.0, 0.0)
            cum = run + jnp.dot(tril, eqf.astype(bf16), preferred_element_type=f32)
            sel_tie = jnp.where((s > lo) | (eq & (cum <= need)), 1.0, -1.0)
            sel_all = jnp.where(s >= lo, 1.0, -1.0)
            sc_ref[kb] = jnp.where(tie > 0.0, sel_tie, sel_all)
            return run + jnp.sum(eqf, axis=0, keepdims=True)

        lax.fori_loop(0, nkb, body, zero)

    return jnp.where(any_tie, 0.0, lo)


def _rank_select(score, k):
    n, width = score.shape
    tiles = [score[v * SUBLANES:(v + 1) * SUBLANES] for v in range(n // SUBLANES)]
    ridx = lax.broadcasted_iota(jnp.int32, (SUBLANES, width), 0)
    ranks = [jnp.zeros((SUBLANES, width), f32) for _ in tiles]
    for i in range(n):
        si = jnp.broadcast_to(tiles[i // SUBLANES][i % SUBLANES:i % SUBLANES + 1], (SUBLANES, width))
        for v, x in enumerate(tiles):
            if v * SUBLANES > i:
                inc = jnp.where(si >= x, 1.0, 0.0)
            elif (v + 1) * SUBLANES <= i:
                inc = jnp.where(si > x, 1.0, 0.0)
            else:
                inc = jnp.where(ridx > i % SUBLANES, jnp.where(si >= x, 1.0, 0.0), jnp.where(si > x, 1.0, 0.0))
            ranks[v] = ranks[v] + inc
    flags = [jnp.where(r < float(k), 1.0, 0.0) for r in ranks]
    flags.append(jnp.zeros((LANES - n, width), f32))
    return jnp.concatenate(flags, axis=0)


Stream = collections.namedtuple("Stream", "q keys vT madd bias slot")


def _attn_init(m_ref, acc_ref):
    m_ref[...] = jnp.full(m_ref.shape, M_INIT, f32)
    acc_ref[...] = jnp.zeros(acc_ref.shape, f32)


def _attn_streams(streams, m_ref, acc_ref):
    def logits(i):
        st = streams[i]
        s = jnp.dot(st.keys, st.q, preferred_element_type=f32).astype(bf16)
        if st.bias is not None:
            s = s + st.bias
        if st.madd is not None:
            s = s + st.madd
        return s

    def values(i, alpha, p):
        st = streams[i]
        acc_ref[st.slot] = alpha * acc_ref[st.slot] + jnp.dot(st.vT, p, preferred_element_type=f32)

    n = len(streams)
    pending = {i: logits(i) for i in range(min(QK_LOOKAHEAD, n))}
    deferred = []
    for i in range(n):
        if i + QK_LOOKAHEAD < n:
            pending[i + QK_LOOKAHEAD] = logits(i + QK_LOOKAHEAD)
        s = pending.pop(i)
        slot = streams[i].slot
        m_old = m_ref[slot]
        m_new = jnp.maximum(m_old, jnp.max(s, axis=0, keepdims=True).astype(f32))
        alpha = jnp.exp2(m_old - m_new)
        p = jnp.exp2(s - m_new.astype(bf16))
        m_ref[slot] = m_new
        deferred.append((i, alpha, p))
        if len(deferred) > PV_DELAY:
            values(*deferred.pop(0))
    for d in deferred:
        values(*d)


def _attn_out(slot, acc_ref):
    acc = acc_ref[slot]
    return acc[0:HEAD_DIM] * (1.0 / jnp.maximum(acc[HEAD_DIM:HEAD_DIM + 1], 1e-30))


def _with_ones(vT):
    return jnp.concatenate([vT, jnp.ones((BF16_ROWS, vT.shape[1]), vT.dtype)], axis=0)


def _mask_add(mask):
    return jnp.where(mask, 0.0, NEG_MASK).astype(bf16)


def _pad_rows(x, start):
    parts = []
    if start:
        parts.append(jnp.zeros((start, x.shape[1]), x.dtype))
    parts.append(x)
    if LANES - start - x.shape[0]:
        parts.append(jnp.zeros((LANES - start - x.shape[0], x.shape[1]), x.dtype))
    return jnp.concatenate(parts, axis=0)


def _far_blocks(n_far, run):
    def body(i, _):
        run([FAR_GROUP * i + u for u in range(FAR_GROUP)])
        return 0
    lax.fori_loop(0, n_far // FAR_GROUP, body, 0)
    rem = n_far % FAR_GROUP
    start = n_far - rem
    size = FAR_GROUP // 2
    while size >= 1:
        @pl.when((rem & size) != 0)
        def _(start=start, size=size):
            run([start + u for u in range(size)])
        start = start + (rem & size)
        size //= 2


def _dsa_kernel(qiT_ref, miscT_ref, keys_ref, qaT_ref, vT_ref, bias_ref, o_ref,
                sc_ref, sc16_ref, qi_ref, qa_ref, oT_ref, m_ref, acc_ref, *, k_top):
    j = pl.program_id(1)
    w_scale = (N_IDX_HEADS ** -0.5) * (IDX_DIM ** -0.5)
    w = miscT_ref[0:N_IDX_HEADS, :] * w_scale
    for h in range(N_IDX_HEADS):
        qi_ref[h] = _pad_rows(qiT_ref[h * IDX_DIM:(h + 1) * IDX_DIM, :], HEAD_DIM)
    for h in range(N_HEADS_A):
        qa_ref[h] = _pad_rows(qaT_ref[h * HEAD_DIM:(h + 1) * HEAD_DIM, :], 0)
    krow = lax.broadcasted_iota(jnp.int32, (TK, TQ), 0)
    qcol = lax.broadcasted_iota(jnp.int32, (TK, TQ), 1)
    causal = krow <= qcol

    def keys_blk(kb):
        return keys_ref[pl.ds(pl.multiple_of(kb * TK, TK), TK), :]

    def score_blk(kb):
        keys = keys_blk(kb)
        acc = None
        for h in range(N_IDX_HEADS):
            sh = jnp.dot(keys, qi_ref[h], preferred_element_type=f32)
            term = w[h:h + 1, :] * jnp.maximum(sh, 0.0)
            acc = term if acc is None else acc + term
        return acc

    def far_score(kb, carry):
        mn, mx = carry
        a = score_blk(kb)
        sc_ref[kb] = a
        sc16_ref[kb] = a.astype(bf16)
        return (jnp.minimum(mn, jnp.min(a, axis=0, keepdims=True)),
                jnp.maximum(mx, jnp.max(a, axis=0, keepdims=True)))

    def diag_score(carry):
        mn, mx = carry
        a = score_blk(j)
        a_diag = jnp.where(causal, a, NEG_SCORE)
        sc_ref[j] = a_diag
        sc16_ref[j] = a_diag.astype(bf16)
        return (jnp.minimum(mn, jnp.min(jnp.where(causal, a, BIG_SCORE), axis=0, keepdims=True)),
                jnp.maximum(mx, jnp.max(a_diag, axis=0, keepdims=True)))

    def far_group(start, size, c):
        for u in range(size):
            c = far_score(start + u, c)
        return c

    stats = lax.fori_loop(0, j // FAR_GROUP, lambda i, c: far_group(FAR_GROUP * i, FAR_GROUP, c),
                          (jnp.full((1, TQ), BIG_SCORE, f32), jnp.full((1, TQ), NEG_SCORE, f32)))
    rem = j % FAR_GROUP
    start = j - rem
    size = FAR_GROUP // 2
    while size >= 2:
        stats = lax.cond((rem & size) != 0, functools.partial(far_group, start, size), lambda c: c, stats)
        start = start + (rem & size)
        size //= 2
    rmin, rmax = lax.cond((rem & 1) != 0, lambda c: diag_score(far_score(j - 1, c)), diag_score, stats)
    nvalid = (j * TQ + qcol[0:1, :] + 1).astype(f32)
    lo = _select_topk(sc_ref, sc16_ref, j + 1, rmin, rmax, nvalid, k_top)

    _attn_init(m_ref, acc_ref)

    def block_streams(kb, mask, bias_rows):
        keys, vT, madd = keys_blk(kb), _with_ones(vT_ref[kb]), _mask_add(mask)
        return [Stream(qa_ref[h], keys, vT, madd,
                       None if bias_rows is None else bias_ref[h, bias_rows, :], h) for h in range(N_HEADS_A)]

    def far(kb):
        return block_streams(kb, sc_ref[kb] >= lo, None)

    _far_blocks(jnp.maximum(j - 1, 0),
                lambda kbs: _attn_streams([st for kb in kbs for st in far(kb)], m_ref, acc_ref))
    diag_mask = (sc_ref[j] >= lo) & causal

    @pl.when(j >= 1)
    def _():
        _attn_streams(block_streams(j - 1, sc_ref[j - 1] >= lo, slice(0, TK))
                      + block_streams(j, diag_mask, slice(TK, 2 * TK)), m_ref, acc_ref)

    @pl.when(j == 0)
    def _():
        _attn_streams(block_streams(j, diag_mask, slice(TK, 2 * TK)), m_ref, acc_ref)

    for h in range(N_HEADS_A):
        oT_ref[h * HEAD_DIM:(h + 1) * HEAD_DIM, :] = _attn_out(h, acc_ref)
    o_ref[...] = oT_ref[...].T.astype(o_ref.dtype)


def _attn_scratch(n_slots):
    return [pltpu.VMEM((n_slots, 1, TQ), f32), pltpu.VMEM((n_slots, ACC_ROWS, TQ), f32)]


def _dsa(qiT, miscT, keys, qaT, vT_blocks, biasT, B, S, k_top):
    nq = S // TQ
    nkb = S // TK
    col_spec = lambda n: pl.BlockSpec((n, TQ), lambda b, j: (0, b * nq + j))
    return pl.pallas_call(
        functools.partial(_dsa_kernel, k_top=k_top),
        grid=(B, nq),
        in_specs=[col_spec(W_QI), col_spec(W_MISC),
                  pl.BlockSpec((S, LANES), lambda b, j: (b, 0)),
                  col_spec(W_QA),
                  pl.BlockSpec((None, nkb, HEAD_DIM, TK), lambda b, j: (b, 0, 4, 0)),
                  pl.BlockSpec(biasT.shape, lambda b, j: (0, 0, 0))],
        out_specs=pl.BlockSpec((TQ, W_QA), lambda b, j: (b * nq + j, 0)),
        out_shape=jax.ShapeDtypeStruct((B * S, W_QA), bf16),
        scratch_shapes=[pltpu.VMEM((nkb, TK, TQ), f32),
                        pltpu.VMEM((nkb, TK, TQ), bf16),
                        pltpu.VMEM((N_IDX_HEADS, LANES, TQ), bf16),
                        pltpu.VMEM((N_HEADS_A, LANES, TQ), bf16),
                        pltpu.VMEM((W_QA, TQ), f32)] + _attn_scratch(N_HEADS_A),
        compiler_params=_cparams("parallel", "parallel"),
    )(qiT, miscT, keys, qaT, vT_blocks, biasT)


def _compress_kernel(xk_ref, xv_ref, pe_ref, w1_ref, w1tok_ref, w2bd_ref, w2bdT_ref, ck_ref, cvT_ref):
    nc = ck_ref.shape[0]
    G = N_KV_GROUPS_B

    def hidden(kind):
        x_ref = (xk_ref, xv_ref)[kind]
        const = jnp.dot(pe_ref[kind], w1_ref[kind], preferred_element_type=f32)[0:1]
        ab = jnp.zeros((nc, 2 * G * CMP_HIDDEN), f32)
        for i in range(CMP_STRIDE):
            tok = x_ref[pl.ds(i, nc, stride=CMP_STRIDE), :]
            ab = ab + jnp.dot(tok.astype(bf16), w1tok_ref[kind, i], preferred_element_type=f32)
        hs = []
        for g in range(G):
            first = ab[:, g * CMP_HIDDEN:(g + 1) * CMP_HIDDEN]
            second = pltpu.roll(ab[:, (G + g) * CMP_HIDDEN:(G + g + 1) * CMP_HIDDEN], nc - 1, 0)
            hs.append(jax.nn.gelu(first + second + const).astype(bf16))
        return jnp.concatenate(hs, axis=1)

    ck_ref[...] = jnp.dot(hidden(0), w2bd_ref[...], preferred_element_type=f32).astype(ck_ref.dtype)
    cvT_ref[...] = lax.dot_general(w2bdT_ref[...], hidden(1), NT_DIMS,
                                   preferred_element_type=f32).astype(cvT_ref.dtype)


def _compress(kcv, pe8, w1, w1tok, w2bd_k, w2bdT_v, B, S):
    nc = S // CMP_STRIDE
    cs = lambda a: pl.BlockSpec(a.shape, lambda b: (0,) * a.ndim)
    return pl.pallas_call(
        _compress_kernel,
        grid=(B,),
        in_specs=[pl.BlockSpec((S, LANES), lambda b: (b, 0)),
                  pl.BlockSpec((S, LANES), lambda b: (b, 1)),
                  cs(pe8), cs(w1), cs(w1tok), cs(w2bd_k), cs(w2bdT_v)],
        out_specs=[pl.BlockSpec((None, nc, LANES), lambda b: (b, 0, 0)),
                   pl.BlockSpec((None, LANES, nc), lambda b: (b, 0, 0))],
        out_shape=[jax.ShapeDtypeStruct((B, nc, LANES), bf16), jax.ShapeDtypeStruct((B, LANES, nc), bf16)],
        compiler_params=_cparams("parallel"),
    )(kcv, kcv, pe8, w1, w1tok, w2bd_k, w2bdT_v)


def _nsa_kernel(qbT_ref, miscT_ref, ck_ref, cvT_ref, ksw_ref, vT_ref, bias_ref, bcmp_ref, ovT_ref, e3_ref,
                o_ref, qb_ref, s_ref, oT_ref, m_ref, acc_ref, *, n_sel):
    j = pl.program_id(1)
    nc = ck_ref.shape[0]
    n_s = ovT_ref.shape[0]
    R = HEADS_PER_GROUP
    G = N_KV_GROUPS_B
    H = N_HEADS_B
    for h in range(H):
        qb_ref[h] = _pad_rows(qbT_ref[h * HEAD_DIM:(h + 1) * HEAD_DIM, :], (h // R) * HEAD_DIM)
    krow = lax.broadcasted_iota(jnp.int32, (TK, TQ), 0)
    qcol = lax.broadcasted_iota(jnp.int32, (TK, TQ), 1)
    causal = krow <= qcol
    t_c = j * TQ + lax.broadcasted_iota(jnp.int32, (nc, TQ), 1)
    c_c = lax.broadcasted_iota(jnp.int32, (nc, TQ), 0)
    mask_c = c_c * CMP_STRIDE + (CMP_BLOCK - 1) <= t_c
    blk = lax.broadcasted_iota(jnp.int32, (n_s, TQ), 0)
    cur = (j * TQ + lax.broadcasted_iota(jnp.int32, (n_s, TQ), 1)) // SLC_BLOCK
    valid = blk <= cur
    forced = (blk == 0) | (blk == cur) | (blk == cur - 1)
    gates = jax.nn.sigmoid(miscT_ref[N_IDX_HEADS:N_IDX_HEADS + W_GB, :])
    win_blocks = WINDOW // TK
    cmp_per_q = TQ // CMP_STRIDE
    half = CMP_WIN // 2

    for h in range(H):
        s_ref[h] = jnp.dot(ck_ref[...], qb_ref[h], preferred_element_type=f32)

    @pl.when(j == 0)
    def _():
        for h in range(H):
            s_ref[h, 0:half, :] += bcmp_ref[h, half:CMP_WIN, :]

    @pl.when(j > 0)
    def _():
        rows = pl.ds(pl.multiple_of(j * cmp_per_q - half, half), CMP_WIN)
        for h in range(H):
            s_ref[h, rows, :] += bcmp_ref[h]

    o_c = []
    psum = [jnp.zeros((nc, TQ), f32) for _ in range(G)]
    for h in range(H):
        g = h // R
        s = jnp.where(mask_c, s_ref[h], NEG_MASK)
        m = jnp.max(s, axis=0, keepdims=True)
        e = jnp.where(mask_c, jnp.exp2(s - m), 0.0)
        p = e * (1.0 / jnp.maximum(jnp.sum(e, axis=0, keepdims=True), 1e-30))
        psum[g] = psum[g] + p
        o_c.append(jnp.dot(cvT_ref[g * HEAD_DIM:(g + 1) * HEAD_DIM, :], p.astype(bf16),
                           preferred_element_type=f32))

    selT = []
    for g in range(G):
        p_hi = psum[g].astype(bf16)
        p_lo = (psum[g] - p_hi.astype(f32)).astype(bf16)
        imp = (jnp.dot(ovT_ref[...], p_hi, preferred_element_type=f32)
               + jnp.dot(ovT_ref[...], p_lo, preferred_element_type=f32))
        score = jnp.where(valid, jnp.where(forced, FORCED_SCORE, imp), NEG_SCORE)
        selT.append(_rank_select(score, n_sel).astype(bf16))

    _attn_init(m_ref, acc_ref)

    def branch_streams(kb, lanes, v_row0, masks, bias_rows, slot0):
        keys = ksw_ref[pl.ds(pl.multiple_of(kb * TK, TK), TK), lanes]
        out = []
        for g in range(G):
            vT = _with_ones(vT_ref[kb, v_row0 + g * HEAD_DIM:v_row0 + (g + 1) * HEAD_DIM, :])
            madd = None if masks[g] is None else _mask_add(masks[g])
            for h in range(g * R, (g + 1) * R):
                out.append(Stream(qb_ref[h], keys, vT, madd,
                                  None if bias_rows is None else bias_ref[h, bias_rows, :], slot0 + h))
        return out

    def tok_masks(kb, extra=None):
        ms = [jnp.dot(e3_ref[kb], selT[g], preferred_element_type=f32) > 0.5 for g in range(G)]
        return ms if extra is None else [m & extra for m in ms]

    def sel_streams(kb, extra=None, bias_rows=None):
        return branch_streams(kb, slice(0, LANES), 0, tok_masks(kb, extra), bias_rows, 0)

    def win_streams(kb, mask=None, bias_rows=None):
        return branch_streams(kb, slice(LANES, 2 * LANES), G * HEAD_DIM, [mask] * G, bias_rows, H)

    _far_blocks(jnp.maximum(j - 1, 0),
                lambda kbs: _attn_streams([st for kb in kbs for st in sel_streams(kb)], m_ref, acc_ref))

    prev_rows, diag_rows = slice(0, TK), slice(TK, 2 * TK)

    def near_streams(n_before):
        out = []
        for d in range(min(n_before, win_blocks), 1, -1):
            out += win_streams(j - d, (krow > qcol) if d == win_blocks else None)
        if n_before >= 1:
            out += sel_streams(j - 1, None, prev_rows)
            out += win_streams(j - 1, (krow > qcol) if win_blocks == 1 else None, prev_rows)
        return out + sel_streams(j, causal, diag_rows) + win_streams(j, causal, diag_rows)

    for n_before in range(win_blocks + 1):
        @pl.when((j >= n_before) if n_before == win_blocks else (j == n_before))
        def _(n_before=n_before):
            _attn_streams(near_streams(n_before), m_ref, acc_ref)

    for h in range(H):
        oT_ref[h * HEAD_DIM:(h + 1) * HEAD_DIM, :] = (
            gates[h:h + 1, :] * o_c[h]
            + gates[H + h:H + h + 1, :] * _attn_out(h, acc_ref)
            + gates[2 * H + h:2 * H + h + 1, :] * _attn_out(H + h, acc_ref))
    o_ref[...] = oT_ref[...].T.astype(o_ref.dtype)


def _nsa(qbT, miscT, ck, cvT, ksw, vT_blocks, biasT, bcmpT, ovT, e3T, B, S, n_sel):
    nq = S // TQ
    nkb = S // TK
    nc = ck.shape[1]
    col_spec = lambda n: pl.BlockSpec((n, TQ), lambda b, j: (0, b * nq + j))
    const = lambda a: pl.BlockSpec(a.shape, lambda b, j: (0,) * a.ndim)
    return pl.pallas_call(
        functools.partial(_nsa_kernel, n_sel=n_sel),
        grid=(B, nq),
        in_specs=[col_spec(W_QB), col_spec(W_MISC),
                  pl.BlockSpec((None, nc, LANES), lambda b, j: (b, 0, 0)),
                  pl.BlockSpec((None, LANES, nc), lambda b, j: (b, 0, 0)),
                  pl.BlockSpec((S, 2 * LANES), lambda b, j: (b, 0)),
                  pl.BlockSpec((None, nkb, 4 * HEAD_DIM, TK), lambda b, j: (b, 0, 0, 0)),
                  const(biasT), const(bcmpT), const(ovT), const(e3T)],
        out_specs=pl.BlockSpec((TQ, W_QB), lambda b, j: (b * nq + j, 0)),
        out_shape=jax.ShapeDtypeStruct((B * S, W_QB), bf16),
        scratch_shapes=[pltpu.VMEM((N_HEADS_B, LANES, TQ), bf16),
                        pltpu.VMEM((N_HEADS_B, nc, TQ), f32),
                        pltpu.VMEM((W_QB, TQ), f32)] + _attn_scratch(2 * N_HEADS_B),
        compiler_params=_cparams("parallel", "parallel"),
    )(qbT, miscT, ck, cvT, ksw, vT_blocks, biasT, bcmpT, ovT, e3T)


def _mix_kernel(oa_ref, ob_ref, gate_ref, x_ref, wa_ref, wb_ref, wo_ref, x1_ref):
    d = x_ref.shape[-1]
    ga = jax.nn.sigmoid(gate_ref[:, :d].astype(f32))
    gb = jax.nn.sigmoid(gate_ref[:, d:].astype(f32))
    mix = (ga * jnp.dot(oa_ref[...], wa_ref[...], preferred_element_type=f32)
           + gb * jnp.dot(ob_ref[...], wb_ref[...], preferred_element_type=f32))
    x1_ref[...] = x_ref[...] + jnp.dot(mix.astype(bf16), wo_ref[...], preferred_element_type=f32)


def _mix(oa, ob, gate, x2, wa, wb, wo, tm=512):
    rows, d = x2.shape
    rs = lambda w: pl.BlockSpec((tm, w), lambda i: (i, 0))
    cs = lambda a: pl.BlockSpec(a.shape, lambda i: (0, 0))
    return pl.pallas_call(
        _mix_kernel,
        grid=(rows // tm,),
        in_specs=[rs(W_QA), rs(W_QB), rs(2 * d), rs(d), cs(wa), cs(wb), cs(wo)],
        out_specs=rs(d),
        out_shape=jax.ShapeDtypeStruct((rows, d), f32),
        compiler_params=_cparams("parallel"),
    )(oa, ob, gate, x2, wa, wb, wo)


def _mlp_kernel(x1_ref, g_ref, w1_ref, w2_ref, gf_ref, o_ref, *, chunk, sub_rows):
    for r0 in range(0, x1_ref.shape[0], sub_rows):
        x1 = x1_ref[r0:r0 + sub_rows, :]
        h = _rms(x1, g_ref[...]).astype(bf16)
        acc = jnp.zeros(x1.shape, f32)
        for c0 in range(0, w1_ref.shape[1], chunk):
            u = jnp.dot(h, w1_ref[:, c0:c0 + chunk], preferred_element_type=f32)
            u = jnp.square(jnp.maximum(u, 0.0)).astype(bf16)
            acc = acc + jnp.dot(u, w2_ref[c0:c0 + chunk, :], preferred_element_type=f32)
        o_ref[r0:r0 + sub_rows, :] = _rms(x1 + acc, gf_ref[...])


def _mlp(x1, g, w1, w2, gf, tm=512, chunk=512, sub_rows=256):
    rows, d = x1.shape
    rs = pl.BlockSpec((tm, d), lambda i: (i, 0))
    cs = lambda a: pl.BlockSpec(a.shape, lambda i: (0, 0))
    ws = lambda a: pl.BlockSpec(a.shape, lambda i: (0, 0), pipeline_mode=pl.Buffered(1))
    return pl.pallas_call(
        functools.partial(_mlp_kernel, chunk=chunk, sub_rows=sub_rows),
        grid=(rows // tm,),
        in_specs=[rs, cs(g), ws(w1), ws(w2), cs(gf)],
        out_specs=rs,
        out_shape=jax.ShapeDtypeStruct((rows, d), f32),
        compiler_params=_cparams("parallel"),
    )(x1, g, w1, w2, gf)


def _t5_bucket_np(dist):
    n = np.maximum(dist, 0)
    max_exact = N_BUCKETS // 2
    nf = np.maximum(n, 1).astype(np.float32)
    large = max_exact + (np.log(nf / np.float32(max_exact)) / np.float32(math.log(MAX_DISTANCE / max_exact))
                         * np.float32(N_BUCKETS - max_exact)).astype(np.int32)
    large = np.minimum(large, N_BUCKETS - 1)
    return np.where(n < max_exact, n, large)


def _bias_tables(rel_bias):
    shifted = (rel_bias.astype(f32) - rel_bias[N_BUCKETS - 1].astype(f32)) * LOG2E

    def lookup(dist, tab):
        bucket = np.where((dist >= 0) & (dist < MAX_DISTANCE), _t5_bucket_np(dist), N_BUCKETS - 1)
        onehot = (jnp.asarray(bucket, jnp.int32)[..., None] == jnp.arange(N_BUCKETS)).astype(f32)
        return jnp.einsum('kqb,bh->hkq', onehot, tab, precision=lax.Precision.HIGHEST)

    q = np.arange(TQ)[None, :]
    near = lookup(q + TK - np.arange(2 * TK)[:, None], shifted).astype(bf16)
    m = np.arange(CMP_WIN)[:, None]
    cmp = lookup(q - (CMP_BLOCK - 1) - CMP_STRIDE * (m - CMP_WIN // 2), shifted[:, N_HEADS_A:])
    return near[:N_HEADS_A], near[N_HEADS_A:], cmp


def _static_tables(S, nc):
    n_s = S // SLC_BLOCK
    cmp_start = np.arange(nc)[None, :] * CMP_STRIDE
    slc_start = np.arange(n_s)[:, None] * SLC_BLOCK
    ovT = np.clip(np.minimum(cmp_start + CMP_BLOCK, slc_start + SLC_BLOCK)
                  - np.maximum(cmp_start, slc_start), 0, None).astype(np.float32) / CMP_BLOCK
    nkb = S // TK
    tok_blk = (np.arange(nkb)[:, None, None] * TK + np.arange(TK)[None, :, None]) // SLC_BLOCK
    e3T = (np.arange(LANES)[None, None, :] == tok_blk).astype(np.float32)
    return jnp.asarray(ovT, bf16), jnp.asarray(e3T, bf16)


def _compress_weights(w1, w2):
    z = jnp.zeros((HEAD_DIM, CMP_HIDDEN), w1.dtype)
    first = w1[:CMP_STRIDE * HEAD_DIM].reshape(CMP_STRIDE, HEAD_DIM, CMP_HIDDEN)
    second = w1[CMP_STRIDE * HEAD_DIM:].reshape(CMP_STRIDE, HEAD_DIM, CMP_HIDDEN)
    zz = jnp.broadcast_to(z, first.shape)
    tok = jnp.concatenate([jnp.concatenate([first, zz, second, zz], axis=2),
                           jnp.concatenate([zz, first, zz, second], axis=2)], axis=1)
    z2 = jnp.zeros((CMP_HIDDEN, HEAD_DIM), w2.dtype)
    return tok, jnp.block([[w2, z2], [z2, w2]])


def kernel(x, norm_mix, w_in, cmp_pe_k, cmp_w1_k, cmp_w2_k, cmp_pe_v, cmp_w1_v, cmp_w2_v, rel_bias,
           w_branch_a, w_branch_b, w_out, norm_mlp, w_mlp_in, w_mlp_out, norm_final):
    B, S, D = x.shape
    assert norm_mix.shape[0] == 1 and S % TQ == 0 and (S // CMP_STRIDE) % LANES == 0
    assert WINDOW % TK == 0 and (S // SLC_BLOCK) % SUBLANES == 0 and S // SLC_BLOCK <= LANES
    assert S // COUNT_ROWS <= 256
    G = N_KV_GROUPS_B
    rows = B * S
    nkb = S // TK
    nc = S // CMP_STRIDE
    k_top = min(TOPK_TOKENS, S // 4)
    n_sel = min(N_SLC_BLOCKS, S // SLC_BLOCK)
    x2 = x.reshape(rows, D)

    w = w_in[0]
    o_ka = W_QA
    o_va = o_ka + HEAD_DIM
    o_qi = o_va + HEAD_DIM
    o_ki = o_qi + W_QI
    o_wi = o_ki + IDX_DIM
    o_qb = o_wi + N_IDX_HEADS
    o_kvb = o_qb + W_QB
    o_gb = o_kvb + W_KVB
    o_gate = o_gb + W_GB
    kv = lambda kind: w[:, o_kvb + kind * G * HEAD_DIM:o_kvb + (kind + 1) * G * HEAD_DIM]
    pad = jnp.zeros((D, LANES - HEAD_DIM - IDX_DIM), w.dtype)
    w_rm = jnp.concatenate([w[:, o_ka:o_va], w[:, o_ki:o_wi], pad, kv(0), kv(1), kv(2), kv(4),
                            w[:, o_gate:]], axis=1).astype(bf16)
    w_t = jnp.concatenate([w[:, :o_ka], w[:, o_qi:o_ki], w[:, o_wi:o_qb], w[:, o_gb:o_gate],
                           w[:, o_qb:o_kvb], kv(3), kv(5), w[:, o_va:o_qi]], axis=1).T.astype(bf16)
    keys_a, kcv, ksw, gate, qaT, qiT, miscT, qbT, vT = _inproj(x2, norm_mix[0][None], w_rm, w_t)
    vT_blocks = vT.reshape(B, nkb, W_VT, TK)

    bias_a, bias_b, bcmp = _bias_tables(rel_bias)
    ovT, e3T = _static_tables(S, nc)

    o_a = _dsa(qiT, miscT, keys_a, qaT, vT_blocks, bias_a, B, S, k_top)

    chunk_w = CMP_STRIDE * HEAD_DIM
    pe = jnp.stack([cmp_pe_k[0], cmp_pe_v[0]]).reshape(2, 1, 2 * chunk_w)
    pe8 = jnp.broadcast_to(pe, (2, SUBLANES, 2 * chunk_w)).astype(bf16)
    w1 = jnp.stack([cmp_w1_k[0], cmp_w1_v[0]]).astype(bf16)
    tok_k, w2bd_k = _compress_weights(cmp_w1_k[0], cmp_w2_k[0])
    tok_v, w2bd_v = _compress_weights(cmp_w1_v[0], cmp_w2_v[0])
    ck, cvT = _compress(kcv, pe8, w1, jnp.stack([tok_k, tok_v]).astype(bf16),
                        w2bd_k.astype(bf16), w2bd_v.T.astype(bf16), B, S)

    o_b = _nsa(qbT, miscT, ck, cvT, ksw, vT_blocks, bias_b, bcmp, ovT, e3T, B, S, n_sel)

    x1 = _mix(o_a, o_b, gate, x2, w_branch_a[0].astype(bf16), w_branch_b[0].astype(bf16),
              w_out[0].astype(bf16))
    out = _mlp(x1, norm_mlp[0][None], w_mlp_in[0].astype(bf16), w_mlp_out[0].astype(bf16),
               norm_final[None])
    return out.reshape(B, S, D)
```

```python
import collections
import functools
import math

import numpy as np
import jax
import jax.numpy as jnp
from jax import lax
from jax.experimental import pallas as pl
from jax.experimental.pallas import tpu as pltpu

f32 = jnp.float32
bf16 = jnp.bfloat16

HEAD_DIM = 64
N_HEADS_A = 8
N_IDX_HEADS = 8
IDX_DIM = 32
TOPK_TOKENS = 256
N_HEADS_B = 8
N_KV_GROUPS_B = 2
HEADS_PER_GROUP = N_HEADS_B // N_KV_GROUPS_B
CMP_BLOCK = 32
CMP_STRIDE = 16
CMP_HIDDEN = 128
SLC_BLOCK = 64
N_SLC_BLOCKS = 16
WINDOW = 512
FORCED_SCORE = 1.0e4
N_BUCKETS = 32
MAX_DISTANCE = 128
EPS = 1e-6

W_QA = N_HEADS_A * HEAD_DIM
W_QI = N_IDX_HEADS * IDX_DIM
W_QB = N_HEADS_B * HEAD_DIM
W_KVB = 6 * N_KV_GROUPS_B * HEAD_DIM
W_GB = 3 * N_HEADS_B
W_MISC = N_IDX_HEADS + W_GB
W_VT = 5 * HEAD_DIM

LANES = 128
SUBLANES = 8
BF16_ROWS = 16
TQ = 256
TK = 256
QK_LOOKAHEAD = 3
PV_DELAY = 2
FAR_GROUP = 8
ACC_ROWS = HEAD_DIM + BF16_ROWS
CMP_WIN = 32
LOG2E = 1.4426950408889634
NEG_MASK = -2.0 ** 101
M_INIT = -2.0 ** 100
NEG_SCORE = -3.0e38
BIG_SCORE = 3.0e38
BIG_SCORE16 = 2.0 ** 127
COARSE_ITERS = 12
FINE_FIXED_ITERS = 9
MAX_SEARCH_ITERS = 400
TIE_CHECK_START = 14
COUNT_ROWS = 32
VMEM_LIMIT = 56 * 1024 * 1024

NT_DIMS = (((1,), (1,)), ((), ()))


def _cparams(*sem):
    return pltpu.CompilerParams(dimension_semantics=sem, vmem_limit_bytes=VMEM_LIMIT)


def _rms(x, g):
    return x * lax.rsqrt(jnp.mean(x * x, axis=-1, keepdims=True) + EPS) * g


def _inproj_kernel(x_ref, g_ref, w_ref, wT_ref, keys_ref, kcv_ref, ksw_ref, gate_ref,
                   qaT_ref, qiT_ref, miscT_ref, qbT_ref, vT_ref):
    q_scale = HEAD_DIM ** -0.5 * LOG2E
    for t in range(vT_ref.shape[0]):
        rs = slice(t * TK, (t + 1) * TK)
        h = _rms(x_ref[rs, :], g_ref[...]).astype(bf16)
        off = 0
        for ref in (keys_ref, kcv_ref, ksw_ref, gate_ref):
            width = ref.shape[-1]
            for c0 in range(0, width, 512):
                c1 = min(c0 + 512, width)
                y = jnp.dot(h, w_ref[:, off + c0:off + c1], preferred_element_type=f32)
                ref[rs, c0:c1] = y.astype(ref.dtype)
            off += width
        off = 0
        for ref, scale in ((qaT_ref, q_scale), (qiT_ref, None), (miscT_ref, None), (qbT_ref, q_scale),
                           (vT_ref.at[t], None)):
            n = ref.shape[0]
            y = lax.dot_general(wT_ref[off:off + n, :], h, NT_DIMS, preferred_element_type=f32)
            if scale is not None:
                y = y * scale
            if ref.shape[1] == TK:
                ref[...] = y.astype(ref.dtype)
            else:
                ref[:, rs] = y.astype(ref.dtype)
            off += n


def _inproj(x2, g, w_rm, w_t, sub_tiles=2):
    rows, d = x2.shape
    tm = sub_tiles * TK
    rm = ((LANES, bf16), (2 * LANES, f32), (2 * LANES, bf16), (2 * d, bf16))
    tr = ((W_QA, bf16), (W_QI, bf16), (W_MISC, f32), (W_QB, bf16))
    return pl.pallas_call(
        _inproj_kernel,
        grid=(rows // tm,),
        in_specs=[pl.BlockSpec((tm, d), lambda i: (i, 0)),
                  pl.BlockSpec((1, d), lambda i: (0, 0)),
                  pl.BlockSpec(w_rm.shape, lambda i: (0, 0)),
                  pl.BlockSpec(w_t.shape, lambda i: (0, 0))],
        out_specs=([pl.BlockSpec((tm, w), lambda i: (i, 0)) for w, _ in rm]
                   + [pl.BlockSpec((n, tm), lambda i: (0, i)) for n, _ in tr]
                   + [pl.BlockSpec((sub_tiles, W_VT, TK), lambda i: (i, 0, 0))]),
        out_shape=([jax.ShapeDtypeStruct((rows, w), dt) for w, dt in rm]
                   + [jax.ShapeDtypeStruct((n, rows), dt) for n, dt in tr]
                   + [jax.ShapeDtypeStruct((rows // TK, W_VT, TK), bf16)]),
        compiler_params=_cparams("parallel"),
    )(x2, g, w_rm, w_t)


def _loop_by_pairs(n, body, init):
    acc = lax.fori_loop(0, n // 2, lambda i, a: body(2 * i + 1, body(2 * i, a)), init)
    return lax.cond(n % 2 == 1, lambda a: body(n - 1, a), lambda a: a, acc)


def _count_ge(sc_ref, nkb, p):
    def body(kb, acc):
        hit = jnp.where(sc_ref[kb] >= p, 1.0, 0.0)
        return acc + jnp.sum(hit.reshape(TK // COUNT_ROWS, COUNT_ROWS, TQ), axis=0)
    acc = _loop_by_pairs(nkb, body, jnp.zeros((COUNT_ROWS, TQ), f32))
    return jnp.sum(acc, axis=0, keepdims=True)


def _count_ge16(sc16_ref, nkb, p16):
    one, nil = jnp.ones((), bf16), jnp.zeros((), bf16)
    def body(kb, acc):
        hit = jnp.where(sc16_ref[kb] >= p16, one, nil).reshape(TK // COUNT_ROWS, COUNT_ROWS, TQ)
        for i in range(TK // COUNT_ROWS):
            acc = acc + hit[i]
        return acc
    acc = _loop_by_pairs(nkb, body, jnp.zeros((COUNT_ROWS, TQ), bf16))
    return jnp.sum(acc.astype(f32), axis=0, keepdims=True)


def _coarse_bracket(sc16_ref, nkb, rmin, rmax, active, kf):
    rmax16 = rmax.astype(bf16).astype(f32)

    def pivot(lo16, hi16):
        mid = ((lo16 + jnp.minimum(hi16, rmax16)) * 0.5).astype(bf16).astype(f32)
        return mid, active & (mid > lo16) & (mid < hi16)

    def body(_, st):
        lo16, hi16 = st
        mid, ok = pivot(lo16, hi16)
        ge = _count_ge16(sc16_ref, nkb, mid.astype(bf16)) >= kf
        return jnp.where(ok & ge, mid, lo16), jnp.where(ok & jnp.logical_not(ge), mid, hi16)

    lo16, hi16 = lax.fori_loop(
        0, COARSE_ITERS, body, (rmin.astype(bf16).astype(f32), jnp.full_like(rmin, BIG_SCORE16)))
    below = jnp.maximum(jnp.abs(lo16) * 2.0 ** -7, 1e-37)
    return lo16 - below, jnp.where(hi16 < BIG_SCORE16, hi16, BIG_SCORE)


def _select_topk(sc_ref, sc16_ref, nkb, rmin, rmax, nvalid, k):
    kf = float(k)
    active = nvalid > kf
    lo_c, hi_c = _coarse_bracket(sc16_ref, nkb, rmin, rmax, active, kf)
    lo0 = jnp.where(active, lo_c, NEG_SCORE)
    cl0 = jnp.where(active, nvalid, kf)
    hi0 = jnp.where(active, hi_c, BIG_SCORE)
    zero = jnp.zeros_like(lo0)

    def pending(cl, tie):
        return (cl != kf) & (tie == 0.0)

    def any_pending(st):
        return jnp.max(jnp.where(pending(st[3], st[5]), 1.0, 0.0)) > 0.0

    def bisect(st):
        it, lo, hi, cl, ch, tie = st
        pend = pending(cl, tie)
        hie = jnp.minimum(hi, rmax)
        p = lo + (hie - lo) * 0.5
        p = jnp.where(p > lo, p, hie)
        cnt = _count_ge(sc_ref, nkb, p)
        ge = cnt >= kf
        up_lo = pend & ge
        up_hi = pend & jnp.logical_not(ge)
        lo = jnp.where(up_lo, p, lo)
        cl = jnp.where(up_lo, cnt, cl)
        hi = jnp.where(up_hi, p, hi)
        ch = jnp.where(up_hi, cnt, ch)
        return it + 1, lo, hi, cl, ch, tie

    def tie_check(st):
        it, lo, hi, cl, ch, tie = st
        def scan(kb, c):
            dmin, dmax = c
            s = sc_ref[kb]
            dmin = jnp.minimum(dmin, jnp.min(jnp.where(s >= lo, s, BIG_SCORE), axis=0, keepdims=True))
            dmax = jnp.maximum(dmax, jnp.max(jnp.where(s < hi, s, NEG_SCORE), axis=0, keepdims=True))
            return dmin, dmax
        dmin, dmax = lax.fori_loop(
            0, nkb, scan, (jnp.full((1, TQ), BIG_SCORE, f32), jnp.full((1, TQ), NEG_SCORE, f32)))
        new_tie = pending(cl, tie) & (dmin == dmax)
        return it, jnp.where(new_tie, dmin, lo), hi, cl, ch, jnp.where(new_tie, 1.0, tie)

    st = (jnp.int32(0), lo0, hi0, cl0, zero, zero)
    st = lax.fori_loop(0, FINE_FIXED_ITERS, lambda _, st: bisect(st), st)
    st = lax.while_loop(lambda st: any_pending(st) & (st[0] < TIE_CHECK_START), bisect, st)
    st = lax.while_loop(lambda st: any_pending(st) & (st[0] < MAX_SEARCH_ITERS),
                        lambda st: bisect(tie_check(st)), st)
    _, lo, hi, cl, ch, tie = st
    any_tie = jnp.max(tie) > 0.0

    @pl.when(any_tie)
    def _():
        r = lax.broadcasted_iota(jnp.int32, (TK, TK), 0)
        c = lax.broadcasted_iota(jnp.int32, (TK, TK), 1)
        tril = jnp.where(c <= r, 1.0, 0.0).astype(bf16)
        above = lax.fori_loop(
            0, nkb, lambda kb, n: n + jnp.sum(jnp.where(sc_ref[kb] > lo, 1.0, 0.0), axis=0, keepdims=True), zero)
        need = kf - above

        def body(kb, run):
            s = sc_ref[kb]
            eq = s == lo
            eqf = jnp.where(eq, 1.0, 0.0)
            cum = run + jnp.dot(tril, eqf.astype(bf16), preferred_element_type=f32)
            sel_tie = jnp.where((s > lo) | (eq & (cum <= need)), 1.0, -1.0)
            sel_all = jnp.where(s >= lo, 1.0, -1.0)
            sc_ref[kb] = jnp.where(tie > 0.0, sel_tie, sel_all)
            return run + jnp.sum(eqf, axis=0, keepdims=True)

        lax.fori_loop(0, nkb, body, zero)

    return jnp.where(any_tie, 0.0, lo)


def _rank_select(score, k, n_live, rank_ref):
    n, width = score.shape
    n_tiles = n // SUBLANES
    tiles = [score[v * SUBLANES:(v + 1) * SUBLANES] for v in range(n_tiles)]
    ridx = lax.broadcasted_iota(jnp.int32, (SUBLANES, width), 0)
    rank_ref[...] = jnp.zeros((n, width), f32)
    for t in range(n_tiles):
        @pl.when(t * SUBLANES < n_live)
        def _(t=t):
            ranks = [rank_ref[v * SUBLANES:(v + 1) * SUBLANES, :] for v in range(n_tiles)]
            for i in range(t * SUBLANES, (t + 1) * SUBLANES):
                si = jnp.broadcast_to(tiles[t][i % SUBLANES:i % SUBLANES + 1], (SUBLANES, width))
                for v, x in enumerate(tiles):
                    if v > t:
                        inc = jnp.where(si >= x, 1.0, 0.0)
                    elif v < t:
                        inc = jnp.where(si > x, 1.0, 0.0)
                    else:
                        inc = jnp.where(ridx > i % SUBLANES, jnp.where(si >= x, 1.0, 0.0),
                                        jnp.where(si > x, 1.0, 0.0))
                    ranks[v] = ranks[v] + inc
            for v in range(n_tiles):
                rank_ref[v * SUBLANES:(v + 1) * SUBLANES, :] = ranks[v]
    flags = jnp.where(rank_ref[...] < float(k), 1.0, 0.0)
    return jnp.concatenate([flags, jnp.zeros((LANES - n, width), f32)], axis=0)


Stream = collections.namedtuple("Stream", "q keys vT madd bias slot")


def _attn_init(m_ref, acc_ref):
    m_ref[...] = jnp.full(m_ref.shape, M_INIT, f32)
    acc_ref[...] = jnp.zeros(acc_ref.shape, f32)


def _attn_streams(streams, m_ref, acc_ref):
    def logits(i):
        st = streams[i]
        s = jnp.dot(st.keys, st.q, preferred_element_type=f32).astype(bf16)
        if st.bias is not None:
            s = s + st.bias
        if st.madd is not None:
            s = s + st.madd
        return s

    def values(i, alpha, p):
        st = streams[i]
        acc_ref[st.slot] = alpha * acc_ref[st.slot] + jnp.dot(st.vT, p, preferred_element_type=f32)

    n = len(streams)
    pending = {i: logits(i) for i in range(min(QK_LOOKAHEAD, n))}
    deferred = []
    for i in range(n):
        if i + QK_LOOKAHEAD < n:
            pending[i + QK_LOOKAHEAD] = logits(i + QK_LOOKAHEAD)
        s = pending.pop(i)
        slot = streams[i].slot
        m_old = m_ref[slot]
        m_new = jnp.maximum(m_old, jnp.max(s, axis=0, keepdims=True).astype(f32))
        alpha = jnp.exp2(m_old - m_new)
        p = jnp.exp2(s - m_new.astype(bf16))
        m_ref[slot] = m_new
        deferred.append((i, alpha, p))
        if len(deferred) > PV_DELAY:
            values(*deferred.pop(0))
    for d in deferred:
        values(*d)


def _attn_out(slot, acc_ref):
    acc = acc_ref[slot]
    return acc[0:HEAD_DIM] * (1.0 / jnp.maximum(acc[HEAD_DIM:HEAD_DIM + 1], 1e-30))


def _with_ones(vT):
    return jnp.concatenate([vT, jnp.ones((BF16_ROWS, vT.shape[1]), vT.dtype)], axis=0)


def _mask_add(mask):
    return jnp.where(mask, 0.0, NEG_MASK).astype(bf16)


def _pad_rows(x, start):
    parts = []
    if start:
        parts.append(jnp.zeros((start, x.shape[1]), x.dtype))
    parts.append(x)
    if LANES - start - x.shape[0]:
        parts.append(jnp.zeros((LANES - start - x.shape[0], x.shape[1]), x.dtype))
    return jnp.concatenate(parts, axis=0)


def _far_blocks(n_far, run):
    def body(i, _):
        run([FAR_GROUP * i + u for u in range(FAR_GROUP)])
        return 0
    lax.fori_loop(0, n_far // FAR_GROUP, body, 0)
    rem = n_far % FAR_GROUP
    start = n_far - rem
    size = FAR_GROUP // 2
    while size >= 1:
        @pl.when((rem & size) != 0)
        def _(start=start, size=size):
            run([start + u for u in range(size)])
        start = start + (rem & size)
        size //= 2


def _dsa_kernel(qiT_ref, miscT_ref, keys_ref, qaT_ref, vT_ref, bias_ref, o_ref,
                sc_ref, sc16_ref, qi_ref, qa_ref, oT_ref, m_ref, acc_ref, *, k_top):
    j = pl.program_id(1)
    w_scale = (N_IDX_HEADS ** -0.5) * (IDX_DIM ** -0.5)
    w = miscT_ref[0:N_IDX_HEADS, :] * w_scale
    for h in range(N_IDX_HEADS):
        qi_ref[h] = _pad_rows(qiT_ref[h * IDX_DIM:(h + 1) * IDX_DIM, :], HEAD_DIM)
    for h in range(N_HEADS_A):
        qa_ref[h] = _pad_rows(qaT_ref[h * HEAD_DIM:(h + 1) * HEAD_DIM, :], 0)
    krow = lax.broadcasted_iota(jnp.int32, (TK, TQ), 0)
    qcol = lax.broadcasted_iota(jnp.int32, (TK, TQ), 1)
    causal = krow <= qcol

    def keys_blk(kb):
        return keys_ref[pl.ds(pl.multiple_of(kb * TK, TK), TK), :]

    def score_blk(kb):
        keys = keys_blk(kb)
        acc = None
        for h in range(N_IDX_HEADS):
            sh = jnp.dot(keys, qi_ref[h], preferred_element_type=f32)
            term = w[h:h + 1, :] * jnp.maximum(sh, 0.0)
            acc = term if acc is None else acc + term
        return acc

    def far_score(kb, carry):
        mn, mx = carry
        a = score_blk(kb)
        sc_ref[kb] = a
        sc16_ref[kb] = a.astype(bf16)
        return (jnp.minimum(mn, jnp.min(a, axis=0, keepdims=True)),
                jnp.maximum(mx, jnp.max(a, axis=0, keepdims=True)))

    def diag_score(carry):
        mn, mx = carry
        a = score_blk(j)
        a_diag = jnp.where(causal, a, NEG_SCORE)
        sc_ref[j] = a_diag
        sc16_ref[j] = a_diag.astype(bf16)
        return (jnp.minimum(mn, jnp.min(jnp.where(causal, a, BIG_SCORE), axis=0, keepdims=True)),
                jnp.maximum(mx, jnp.max(a_diag, axis=0, keepdims=True)))

    def far_group(start, size, c):
        for u in range(size):
            c = far_score(start + u, c)
        return c

    stats = lax.fori_loop(0, j // FAR_GROUP, lambda i, c: far_group(FAR_GROUP * i, FAR_GROUP, c),
                          (jnp.full((1, TQ), BIG_SCORE, f32), jnp.full((1, TQ), NEG_SCORE, f32)))
    rem = j % FAR_GROUP
    start = j - rem
    size = FAR_GROUP // 2
    while size >= 2:
        stats = lax.cond((rem & size) != 0, functools.partial(far_group, start, size), lambda c: c, stats)
        start = start + (rem & size)
        size //= 2
    rmin, rmax = lax.cond((rem & 1) != 0, lambda c: diag_score(far_score(j - 1, c)), diag_score, stats)
    nvalid = (j * TQ + qcol[0:1, :] + 1).astype(f32)
    lo = _select_topk(sc_ref, sc16_ref, j + 1, rmin, rmax, nvalid, k_top)

    _attn_init(m_ref, acc_ref)

    def block_streams(kb, mask, bias_rows):
        keys, vT, madd = keys_blk(kb), _with_ones(vT_ref[kb]), _mask_add(mask)
        return [Stream(qa_ref[h], keys, vT, madd,
                       None if bias_rows is None else bias_ref[h, bias_rows, :], h) for h in range(N_HEADS_A)]

    def far(kb):
        return block_streams(kb, sc_ref[kb] >= lo, None)

    _far_blocks(jnp.maximum(j - 1, 0),
                lambda kbs: _attn_streams([st for kb in kbs for st in far(kb)], m_ref, acc_ref))
    diag_mask = (sc_ref[j] >= lo) & causal

    @pl.when(j >= 1)
    def _():
        _attn_streams(block_streams(j - 1, sc_ref[j - 1] >= lo, slice(0, TK))
                      + block_streams(j, diag_mask, slice(TK, 2 * TK)), m_ref, acc_ref)

    @pl.when(j == 0)
    def _():
        _attn_streams(block_streams(j, diag_mask, slice(TK, 2 * TK)), m_ref, acc_ref)

    for h in range(N_HEADS_A):
        oT_ref[h * HEAD_DIM:(h + 1) * HEAD_DIM, :] = _attn_out(h, acc_ref)
    o_ref[...] = oT_ref[...].T.astype(o_ref.dtype)


def _attn_scratch(n_slots):
    return [pltpu.VMEM((n_slots, 1, TQ), f32), pltpu.VMEM((n_slots, ACC_ROWS, TQ), f32)]


def _dsa(qiT, miscT, keys, qaT, vT_blocks, biasT, B, S, k_top):
    nq = S // TQ
    nkb = S // TK
    col_spec = lambda n: pl.BlockSpec((n, TQ), lambda b, j: (0, b * nq + j))
    return pl.pallas_call(
        functools.partial(_dsa_kernel, k_top=k_top),
        grid=(B, nq),
        in_specs=[col_spec(W_QI), col_spec(W_MISC),
                  pl.BlockSpec((S, LANES), lambda b, j: (b, 0)),
                  col_spec(W_QA),
                  pl.BlockSpec((None, nkb, HEAD_DIM, TK), lambda b, j: (b, 0, 4, 0)),
                  pl.BlockSpec(biasT.shape, lambda b, j: (0, 0, 0))],
        out_specs=pl.BlockSpec((TQ, W_QA), lambda b, j: (b * nq + j, 0)),
        out_shape=jax.ShapeDtypeStruct((B * S, W_QA), bf16),
        scratch_shapes=[pltpu.VMEM((nkb, TK, TQ), f32),
                        pltpu.VMEM((nkb, TK, TQ), bf16),
                        pltpu.VMEM((N_IDX_HEADS, LANES, TQ), bf16),
                        pltpu.VMEM((N_HEADS_A, LANES, TQ), bf16),
                        pltpu.VMEM((W_QA, TQ), f32)] + _attn_scratch(N_HEADS_A),
        compiler_params=_cparams("parallel", "parallel"),
    )(qiT, miscT, keys, qaT, vT_blocks, biasT)


def _compress_kernel(xk_ref, xv_ref, pe_ref, w1_ref, w1tok_ref, w2bd_ref, w2bdT_ref, ck_ref, cvT_ref):
    nc = ck_ref.shape[0]
    G = N_KV_GROUPS_B

    def hidden(kind):
        x_ref = (xk_ref, xv_ref)[kind]
        const = jnp.dot(pe_ref[kind], w1_ref[kind], preferred_element_type=f32)[0:1]
        ab = jnp.zeros((nc, 2 * G * CMP_HIDDEN), f32)
        for i in range(CMP_STRIDE):
            tok = x_ref[pl.ds(i, nc, stride=CMP_STRIDE), :]
            ab = ab + jnp.dot(tok.astype(bf16), w1tok_ref[kind, i], preferred_element_type=f32)
        hs = []
        for g in range(G):
            first = ab[:, g * CMP_HIDDEN:(g + 1) * CMP_HIDDEN]
            second = pltpu.roll(ab[:, (G + g) * CMP_HIDDEN:(G + g + 1) * CMP_HIDDEN], nc - 1, 0)
            hs.append(jax.nn.gelu(first + second + const).astype(bf16))
        return jnp.concatenate(hs, axis=1)

    ck_ref[...] = jnp.dot(hidden(0), w2bd_ref[...], preferred_element_type=f32).astype(ck_ref.dtype)
    cvT_ref[...] = lax.dot_general(w2bdT_ref[...], hidden(1), NT_DIMS,
                                   preferred_element_type=f32).astype(cvT_ref.dtype)


def _compress(kcv, pe8, w1, w1tok, w2bd_k, w2bdT_v, B, S):
    nc = S // CMP_STRIDE
    cs = lambda a: pl.BlockSpec(a.shape, lambda b: (0,) * a.ndim)
    return pl.pallas_call(
        _compress_kernel,
        grid=(B,),
        in_specs=[pl.BlockSpec((S, LANES), lambda b: (b, 0)),
                  pl.BlockSpec((S, LANES), lambda b: (b, 1)),
                  cs(pe8), cs(w1), cs(w1tok), cs(w2bd_k), cs(w2bdT_v)],
        out_specs=[pl.BlockSpec((None, nc, LANES), lambda b: (b, 0, 0)),
                   pl.BlockSpec((None, LANES, nc), lambda b: (b, 0, 0))],
        out_shape=[jax.ShapeDtypeStruct((B, nc, LANES), bf16), jax.ShapeDtypeStruct((B, LANES, nc), bf16)],
        compiler_params=_cparams("parallel"),
    )(kcv, kcv, pe8, w1, w1tok, w2bd_k, w2bdT_v)


def _nsa_kernel(qbT_ref, miscT_ref, ck_ref, cvT_ref, ksw_ref, vT_ref, bias_ref, bcmp_ref, ovT_ref, e3_ref,
                o_ref, qb_ref, s_ref, oT_ref, rank_ref, m_ref, acc_ref, *, n_sel):
    j = pl.program_id(1)
    nc = ck_ref.shape[0]
    n_s = ovT_ref.shape[0]
    R = HEADS_PER_GROUP
    G = N_KV_GROUPS_B
    H = N_HEADS_B
    for h in range(H):
        qb_ref[h] = _pad_rows(qbT_ref[h * HEAD_DIM:(h + 1) * HEAD_DIM, :], (h // R) * HEAD_DIM)
    krow = lax.broadcasted_iota(jnp.int32, (TK, TQ), 0)
    qcol = lax.broadcasted_iota(jnp.int32, (TK, TQ), 1)
    causal = krow <= qcol
    t_c = j * TQ + lax.broadcasted_iota(jnp.int32, (nc, TQ), 1)
    c_c = lax.broadcasted_iota(jnp.int32, (nc, TQ), 0)
    mask_c = c_c * CMP_STRIDE + (CMP_BLOCK - 1) <= t_c
    blk = lax.broadcasted_iota(jnp.int32, (n_s, TQ), 0)
    cur = (j * TQ + lax.broadcasted_iota(jnp.int32, (n_s, TQ), 1)) // SLC_BLOCK
    valid = blk <= cur
    forced = (blk == 0) | (blk == cur) | (blk == cur - 1)
    gates = jax.nn.sigmoid(miscT_ref[N_IDX_HEADS:N_IDX_HEADS + W_GB, :])
    win_blocks = WINDOW // TK
    cmp_per_q = TQ // CMP_STRIDE
    half = CMP_WIN // 2

    for h in range(H):
        s_ref[h] = jnp.dot(ck_ref[...], qb_ref[h], preferred_element_type=f32)

    @pl.when(j == 0)
    def _():
        for h in range(H):
            s_ref[h, 0:half, :] += bcmp_ref[h, half:CMP_WIN, :]

    @pl.when(j > 0)
    def _():
        rows = pl.ds(pl.multiple_of(j * cmp_per_q - half, half), CMP_WIN)
        for h in range(H):
            s_ref[h, rows, :] += bcmp_ref[h]

    o_c = []
    psum = [jnp.zeros((nc, TQ), f32) for _ in range(G)]
    for h in range(H):
        g = h // R
        s = jnp.where(mask_c, s_ref[h], NEG_MASK)
        m = jnp.max(s, axis=0, keepdims=True)
        e = jnp.where(mask_c, jnp.exp2(s - m), 0.0)
        p = e * (1.0 / jnp.maximum(jnp.sum(e, axis=0, keepdims=True), 1e-30))
        psum[g] = psum[g] + p
        o_c.append(jnp.dot(cvT_ref[g * HEAD_DIM:(g + 1) * HEAD_DIM, :], p.astype(bf16),
                           preferred_element_type=f32))

    selT = []
    for g in range(G):
        p_hi = psum[g].astype(bf16)
        p_lo = (psum[g] - p_hi.astype(f32)).astype(bf16)
        imp = (jnp.dot(ovT_ref[...], p_hi, preferred_element_type=f32)
               + jnp.dot(ovT_ref[...], p_lo, preferred_element_type=f32))
        score = jnp.where(valid, jnp.where(forced, FORCED_SCORE, imp), NEG_SCORE)
        n_live = (j + 1) * (TQ // SLC_BLOCK)
        selT.append(_rank_select(score, n_sel, n_live, rank_ref).astype(bf16))

    _attn_init(m_ref, acc_ref)

    def branch_streams(kb, lanes, v_row0, masks, bias_rows, slot0):
        keys = ksw_ref[pl.ds(pl.multiple_of(kb * TK, TK), TK), lanes]
        out = []
        for g in range(G):
            vT = _with_ones(vT_ref[kb, v_row0 + g * HEAD_DIM:v_row0 + (g + 1) * HEAD_DIM, :])
            madd = None if masks[g] is None else _mask_add(masks[g])
            for h in range(g * R, (g + 1) * R):
                out.append(Stream(qb_ref[h], keys, vT, madd,
                                  None if bias_rows is None else bias_ref[h, bias_rows, :], slot0 + h))
        return out

    def tok_masks(kb, extra=None):
        ms = [jnp.dot(e3_ref[kb], selT[g], preferred_element_type=f32) > 0.5 for g in range(G)]
        return ms if extra is None else [m & extra for m in ms]

    def sel_streams(kb, extra=None, bias_rows=None):
        return branch_streams(kb, slice(0, LANES), 0, tok_masks(kb, extra), bias_rows, 0)

    def win_streams(kb, mask=None, bias_rows=None):
        return branch_streams(kb, slice(LANES, 2 * LANES), G * HEAD_DIM, [mask] * G, bias_rows, H)

    _far_blocks(jnp.maximum(j - 1, 0),
                lambda kbs: _attn_streams([st for kb in kbs for st in sel_streams(kb)], m_ref, acc_ref))

    prev_rows, diag_rows = slice(0, TK), slice(TK, 2 * TK)

    def near_streams(n_before):
        out = []
        for d in range(min(n_before, win_blocks), 1, -1):
            out += win_streams(j - d, (krow > qcol) if d == win_blocks else None)
        if n_before >= 1:
            out += sel_streams(j - 1, None, prev_rows)
            out += win_streams(j - 1, (krow > qcol) if win_blocks == 1 else None, prev_rows)
        return out + sel_streams(j, causal, diag_rows) + win_streams(j, causal, diag_rows)

    for n_before in range(win_blocks + 1):
        @pl.when((j >= n_before) if n_before == win_blocks else (j == n_before))
        def _(n_before=n_before):
            _attn_streams(near_streams(n_before), m_ref, acc_ref)

    for h in range(H):
        oT_ref[h * HEAD_DIM:(h + 1) * HEAD_DIM, :] = (
            gates[h:h + 1, :] * o_c[h]
            + gates[H + h:H + h + 1, :] * _attn_out(h, acc_ref)
            + gates[2 * H + h:2 * H + h + 1, :] * _attn_out(H + h, acc_ref))
    o_ref[...] = oT_ref[...].T.astype(o_ref.dtype)


def _nsa(qbT, miscT, ck, cvT, ksw, vT_blocks, biasT, bcmpT, ovT, e3T, B, S, n_sel):
    nq = S // TQ
    nkb = S // TK
    nc = ck.shape[1]
    col_spec = lambda n: pl.BlockSpec((n, TQ), lambda b, j: (0, b * nq + j))
    const = lambda a: pl.BlockSpec(a.shape, lambda b, j: (0,) * a.ndim)
    return pl.pallas_call(
        functools.partial(_nsa_kernel, n_sel=n_sel),
        grid=(B, nq),
        in_specs=[col_spec(W_QB), col_spec(W_MISC),
                  pl.BlockSpec((None, nc, LANES), lambda b, j: (b, 0, 0)),
                  pl.BlockSpec((None, LANES, nc), lambda b, j: (b, 0, 0)),
                  pl.BlockSpec((S, 2 * LANES), lambda b, j: (b, 0)),
                  pl.BlockSpec((None, nkb, 4 * HEAD_DIM, TK), lambda b, j: (b, 0, 0, 0)),
                  const(biasT), const(bcmpT), const(ovT), const(e3T)],
        out_specs=pl.BlockSpec((TQ, W_QB), lambda b, j: (b * nq + j, 0)),
        out_shape=jax.ShapeDtypeStruct((B * S, W_QB), bf16),
        scratch_shapes=[pltpu.VMEM((N_HEADS_B, LANES, TQ), bf16),
                        pltpu.VMEM((N_HEADS_B, nc, TQ), f32),
                        pltpu.VMEM((W_QB, TQ), f32),
                        pltpu.VMEM((S // SLC_BLOCK, TQ), f32)] + _attn_scratch(2 * N_HEADS_B),
        compiler_params=_cparams("parallel", "parallel"),
    )(qbT, miscT, ck, cvT, ksw, vT_blocks, biasT, bcmpT, ovT, e3T)


def _mix_kernel(oa_ref, ob_ref, gate_ref, x_ref, wa_ref, wb_ref, wo_ref, x1_ref):
    d = x_ref.shape[-1]
    ga = jax.nn.sigmoid(gate_ref[:, :d].astype(f32))
    gb = jax.nn.sigmoid(gate_ref[:, d:].astype(f32))
    mix = (ga * jnp.dot(oa_ref[...], wa_ref[...], preferred_element_type=f32)
           + gb * jnp.dot(ob_ref[...], wb_ref[...], preferred_element_type=f32))
    x1_ref[...] = x_ref[...] + jnp.dot(mix.astype(bf16), wo_ref[...], preferred_element_type=f32)


def _mix(oa, ob, gate, x2, wa, wb, wo, tm=512):
    rows, d = x2.shape
    rs = lambda w: pl.BlockSpec((tm, w), lambda i: (i, 0))
    cs = lambda a: pl.BlockSpec(a.shape, lambda i: (0, 0))
    return pl.pallas_call(
        _mix_kernel,
        grid=(rows // tm,),
        in_specs=[rs(W_QA), rs(W_QB), rs(2 * d), rs(d), cs(wa), cs(wb), cs(wo)],
        out_specs=rs(d),
        out_shape=jax.ShapeDtypeStruct((rows, d), f32),
        compiler_params=_cparams("parallel"),
    )(oa, ob, gate, x2, wa, wb, wo)


def _mlp_kernel(x1_ref, g_ref, w1_ref, w2_ref, gf_ref, o_ref, *, chunk, sub_rows):
    for r0 in range(0, x1_ref.shape[0], sub_rows):
        x1 = x1_ref[r0:r0 + sub_rows, :]
        h = _rms(x1, g_ref[...]).astype(bf16)
        acc = jnp.zeros(x1.shape, f32)
        for c0 in range(0, w1_ref.shape[1], chunk):
            u = jnp.dot(h, w1_ref[:, c0:c0 + chunk], preferred_element_type=f32)
            u = jnp.square(jnp.maximum(u, 0.0)).astype(bf16)
            acc = acc + jnp.dot(u, w2_ref[c0:c0 + chunk, :], preferred_element_type=f32)
        o_ref[r0:r0 + sub_rows, :] = _rms(x1 + acc, gf_ref[...])


def _mlp(x1, g, w1, w2, gf, tm=512, chunk=512, sub_rows=256):
    rows, d = x1.shape
    rs = pl.BlockSpec((tm, d), lambda i: (i, 0))
    cs = lambda a: pl.BlockSpec(a.shape, lambda i: (0, 0))
    ws = lambda a: pl.BlockSpec(a.shape, lambda i: (0, 0), pipeline_mode=pl.Buffered(1))
    return pl.pallas_call(
        functools.partial(_mlp_kernel, chunk=chunk, sub_rows=sub_rows),
        grid=(rows // tm,),
        in_specs=[rs, cs(g), ws(w1), ws(w2), cs(gf)],
        out_specs=rs,
        out_shape=jax.ShapeDtypeStruct((rows, d), f32),
        compiler_params=_cparams("parallel"),
    )(x1, g, w1, w2, gf)


def _t5_bucket_np(dist):
    n = np.maximum(dist, 0)
    max_exact = N_BUCKETS // 2
    nf = np.maximum(n, 1).astype(np.float32)
    large = max_exact + (np.log(nf / np.float32(max_exact)) / np.float32(math.log(MAX_DISTANCE / max_exact))
                         * np.float32(N_BUCKETS - max_exact)).astype(np.int32)
    large = np.minimum(large, N_BUCKETS - 1)
    return np.where(n < max_exact, n, large)


def _bias_tables(rel_bias):
    shifted = (rel_bias.astype(f32) - rel_bias[N_BUCKETS - 1].astype(f32)) * LOG2E

    def lookup(dist, tab):
        bucket = np.where((dist >= 0) & (dist < MAX_DISTANCE), _t5_bucket_np(dist), N_BUCKETS - 1)
        onehot = (jnp.asarray(bucket, jnp.int32)[..., None] == jnp.arange(N_BUCKETS)).astype(f32)
        return jnp.einsum('kqb,bh->hkq', onehot, tab, precision=lax.Precision.HIGHEST)

    q = np.arange(TQ)[None, :]
    near = lookup(q + TK - np.arange(2 * TK)[:, None], shifted).astype(bf16)
    m = np.arange(CMP_WIN)[:, None]
    cmp = lookup(q - (CMP_BLOCK - 1) - CMP_STRIDE * (m - CMP_WIN // 2), shifted[:, N_HEADS_A:])
    return near[:N_HEADS_A], near[N_HEADS_A:], cmp


def _static_tables(S, nc):
    n_s = S // SLC_BLOCK
    cmp_start = np.arange(nc)[None, :] * CMP_STRIDE
    slc_start = np.arange(n_s)[:, None] * SLC_BLOCK
    ovT = np.clip(np.minimum(cmp_start + CMP_BLOCK, slc_start + SLC_BLOCK)
                  - np.maximum(cmp_start, slc_start), 0, None).astype(np.float32) / CMP_BLOCK
    nkb = S // TK
    tok_blk = (np.arange(nkb)[:, None, None] * TK + np.arange(TK)[None, :, None]) // SLC_BLOCK
    e3T = (np.arange(LANES)[None, None, :] == tok_blk).astype(np.float32)
    return jnp.asarray(ovT, bf16), jnp.asarray(e3T, bf16)


def _compress_weights(w1, w2):
    z = jnp.zeros((HEAD_DIM, CMP_HIDDEN), w1.dtype)
    first = w1[:CMP_STRIDE * HEAD_DIM].reshape(CMP_STRIDE, HEAD_DIM, CMP_HIDDEN)
    second = w1[CMP_STRIDE * HEAD_DIM:].reshape(CMP_STRIDE, HEAD_DIM, CMP_HIDDEN)
    zz = jnp.broadcast_to(z, first.shape)
    tok = jnp.concatenate([jnp.concatenate([first, zz, second, zz], axis=2),
                           jnp.concatenate([zz, first, zz, second], axis=2)], axis=1)
    z2 = jnp.zeros((CMP_HIDDEN, HEAD_DIM), w2.dtype)
    return tok, jnp.block([[w2, z2], [z2, w2]])


def kernel(x, norm_mix, w_in, cmp_pe_k, cmp_w1_k, cmp_w2_k, cmp_pe_v, cmp_w1_v, cmp_w2_v, rel_bias,
           w_branch_a, w_branch_b, w_out, norm_mlp, w_mlp_in, w_mlp_out, norm_final):
    B, S, D = x.shape
    assert norm_mix.shape[0] == 1 and S % TQ == 0 and (S // CMP_STRIDE) % LANES == 0
    assert WINDOW % TK == 0 and (S // SLC_BLOCK) % SUBLANES == 0 and S // SLC_BLOCK <= LANES
    assert S // COUNT_ROWS <= 256
    G = N_KV_GROUPS_B
    rows = B * S
    nkb = S // TK
    nc = S // CMP_STRIDE
    k_top = min(TOPK_TOKENS, S // 4)
    n_sel = min(N_SLC_BLOCKS, S // SLC_BLOCK)
    x2 = x.reshape(rows, D)

    w = w_in[0]
    o_ka = W_QA
    o_va = o_ka + HEAD_DIM
    o_qi = o_va + HEAD_DIM
    o_ki = o_qi + W_QI
    o_wi = o_ki + IDX_DIM
    o_qb = o_wi + N_IDX_HEADS
    o_kvb = o_qb + W_QB
    o_gb = o_kvb + W_KVB
    o_gate = o_gb + W_GB
    kv = lambda kind: w[:, o_kvb + kind * G * HEAD_DIM:o_kvb + (kind + 1) * G * HEAD_DIM]
    pad = jnp.zeros((D, LANES - HEAD_DIM - IDX_DIM), w.dtype)
    w_rm = jnp.concatenate([w[:, o_ka:o_va], w[:, o_ki:o_wi], pad, kv(0), kv(1), kv(2), kv(4),
                            w[:, o_gate:]], axis=1).astype(bf16)
    w_t = jnp.concatenate([w[:, :o_ka], w[:, o_qi:o_ki], w[:, o_wi:o_qb], w[:, o_gb:o_gate],
                           w[:, o_qb:o_kvb], kv(3), kv(5), w[:, o_va:o_qi]], axis=1).T.astype(bf16)
    keys_a, kcv, ksw, gate, qaT, qiT, miscT, qbT, vT = _inproj(x2, norm_mix[0][None], w_rm, w_t)
    vT_blocks = vT.reshape(B, nkb, W_VT, TK)

    bias_a, bias_b, bcmp = _bias_tables(rel_bias)
    ovT, e3T = _static_tables(S, nc)

    o_a = _dsa(qiT, miscT, keys_a, qaT, vT_blocks, bias_a, B, S, k_top)

    chunk_w = CMP_STRIDE * HEAD_DIM
    pe = jnp.stack([cmp_pe_k[0], cmp_pe_v[0]]).reshape(2, 1, 2 * chunk_w)
    pe8 = jnp.broadcast_to(pe, (2, SUBLANES, 2 * chunk_w)).astype(bf16)
    w1 = jnp.stack([cmp_w1_k[0], cmp_w1_v[0]]).astype(bf16)
    tok_k, w2bd_k = _compress_weights(cmp_w1_k[0], cmp_w2_k[0])
    tok_v, w2bd_v = _compress_weights(cmp_w1_v[0], cmp_w2_v[0])
    ck, cvT = _compress(kcv, pe8, w1, jnp.stack([tok_k, tok_v]).astype(bf16),
                        w2bd_k.astype(bf16), w2bd_v.T.astype(bf16), B, S)

    o_b = _nsa(qbT, miscT, ck, cvT, ksw, vT_blocks, bias_b, bcmp, ovT, e3T, B, S, n_sel)

    x1 = _mix(o_a, o_b, gate, x2, w_branch_a[0].astype(bf16), w_branch_b[0].astype(bf16),
              w_out[0].astype(bf16))
    out = _mlp(x1, norm_mlp[0][None], w_mlp_in[0].astype(bf16), w_mlp_out[0].astype(bf16),
               norm_final[None])
    return out.reshape(B, S, D)
```

```python
import collections
import functools
import math

import numpy as np
import jax
import jax.numpy as jnp
from jax import lax
from jax.experimental import pallas as pl
from jax.experimental.pallas import tpu as pltpu

f32 = jnp.float32
bf16 = jnp.bfloat16

HEAD_DIM = 64
N_HEADS_A = 8
N_IDX_HEADS = 8
IDX_DIM = 32
TOPK_TOKENS = 256
N_HEADS_B = 8
N_KV_GROUPS_B = 2
HEADS_PER_GROUP = N_HEADS_B // N_KV_GROUPS_B
CMP_BLOCK = 32
CMP_STRIDE = 16
CMP_HIDDEN = 128
SLC_BLOCK = 64
N_SLC_BLOCKS = 16
WINDOW = 512
FORCED_SCORE = 1.0e4
N_BUCKETS = 32
MAX_DISTANCE = 128
EPS = 1e-6

W_QA = N_HEADS_A * HEAD_DIM
W_QI = N_IDX_HEADS * IDX_DIM
W_QB = N_HEADS_B * HEAD_DIM
W_KVB = 6 * N_KV_GROUPS_B * HEAD_DIM
W_GB = 3 * N_HEADS_B
W_MISC = N_IDX_HEADS + W_GB
W_VT = 5 * HEAD_DIM

LANES = 128
SUBLANES = 8
BF16_ROWS = 16
TQ = 256
TK = 256
QK_LOOKAHEAD = 3
PV_DELAY = 2
FAR_GROUP = 8
ACC_ROWS = HEAD_DIM + BF16_ROWS
CMP_WIN = 32
LOG2E = 1.4426950408889634
NEG_MASK = -2.0 ** 101
M_INIT = -2.0 ** 100
NEG_SCORE = -3.0e38
BIG_SCORE = 3.0e38
BIG_SCORE16 = 2.0 ** 127
COARSE_ITERS = 12
FINE_FIXED_ITERS = 9
MAX_SEARCH_ITERS = 400
TIE_CHECK_START = 14
COUNT_ROWS = 32
VMEM_LIMIT = 56 * 1024 * 1024

NT_DIMS = (((1,), (1,)), ((), ()))


def _cparams(*sem):
    return pltpu.CompilerParams(dimension_semantics=sem, vmem_limit_bytes=VMEM_LIMIT)


def _rms(x, g):
    return x * lax.rsqrt(jnp.mean(x * x, axis=-1, keepdims=True) + EPS) * g


def _inproj_kernel(x_ref, g_ref, w_ref, wT_ref, keys_ref, kcv_ref, ksw_ref, gate_ref,
                   qaT_ref, qiT_ref, miscT_ref, qbT_ref, vT_ref):
    q_scale = HEAD_DIM ** -0.5 * LOG2E
    for t in range(vT_ref.shape[0]):
        rs = slice(t * TK, (t + 1) * TK)
        h = _rms(x_ref[rs, :], g_ref[...]).astype(bf16)
        off = 0
        for ref in (keys_ref, kcv_ref, ksw_ref, gate_ref):
            width = ref.shape[-1]
            for c0 in range(0, width, 512):
                c1 = min(c0 + 512, width)
                y = jnp.dot(h, w_ref[:, off + c0:off + c1], preferred_element_type=f32)
                ref[rs, c0:c1] = y.astype(ref.dtype)
            off += width
        off = 0
        for ref, scale in ((qaT_ref, q_scale), (qiT_ref, None), (miscT_ref, None), (qbT_ref, q_scale),
                           (vT_ref.at[t], None)):
            n = ref.shape[0]
            y = lax.dot_general(wT_ref[off:off + n, :], h, NT_DIMS, preferred_element_type=f32)
            if scale is not None:
                y = y * scale
            if ref.shape[1] == TK:
                ref[...] = y.astype(ref.dtype)
            else:
                ref[:, rs] = y.astype(ref.dtype)
            off += n


def _inproj(x2, g, w_rm, w_t, sub_tiles=2):
    rows, d = x2.shape
    tm = sub_tiles * TK
    rm = ((LANES, bf16), (2 * LANES, f32), (2 * LANES, bf16), (2 * d, bf16))
    tr = ((W_QA, bf16), (W_QI, bf16), (W_MISC, f32), (W_QB, bf16))
    return pl.pallas_call(
        _inproj_kernel,
        grid=(rows // tm,),
        in_specs=[pl.BlockSpec((tm, d), lambda i: (i, 0)),
                  pl.BlockSpec((1, d), lambda i: (0, 0)),
                  pl.BlockSpec(w_rm.shape, lambda i: (0, 0)),
                  pl.BlockSpec(w_t.shape, lambda i: (0, 0))],
        out_specs=([pl.BlockSpec((tm, w), lambda i: (i, 0)) for w, _ in rm]
                   + [pl.BlockSpec((n, tm), lambda i: (0, i)) for n, _ in tr]
                   + [pl.BlockSpec((sub_tiles, W_VT, TK), lambda i: (i, 0, 0))]),
        out_shape=([jax.ShapeDtypeStruct((rows, w), dt) for w, dt in rm]
                   + [jax.ShapeDtypeStruct((n, rows), dt) for n, dt in tr]
                   + [jax.ShapeDtypeStruct((rows // TK, W_VT, TK), bf16)]),
        compiler_params=_cparams("parallel"),
    )(x2, g, w_rm, w_t)


def _loop_by_pairs(n, body, init):
    acc = lax.fori_loop(0, n // 2, lambda i, a: body(2 * i + 1, body(2 * i, a)), init)
    return lax.cond(n % 2 == 1, lambda a: body(n - 1, a), lambda a: a, acc)


def _count_ge(sc_ref, nkb, p):
    def body(kb, acc):
        hit = jnp.where(sc_ref[kb] >= p, 1.0, 0.0)
        return acc + jnp.sum(hit.reshape(TK // COUNT_ROWS, COUNT_ROWS, TQ), axis=0)
    acc = _loop_by_pairs(nkb, body, jnp.zeros((COUNT_ROWS, TQ), f32))
    return jnp.sum(acc, axis=0, keepdims=True)


def _count_ge16(sc16_ref, nkb, p16):
    one, nil = jnp.ones((), bf16), jnp.zeros((), bf16)
    def body(kb, acc):
        hit = jnp.where(sc16_ref[kb] >= p16, one, nil).reshape(TK // COUNT_ROWS, COUNT_ROWS, TQ)
        for i in range(TK // COUNT_ROWS):
            acc = acc + hit[i]
        return acc
    acc = _loop_by_pairs(nkb, body, jnp.zeros((COUNT_ROWS, TQ), bf16))
    return jnp.sum(acc.astype(f32), axis=0, keepdims=True)


def _coarse_bracket(sc16_ref, nkb, rmin, rmax, active, kf):
    rmax16 = rmax.astype(bf16).astype(f32)

    def pivot(lo16, hi16):
        mid = ((lo16 + jnp.minimum(hi16, rmax16)) * 0.5).astype(bf16).astype(f32)
        return mid, active & (mid > lo16) & (mid < hi16)

    def body(_, st):
        lo16, hi16 = st
        mid, ok = pivot(lo16, hi16)
        ge = _count_ge16(sc16_ref, nkb, mid.astype(bf16)) >= kf
        return jnp.where(ok & ge, mid, lo16), jnp.where(ok & jnp.logical_not(ge), mid, hi16)

    lo16, hi16 = lax.fori_loop(
        0, COARSE_ITERS, body, (rmin.astype(bf16).astype(f32), jnp.full_like(rmin, BIG_SCORE16)))
    below = jnp.maximum(jnp.abs(lo16) * 2.0 ** -7, 1e-37)
    return lo16 - below, jnp.where(hi16 < BIG_SCORE16, hi16, BIG_SCORE)


def _select_topk(sc_ref, sc16_ref, nkb, rmin, rmax, nvalid, k):
    kf = float(k)
    active = nvalid > kf
    lo_c, hi_c = _coarse_bracket(sc16_ref, nkb, rmin, rmax, active, kf)
    lo0 = jnp.where(active, lo_c, NEG_SCORE)
    cl0 = jnp.where(active, nvalid, kf)
    hi0 = jnp.where(active, hi_c, BIG_SCORE)
    zero = jnp.zeros_like(lo0)

    def pending(cl, tie):
        return (cl != kf) & (tie == 0.0)

    def any_pending(st):
        return jnp.max(jnp.where(pending(st[3], st[5]), 1.0, 0.0)) > 0.0

    def bisect(st):
        it, lo, hi, cl, ch, tie = st
        pend = pending(cl, tie)
        hie = jnp.minimum(hi, rmax)
        p = lo + (hie - lo) * 0.5
        p = jnp.where(p > lo, p, hie)
        cnt = _count_ge(sc_ref, nkb, p)
        ge = cnt >= kf
        up_lo = pend & ge
        up_hi = pend & jnp.logical_not(ge)
        lo = jnp.where(up_lo, p, lo)
        cl = jnp.where(up_lo, cnt, cl)
        hi = jnp.where(up_hi, p, hi)
        ch = jnp.where(up_hi, cnt, ch)
        return it + 1, lo, hi, cl, ch, tie

    def tie_check(st):
        it, lo, hi, cl, ch, tie = st
        def scan(kb, c):
            dmin, dmax = c
            s = sc_ref[kb]
            dmin = jnp.minimum(dmin, jnp.min(jnp.where(s >= lo, s, BIG_SCORE), axis=0, keepdims=True))
            dmax = jnp.maximum(dmax, jnp.max(jnp.where(s < hi, s, NEG_SCORE), axis=0, keepdims=True))
            return dmin, dmax
        dmin, dmax = lax.fori_loop(
            0, nkb, scan, (jnp.full((1, TQ), BIG_SCORE, f32), jnp.full((1, TQ), NEG_SCORE, f32)))
        new_tie = pending(cl, tie) & (dmin == dmax)
        return it, jnp.where(new_tie, dmin, lo), hi, cl, ch, jnp.where(new_tie, 1.0, tie)

    st = (jnp.int32(0), lo0, hi0, cl0, zero, zero)
    st = lax.fori_loop(0, FINE_FIXED_ITERS, lambda _, st: bisect(st), st)
    st = lax.while_loop(lambda st: any_pending(st) & (st[0] < TIE_CHECK_START), bisect, st)
    st = lax.while_loop(lambda st: any_pending(st) & (st[0] < MAX_SEARCH_ITERS),
                        lambda st: bisect(tie_check(st)), st)
    _, lo, hi, cl, ch, tie = st
    any_tie = jnp.max(tie) > 0.0

    @pl.when(any_tie)
    def _():
        r = lax.broadcasted_iota(jnp.int32, (TK, TK), 0)
        c = lax.broadcasted_iota(jnp.int32, (TK, TK), 1)
        tril = jnp.where(c <= r, 1.0, 0.0).astype(bf16)
        above = lax.fori_loop(
            0, nkb, lambda kb, n: n + jnp.sum(jnp.where(sc_ref[kb] > lo, 1.0, 0.0), axis=0, keepdims=True), zero)
        need = kf - above

        def body(kb, run):
            s = sc_ref[kb]
            eq = s == lo
            eqf = jnp.where(eq, 1.0, 0.0)
            cum = run + jnp.dot(tril, eqf.astype(bf16), preferred_element_type=f32)
            sel_tie = jnp.where((s > lo) | (eq & (cum <= need)), 1.0, -1.0)
            sel_all = jnp.where(s >= lo, 1.0, -1.0)
            sc_ref[kb] = jnp.where(tie > 0.0, sel_tie, sel_all)
            return run + jnp.sum(eqf, axis=0, keepdims=True)

        lax.fori_loop(0, nkb, body, zero)

    return jnp.where(any_tie, 0.0, lo)


def _rank_select(score, k):
    n, width = score.shape
    tiles = [score[v * SUBLANES:(v + 1) * SUBLANES] for v in range(n // SUBLANES)]
    ridx = lax.broadcasted_iota(jnp.int32, (SUBLANES, width), 0)
    ranks = [jnp.zeros((SUBLANES, width), f32) for _ in tiles]
    for i in range(n):
        si = jnp.broadcast_to(tiles[i // SUBLANES][i % SUBLANES:i % SUBLANES + 1], (SUBLANES, width))
        for v, x in enumerate(tiles):
            if v * SUBLANES > i:
                inc = jnp.where(si >= x, 1.0, 0.0)
            elif (v + 1) * SUBLANES <= i:
                inc = jnp.where(si > x, 1.0, 0.0)
            else:
                inc = jnp.where(ridx > i % SUBLANES, jnp.where(si >= x, 1.0, 0.0), jnp.where(si > x, 1.0, 0.0))
            ranks[v] = ranks[v] + inc
    flags = [jnp.where(r < float(k), 1.0, 0.0) for r in ranks]
    flags.append(jnp.zeros((LANES - n, width), f32))
    return jnp.concatenate(flags, axis=0)


Stream = collections.namedtuple("Stream", "q keys vT madd bias slot")


def _attn_init(m_ref, acc_ref):
    m_ref[...] = jnp.full(m_ref.shape, M_INIT, f32)
    acc_ref[...] = jnp.zeros(acc_ref.shape, f32)


def _attn_streams(streams, m_ref, acc_ref):
    def logits(i):
        st = streams[i]
        s = jnp.dot(st.keys, st.q, preferred_element_type=f32).astype(bf16)
        if st.bias is not None:
            s = s + st.bias
        if st.madd is not None:
            s = s + st.madd
        return s

    def values(i, alpha, p):
        st = streams[i]
        acc_ref[st.slot] = alpha * acc_ref[st.slot] + jnp.dot(st.vT, p, preferred_element_type=f32)

    n = len(streams)
    pending = {i: logits(i) for i in range(min(QK_LOOKAHEAD, n))}
    deferred = []
    for i in range(n):
        if i + QK_LOOKAHEAD < n:
            pending[i + QK_LOOKAHEAD] = logits(i + QK_LOOKAHEAD)
        s = pending.pop(i)
        slot = streams[i].slot
        m_old = m_ref[slot]
        m_new = jnp.maximum(m_old, jnp.max(s, axis=0, keepdims=True).astype(f32))
        alpha = jnp.exp2(m_old - m_new)
        p = jnp.exp2(s - m_new.astype(bf16))
        m_ref[slot] = m_new
        deferred.append((i, alpha, p))
        if len(deferred) > PV_DELAY:
            values(*deferred.pop(0))
    for d in deferred:
        values(*d)


def _attn_out(slot, acc_ref):
    acc = acc_ref[slot]
    return acc[0:HEAD_DIM] * (1.0 / jnp.maximum(acc[HEAD_DIM:HEAD_DIM + 1], 1e-30))


def _with_ones(vT):
    return jnp.concatenate([vT, jnp.ones((BF16_ROWS, vT.shape[1]), vT.dtype)], axis=0)


def _mask_add(mask):
    return jnp.where(mask, 0.0, NEG_MASK).astype(bf16)


def _pad_rows(x, start):
    parts = []
    if start:
        parts.append(jnp.zeros((start, x.shape[1]), x.dtype))
    parts.append(x)
    if LANES - start - x.shape[0]:
        parts.append(jnp.zeros((LANES - start - x.shape[0], x.shape[1]), x.dtype))
    return jnp.concatenate(parts, axis=0)


def _far_blocks(n_far, run):
    def body(i, _):
        run([FAR_GROUP * i + u for u in range(FAR_GROUP)])
        return 0
    lax.fori_loop(0, n_far // FAR_GROUP, body, 0)
    rem = n_far % FAR_GROUP
    start = n_far - rem
    size = FAR_GROUP // 2
    while size >= 1:
        @pl.when((rem & size) != 0)
        def _(start=start, size=size):
            run([start + u for u in range(size)])
        start = start + (rem & size)
        size //= 2


def _dsa_kernel(qiT_ref, miscT_ref, keys_ref, qaT_ref, vT_ref, bias_ref, o_ref,
                sc_ref, sc16_ref, qi_ref, qa_ref, oT_ref, m_ref, acc_ref, *, k_top):
    j = pl.program_id(1)
    w_scale = (N_IDX_HEADS ** -0.5) * (IDX_DIM ** -0.5)
    w = miscT_ref[0:N_IDX_HEADS, :] * w_scale
    for h in range(N_IDX_HEADS):
        qi_ref[h] = _pad_rows(qiT_ref[h * IDX_DIM:(h + 1) * IDX_DIM, :], HEAD_DIM)
    for h in range(N_HEADS_A):
        qa_ref[h] = _pad_rows(qaT_ref[h * HEAD_DIM:(h + 1) * HEAD_DIM, :], 0)
    krow = lax.broadcasted_iota(jnp.int32, (TK, TQ), 0)
    qcol = lax.broadcasted_iota(jnp.int32, (TK, TQ), 1)
    causal = krow <= qcol

    def keys_blk(kb):
        return keys_ref[pl.ds(pl.multiple_of(kb * TK, TK), TK), :]

    def score_blk(kb):
        keys = keys_blk(kb)
        acc = None
        for h in range(N_IDX_HEADS):
            sh = jnp.dot(keys, qi_ref[h], preferred_element_type=f32)
            term = w[h:h + 1, :] * jnp.maximum(sh, 0.0)
            acc = term if acc is None else acc + term
        return acc

    def far_score(kb, carry):
        mn, mx = carry
        a = score_blk(kb)
        sc_ref[kb] = a
        sc16_ref[kb] = a.astype(bf16)
        return (jnp.minimum(mn, jnp.min(a, axis=0, keepdims=True)),
                jnp.maximum(mx, jnp.max(a, axis=0, keepdims=True)))

    def diag_score(carry):
        mn, mx = carry
        a = score_blk(j)
        a_diag = jnp.where(causal, a, NEG_SCORE)
        sc_ref[j] = a_diag
        sc16_ref[j] = a_diag.astype(bf16)
        return (jnp.minimum(mn, jnp.min(jnp.where(causal, a, BIG_SCORE), axis=0, keepdims=True)),
                jnp.maximum(mx, jnp.max(a_diag, axis=0, keepdims=True)))

    def far_group(start, size, c):
        for u in range(size):
            c = far_score(start + u, c)
        return c

    stats = lax.fori_loop(0, j // FAR_GROUP, lambda i, c: far_group(FAR_GROUP * i, FAR_GROUP, c),
                          (jnp.full((1, TQ), BIG_SCORE, f32), jnp.full((1, TQ), NEG_SCORE, f32)))
    rem = j % FAR_GROUP
    start = j - rem
    size = FAR_GROUP // 2
    while size >= 2:
        stats = lax.cond((rem & size) != 0, functools.partial(far_group, start, size), lambda c: c, stats)
        start = start + (rem & size)
        size //= 2
    rmin, rmax = lax.cond((rem & 1) != 0, lambda c: diag_score(far_score(j - 1, c)), diag_score, stats)
    nvalid = (j * TQ + qcol[0:1, :] + 1).astype(f32)
    lo = _select_topk(sc_ref, sc16_ref, j + 1, rmin, rmax, nvalid, k_top)

    _attn_init(m_ref, acc_ref)

    def block_streams(kb, mask, bias_rows):
        keys, vT, madd = keys_blk(kb), _with_ones(vT_ref[kb]), _mask_add(mask)
        return [Stream(qa_ref[h], keys, vT, madd,
                       None if bias_rows is None else bias_ref[h, bias_rows, :], h) for h in range(N_HEADS_A)]

    def far(kb):
        return block_streams(kb, sc_ref[kb] >= lo, None)

    _far_blocks(jnp.maximum(j - 1, 0),
                lambda kbs: _attn_streams([st for kb in kbs for st in far(kb)], m_ref, acc_ref))
    diag_mask = (sc_ref[j] >= lo) & causal

    @pl.when(j >= 1)
    def _():
        _attn_streams(block_streams(j - 1, sc_ref[j - 1] >= lo, slice(0, TK))
                      + block_streams(j, diag_mask, slice(TK, 2 * TK)), m_ref, acc_ref)

    @pl.when(j == 0)
    def _():
        _attn_streams(block_streams(j, diag_mask, slice(TK, 2 * TK)), m_ref, acc_ref)

    for h in range(N_HEADS_A):
        oT_ref[h * HEAD_DIM:(h + 1) * HEAD_DIM, :] = _attn_out(h, acc_ref)
    o_ref[...] = oT_ref[...].T.astype(o_ref.dtype)


def _attn_scratch(n_slots):
    return [pltpu.VMEM((n_slots, 1, TQ), f32), pltpu.VMEM((n_slots, ACC_ROWS, TQ), f32)]


def _dsa(qiT, miscT, keys, qaT, vT_blocks, biasT, B, S, k_top):
    nq = S // TQ
    nkb = S // TK
    col_spec = lambda n: pl.BlockSpec((n, TQ), lambda b, j: (0, b * nq + j))
    return pl.pallas_call(
        functools.partial(_dsa_kernel, k_top=k_top),
        grid=(B, nq),
        in_specs=[col_spec(W_QI), col_spec(W_MISC),
                  pl.BlockSpec((S, LANES), lambda b, j: (b, 0)),
                  col_spec(W_QA),
                  pl.BlockSpec((None, nkb, HEAD_DIM, TK), lambda b, j: (b, 0, 4, 0)),
                  pl.BlockSpec(biasT.shape, lambda b, j: (0, 0, 0))],
        out_specs=pl.BlockSpec((TQ, W_QA), lambda b, j: (b * nq + j, 0)),
        out_shape=jax.ShapeDtypeStruct((B * S, W_QA), bf16),
        scratch_shapes=[pltpu.VMEM((nkb, TK, TQ), f32),
                        pltpu.VMEM((nkb, TK, TQ), bf16),
                        pltpu.VMEM((N_IDX_HEADS, LANES, TQ), bf16),
                        pltpu.VMEM((N_HEADS_A, LANES, TQ), bf16),
                        pltpu.VMEM((W_QA, TQ), f32)] + _attn_scratch(N_HEADS_A),
        compiler_params=_cparams("parallel", "parallel"),
    )(qiT, miscT, keys, qaT, vT_blocks, biasT)


def _compress_kernel(xk_ref, xv_ref, pe_ref, w1_ref, w1tok_ref, w2bd_ref, w2bdT_ref, ck_ref, cvT_ref):
    nc = ck_ref.shape[0]
    G = N_KV_GROUPS_B

    def hidden(kind):
        x_ref = (xk_ref, xv_ref)[kind]
        const = jnp.dot(pe_ref[kind], w1_ref[kind], preferred_element_type=f32)[0:1]
        ab = jnp.zeros((nc, 2 * G * CMP_HIDDEN), f32)
        for i in range(CMP_STRIDE):
            tok = x_ref[pl.ds(i, nc, stride=CMP_STRIDE), :]
            ab = ab + jnp.dot(tok.astype(bf16), w1tok_ref[kind, i], preferred_element_type=f32)
        hs = []
        for g in range(G):
            first = ab[:, g * CMP_HIDDEN:(g + 1) * CMP_HIDDEN]
            second = pltpu.roll(ab[:, (G + g) * CMP_HIDDEN:(G + g + 1) * CMP_HIDDEN], nc - 1, 0)
            hs.append(jax.nn.gelu(first + second + const).astype(bf16))
        return jnp.concatenate(hs, axis=1)

    ck_ref[...] = jnp.dot(hidden(0), w2bd_ref[...], preferred_element_type=f32).astype(ck_ref.dtype)
    cvT_ref[...] = lax.dot_general(w2bdT_ref[...], hidden(1), NT_DIMS,
                                   preferred_element_type=f32).astype(cvT_ref.dtype)


def _compress(kcv, pe8, w1, w1tok, w2bd_k, w2bdT_v, B, S):
    nc = S // CMP_STRIDE
    cs = lambda a: pl.BlockSpec(a.shape, lambda b: (0,) * a.ndim)
    return pl.pallas_call(
        _compress_kernel,
        grid=(B,),
        in_specs=[pl.BlockSpec((S, LANES), lambda b: (b, 0)),
                  pl.BlockSpec((S, LANES), lambda b: (b, 1)),
                  cs(pe8), cs(w1), cs(w1tok), cs(w2bd_k), cs(w2bdT_v)],
        out_specs=[pl.BlockSpec((None, nc, LANES), lambda b: (b, 0, 0)),
                   pl.BlockSpec((None, LANES, nc), lambda b: (b, 0, 0))],
        out_shape=[jax.ShapeDtypeStruct((B, nc, LANES), bf16), jax.ShapeDtypeStruct((B, LANES, nc), bf16)],
        compiler_params=_cparams("parallel"),
    )(kcv, kcv, pe8, w1, w1tok, w2bd_k, w2bdT_v)


def _nsa_kernel(qbT_ref, miscT_ref, ck_ref, cvT_ref, ksw_ref, vT_ref, bias_ref, bcmp_ref, ovT_ref, e3_ref,
                o_ref, qb_ref, s_ref, oT_ref, m_ref, acc_ref, *, n_sel):
    j = pl.program_id(1)
    nc = ck_ref.shape[0]
    n_s = ovT_ref.shape[0]
    R = HEADS_PER_GROUP
    G = N_KV_GROUPS_B
    H = N_HEADS_B
    for h in range(H):
        qb_ref[h] = _pad_rows(qbT_ref[h * HEAD_DIM:(h + 1) * HEAD_DIM, :], (h // R) * HEAD_DIM)
    krow = lax.broadcasted_iota(jnp.int32, (TK, TQ), 0)
    qcol = lax.broadcasted_iota(jnp.int32, (TK, TQ), 1)
    causal = krow <= qcol
    t_c = j * TQ + lax.broadcasted_iota(jnp.int32, (nc, TQ), 1)
    c_c = lax.broadcasted_iota(jnp.int32, (nc, TQ), 0)
    mask_c = c_c * CMP_STRIDE + (CMP_BLOCK - 1) <= t_c
    blk = lax.broadcasted_iota(jnp.int32, (n_s, TQ), 0)
    cur = (j * TQ + lax.broadcasted_iota(jnp.int32, (n_s, TQ), 1)) // SLC_BLOCK
    valid = blk <= cur
    forced = (blk == 0) | (blk == cur) | (blk == cur - 1)
    gates = jax.nn.sigmoid(miscT_ref[N_IDX_HEADS:N_IDX_HEADS + W_GB, :])
    win_blocks = WINDOW // TK
    cmp_per_q = TQ // CMP_STRIDE
    half = CMP_WIN // 2

    for h in range(H):
        s_ref[h] = jnp.dot(ck_ref[...], qb_ref[h], preferred_element_type=f32)

    @pl.when(j == 0)
    def _():
        for h in range(H):
            s_ref[h, 0:half, :] += bcmp_ref[h, half:CMP_WIN, :]

    @pl.when(j > 0)
    def _():
        rows = pl.ds(pl.multiple_of(j * cmp_per_q - half, half), CMP_WIN)
        for h in range(H):
            s_ref[h, rows, :] += bcmp_ref[h]

    o_c = []
    psum = [jnp.zeros((nc, TQ), f32) for _ in range(G)]
    for h in range(H):
        g = h // R
        s = jnp.where(mask_c, s_ref[h], NEG_MASK)
        m = jnp.max(s, axis=0, keepdims=True)
        e = jnp.where(mask_c, jnp.exp2(s - m), 0.0)
        p = e * (1.0 / jnp.maximum(jnp.sum(e, axis=0, keepdims=True), 1e-30))
        psum[g] = psum[g] + p
        o_c.append(jnp.dot(cvT_ref[g * HEAD_DIM:(g + 1) * HEAD_DIM, :], p.astype(bf16),
                           preferred_element_type=f32))

    selT = []
    for g in range(G):
        p_hi = psum[g].astype(bf16)
        p_lo = (psum[g] - p_hi.astype(f32)).astype(bf16)
        imp = (jnp.dot(ovT_ref[...], p_hi, preferred_element_type=f32)
               + jnp.dot(ovT_ref[...], p_lo, preferred_element_type=f32))
        score = jnp.where(valid, jnp.where(forced, FORCED_SCORE, imp), NEG_SCORE)
        selT.append(_rank_select(score, n_sel).astype(bf16))

    _attn_init(m_ref, acc_ref)

    def branch_streams(kb, lanes, v_row0, masks, bias_rows, slot0):
        keys = ksw_ref[pl.ds(pl.multiple_of(kb * TK, TK), TK), lanes]
        out = []
        for g in range(G):
            vT = _with_ones(vT_ref[kb, v_row0 + g * HEAD_DIM:v_row0 + (g + 1) * HEAD_DIM, :])
            madd = None if masks[g] is None else _mask_add(masks[g])
            for h in range(g * R, (g + 1) * R):
                out.append(Stream(qb_ref[h], keys, vT, madd,
                                  None if bias_rows is None else bias_ref[h, bias_rows, :], slot0 + h))
        return out

    def tok_masks(kb, extra=None):
        ms = [jnp.dot(e3_ref[kb], selT[g], preferred_element_type=f32) > 0.5 for g in range(G)]
        return ms if extra is None else [m & extra for m in ms]

    def sel_streams(kb, extra=None, bias_rows=None):
        return branch_streams(kb, slice(0, LANES), 0, tok_masks(kb, extra), bias_rows, 0)

    def win_streams(kb, mask=None, bias_rows=None):
        return branch_streams(kb, slice(LANES, 2 * LANES), G * HEAD_DIM, [mask] * G, bias_rows, H)

    _far_blocks(jnp.maximum(j - 1, 0),
                lambda kbs: _attn_streams([st for kb in kbs for st in sel_streams(kb)], m_ref, acc_ref))

    prev_rows, diag_rows = slice(0, TK), slice(TK, 2 * TK)

    def near_streams(n_before):
        out = []
        for d in range(min(n_before, win_blocks), 1, -1):
            out += win_streams(j - d, (krow > qcol) if d == win_blocks else None)
        if n_before >= 1:
            out += sel_streams(j - 1, None, prev_rows)
            out += win_streams(j - 1, (krow > qcol) if win_blocks == 1 else None, prev_rows)
        return out + sel_streams(j, causal, diag_rows) + win_streams(j, causal, diag_rows)

    for n_before in range(win_blocks + 1):
        @pl.when((j >= n_before) if n_before == win_blocks else (j == n_before))
        def _(n_before=n_before):
            _attn_streams(near_streams(n_before), m_ref, acc_ref)

    for h in range(H):
        oT_ref[h * HEAD_DIM:(h + 1) * HEAD_DIM, :] = (
            gates[h:h + 1, :] * o_c[h]
            + gates[H + h:H + h + 1, :] * _attn_out(h, acc_ref)
            + gates[2 * H + h:2 * H + h + 1, :] * _attn_out(H + h, acc_ref))
    o_ref[...] = oT_ref[...].T.astype(o_ref.dtype)


def _nsa(qbT, miscT, ck, cvT, ksw, vT_blocks, biasT, bcmpT, ovT, e3T, B, S, n_sel):
    nq = S // TQ
    nkb = S // TK
    nc = ck.shape[1]
    col_spec = lambda n: pl.BlockSpec((n, TQ), lambda b, j: (0, b * nq + j))
    const = lambda a: pl.BlockSpec(a.shape, lambda b, j: (0,) * a.ndim)
    return pl.pallas_call(
        functools.partial(_nsa_kernel, n_sel=n_sel),
        grid=(B, nq),
        in_specs=[col_spec(W_QB), col_spec(W_MISC),
                  pl.BlockSpec((None, nc, LANES), lambda b, j: (b, 0, 0)),
                  pl.BlockSpec((None, LANES, nc), lambda b, j: (b, 0, 0)),
                  pl.BlockSpec((S, 2 * LANES), lambda b, j: (b, 0)),
                  pl.BlockSpec((None, nkb, 4 * HEAD_DIM, TK), lambda b, j: (b, 0, 0, 0)),
                  const(biasT), const(bcmpT), const(ovT), const(e3T)],
        out_specs=pl.BlockSpec((TQ, W_QB), lambda b, j: (b * nq + j, 0)),
        out_shape=jax.ShapeDtypeStruct((B * S, W_QB), bf16),
        scratch_shapes=[pltpu.VMEM((N_HEADS_B, LANES, TQ), bf16),
                        pltpu.VMEM((N_HEADS_B, nc, TQ), f32),
                        pltpu.VMEM((W_QB, TQ), f32)] + _attn_scratch(2 * N_HEADS_B),
        compiler_params=_cparams("parallel", "parallel"),
    )(qbT, miscT, ck, cvT, ksw, vT_blocks, biasT, bcmpT, ovT, e3T)


def _mix_kernel(oa_ref, ob_ref, gate_ref, x_ref, wa_ref, wb_ref, wo_ref, x1_ref):
    d = x_ref.shape[-1]
    ga = jax.nn.sigmoid(gate_ref[:, :d].astype(f32))
    gb = jax.nn.sigmoid(gate_ref[:, d:].astype(f32))
    mix = (ga * jnp.dot(oa_ref[...], wa_ref[...], preferred_element_type=f32)
           + gb * jnp.dot(ob_ref[...], wb_ref[...], preferred_element_type=f32))
    x1_ref[...] = x_ref[...] + jnp.dot(mix.astype(bf16), wo_ref[...], preferred_element_type=f32)


def _mix(oa, ob, gate, x2, wa, wb, wo, tm=512):
    rows, d = x2.shape
    rs = lambda w: pl.BlockSpec((tm, w), lambda i: (i, 0))
    cs = lambda a: pl.BlockSpec(a.shape, lambda i: (0, 0))
    return pl.pallas_call(
        _mix_kernel,
        grid=(rows // tm,),
        in_specs=[rs(W_QA), rs(W_QB), rs(2 * d), rs(d), cs(wa), cs(wb), cs(wo)],
        out_specs=rs(d),
        out_shape=jax.ShapeDtypeStruct((rows, d), f32),
        compiler_params=_cparams("parallel"),
    )(oa, ob, gate, x2, wa, wb, wo)


def _mix_mlp_kernel(oa_ref, ob_ref, gate_ref, x_ref, wa_ref, wb_ref, wo_ref, g_ref, w1_ref, w2_ref, gf_ref,
                    o_ref, *, chunk, sub_rows):
    d = x_ref.shape[-1]
    for r0 in range(0, x_ref.shape[0], sub_rows):
        rs = slice(r0, r0 + sub_rows)
        ga = jax.nn.sigmoid(gate_ref[rs, :d].astype(f32))
        gb = jax.nn.sigmoid(gate_ref[rs, d:].astype(f32))
        mix = (ga * jnp.dot(oa_ref[rs, :], wa_ref[...], preferred_element_type=f32)
               + gb * jnp.dot(ob_ref[rs, :], wb_ref[...], preferred_element_type=f32))
        x1 = x_ref[rs, :] + jnp.dot(mix.astype(bf16), wo_ref[...], preferred_element_type=f32)
        h = _rms(x1, g_ref[...]).astype(bf16)
        acc = jnp.zeros(x1.shape, f32)
        for c0 in range(0, w1_ref.shape[1], chunk):
            u = jnp.dot(h, w1_ref[:, c0:c0 + chunk], preferred_element_type=f32)
            u = jnp.square(jnp.maximum(u, 0.0)).astype(bf16)
            acc = acc + jnp.dot(u, w2_ref[c0:c0 + chunk, :], preferred_element_type=f32)
        o_ref[rs, :] = _rms(x1 + acc, gf_ref[...])


def _mix_mlp(oa, ob, gate, x2, wa, wb, wo, g, w1, w2, gf, tm=512, chunk=512, sub_rows=256):
    rows, d = x2.shape
    rs = lambda w: pl.BlockSpec((tm, w), lambda i: (i, 0))
    cs = lambda a: pl.BlockSpec(a.shape, lambda i: (0, 0))
    ws = lambda a: pl.BlockSpec(a.shape, lambda i: (0, 0), pipeline_mode=pl.Buffered(1))
    return pl.pallas_call(
        functools.partial(_mix_mlp_kernel, chunk=chunk, sub_rows=sub_rows),
        grid=(rows // tm,),
        in_specs=[rs(W_QA), rs(W_QB), rs(2 * d), rs(d), ws(wa), ws(wb), ws(wo), cs(g), ws(w1), ws(w2), cs(gf)],
        out_specs=rs(d),
        out_shape=jax.ShapeDtypeStruct((rows, d), f32),
        compiler_params=_cparams("parallel"),
    )(oa, ob, gate, x2, wa, wb, wo, g, w1, w2, gf)


def _mlp_kernel(x1_ref, g_ref, w1_ref, w2_ref, gf_ref, o_ref, *, chunk, sub_rows):
    for r0 in range(0, x1_ref.shape[0], sub_rows):
        x1 = x1_ref[r0:r0 + sub_rows, :]
        h = _rms(x1, g_ref[...]).astype(bf16)
        acc = jnp.zeros(x1.shape, f32)
        for c0 in range(0, w1_ref.shape[1], chunk):
            u = jnp.dot(h, w1_ref[:, c0:c0 + chunk], preferred_element_type=f32)
            u = jnp.square(jnp.maximum(u, 0.0)).astype(bf16)
            acc = acc + jnp.dot(u, w2_ref[c0:c0 + chunk, :], preferred_element_type=f32)
        o_ref[r0:r0 + sub_rows, :] = _rms(x1 + acc, gf_ref[...])


def _mlp(x1, g, w1, w2, gf, tm=512, chunk=512, sub_rows=256):
    rows, d = x1.shape
    rs = pl.BlockSpec((tm, d), lambda i: (i, 0))
    cs = lambda a: pl.BlockSpec(a.shape, lambda i: (0, 0))
    ws = lambda a: pl.BlockSpec(a.shape, lambda i: (0, 0), pipeline_mode=pl.Buffered(1))
    return pl.pallas_call(
        functools.partial(_mlp_kernel, chunk=chunk, sub_rows=sub_rows),
        grid=(rows // tm,),
        in_specs=[rs, cs(g), ws(w1), ws(w2), cs(gf)],
        out_specs=rs,
        out_shape=jax.ShapeDtypeStruct((rows, d), f32),
        compiler_params=_cparams("parallel"),
    )(x1, g, w1, w2, gf)


def _t5_bucket_np(dist):
    n = np.maximum(dist, 0)
    max_exact = N_BUCKETS // 2
    nf = np.maximum(n, 1).astype(np.float32)
    large = max_exact + (np.log(nf / np.float32(max_exact)) / np.float32(math.log(MAX_DISTANCE / max_exact))
                         * np.float32(N_BUCKETS - max_exact)).astype(np.int32)
    large = np.minimum(large, N_BUCKETS - 1)
    return np.where(n < max_exact, n, large)


def _bias_tables(rel_bias):
    shifted = (rel_bias.astype(f32) - rel_bias[N_BUCKETS - 1].astype(f32)) * LOG2E

    def lookup(dist, tab):
        bucket = np.where((dist >= 0) & (dist < MAX_DISTANCE), _t5_bucket_np(dist), N_BUCKETS - 1)
        onehot = (jnp.asarray(bucket, jnp.int32)[..., None] == jnp.arange(N_BUCKETS)).astype(f32)
        return jnp.einsum('kqb,bh->hkq', onehot, tab, precision=lax.Precision.HIGHEST)

    q = np.arange(TQ)[None, :]
    near = lookup(q + TK - np.arange(2 * TK)[:, None], shifted).astype(bf16)
    m = np.arange(CMP_WIN)[:, None]
    cmp = lookup(q - (CMP_BLOCK - 1) - CMP_STRIDE * (m - CMP_WIN // 2), shifted[:, N_HEADS_A:])
    return near[:N_HEADS_A], near[N_HEADS_A:], cmp


def _static_tables(S, nc):
    n_s = S // SLC_BLOCK
    cmp_start = np.arange(nc)[None, :] * CMP_STRIDE
    slc_start = np.arange(n_s)[:, None] * SLC_BLOCK
    ovT = np.clip(np.minimum(cmp_start + CMP_BLOCK, slc_start + SLC_BLOCK)
                  - np.maximum(cmp_start, slc_start), 0, None).astype(np.float32) / CMP_BLOCK
    nkb = S // TK
    tok_blk = (np.arange(nkb)[:, None, None] * TK + np.arange(TK)[None, :, None]) // SLC_BLOCK
    e3T = (np.arange(LANES)[None, None, :] == tok_blk).astype(np.float32)
    return jnp.asarray(ovT, bf16), jnp.asarray(e3T, bf16)


def _compress_weights(w1, w2):
    z = jnp.zeros((HEAD_DIM, CMP_HIDDEN), w1.dtype)
    first = w1[:CMP_STRIDE * HEAD_DIM].reshape(CMP_STRIDE, HEAD_DIM, CMP_HIDDEN)
    second = w1[CMP_STRIDE * HEAD_DIM:].reshape(CMP_STRIDE, HEAD_DIM, CMP_HIDDEN)
    zz = jnp.broadcast_to(z, first.shape)
    tok = jnp.concatenate([jnp.concatenate([first, zz, second, zz], axis=2),
                           jnp.concatenate([zz, first, zz, second], axis=2)], axis=1)
    z2 = jnp.zeros((CMP_HIDDEN, HEAD_DIM), w2.dtype)
    return tok, jnp.block([[w2, z2], [z2, w2]])


def kernel(x, norm_mix, w_in, cmp_pe_k, cmp_w1_k, cmp_w2_k, cmp_pe_v, cmp_w1_v, cmp_w2_v, rel_bias,
           w_branch_a, w_branch_b, w_out, norm_mlp, w_mlp_in, w_mlp_out, norm_final):
    B, S, D = x.shape
    assert norm_mix.shape[0] == 1 and S % TQ == 0 and (S // CMP_STRIDE) % LANES == 0
    assert WINDOW % TK == 0 and (S // SLC_BLOCK) % SUBLANES == 0 and S // SLC_BLOCK <= LANES
    assert S // COUNT_ROWS <= 256
    G = N_KV_GROUPS_B
    rows = B * S
    nkb = S // TK
    nc = S // CMP_STRIDE
    k_top = min(TOPK_TOKENS, S // 4)
    n_sel = min(N_SLC_BLOCKS, S // SLC_BLOCK)
    x2 = x.reshape(rows, D)

    w = w_in[0]
    o_ka = W_QA
    o_va = o_ka + HEAD_DIM
    o_qi = o_va + HEAD_DIM
    o_ki = o_qi + W_QI
    o_wi = o_ki + IDX_DIM
    o_qb = o_wi + N_IDX_HEADS
    o_kvb = o_qb + W_QB
    o_gb = o_kvb + W_KVB
    o_gate = o_gb + W_GB
    kv = lambda kind: w[:, o_kvb + kind * G * HEAD_DIM:o_kvb + (kind + 1) * G * HEAD_DIM]
    pad = jnp.zeros((D, LANES - HEAD_DIM - IDX_DIM), w.dtype)
    w_rm = jnp.concatenate([w[:, o_ka:o_va], w[:, o_ki:o_wi], pad, kv(0), kv(1), kv(2), kv(4),
                            w[:, o_gate:]], axis=1).astype(bf16)
    w_t = jnp.concatenate([w[:, :o_ka], w[:, o_qi:o_ki], w[:, o_wi:o_qb], w[:, o_gb:o_gate],
                           w[:, o_qb:o_kvb], kv(3), kv(5), w[:, o_va:o_qi]], axis=1).T.astype(bf16)
    keys_a, kcv, ksw, gate, qaT, qiT, miscT, qbT, vT = _inproj(x2, norm_mix[0][None], w_rm, w_t)
    vT_blocks = vT.reshape(B, nkb, W_VT, TK)

    bias_a, bias_b, bcmp = _bias_tables(rel_bias)
    ovT, e3T = _static_tables(S, nc)

    o_a = _dsa(qiT, miscT, keys_a, qaT, vT_blocks, bias_a, B, S, k_top)

    chunk_w = CMP_STRIDE * HEAD_DIM
    pe = jnp.stack([cmp_pe_k[0], cmp_pe_v[0]]).reshape(2, 1, 2 * chunk_w)
    pe8 = jnp.broadcast_to(pe, (2, SUBLANES, 2 * chunk_w)).astype(bf16)
    w1 = jnp.stack([cmp_w1_k[0], cmp_w1_v[0]]).astype(bf16)
    tok_k, w2bd_k = _compress_weights(cmp_w1_k[0], cmp_w2_k[0])
    tok_v, w2bd_v = _compress_weights(cmp_w1_v[0], cmp_w2_v[0])
    ck, cvT = _compress(kcv, pe8, w1, jnp.stack([tok_k, tok_v]).astype(bf16),
                        w2bd_k.astype(bf16), w2bd_v.T.astype(bf16), B, S)

    o_b = _nsa(qbT, miscT, ck, cvT, ksw, vT_blocks, bias_b, bcmp, ovT, e3T, B, S, n_sel)

    out = _mix_mlp(o_a, o_b, gate, x2, w_branch_a[0].astype(bf16), w_branch_b[0].astype(bf16),
                   w_out[0].astype(bf16), norm_mlp[0][None], w_mlp_in[0].astype(bf16),
                   w_mlp_out[0].astype(bf16), norm_final[None])
    return out.reshape(B, S, D)
```
